```python
import math
import jax, jax.numpy as jnp
from jax import lax
import numpy as np

D_MODEL = 1024
BATCH = 2
SEQ = 8192
DEPTH = 2

Q_BLOCK = 128
NEG_INF = -1e30
FORCE_SEL = 1e9
EPS = 1e-6

NSA_HEADS = 8
NSA_KV_GROUPS = 2
NSA_HPG = NSA_HEADS // NSA_KV_GROUPS
NSA_HEAD_DIM = 64
CMP_LEN = 32
CMP_STRIDE = 16
CMP_RATIO = CMP_LEN // CMP_STRIDE
CMP_HIDDEN = 128
SLC_BLOCK = 64
SLC_RATIO = SLC_BLOCK // CMP_STRIDE
SLC_TOPK = 16
SLC_LOCAL = 2
WINDOW = 512

MLA_HEADS = 8
MLA_Q_RANK = 256
MLA_KV_RANK = 128
MLA_NOPE_DIM = 64
MLA_ROPE_DIM = 32
MLA_QK_DIM = MLA_NOPE_DIM + MLA_ROPE_DIM
MLA_V_DIM = 64
ROPE_THETA = 10000.0

FOX_HEADS = 8
FOX_HEAD_DIM = 64

REL_BUCKETS = 32
REL_MAX_DIST = 128

N_BRANCHES = 3
N_GROUPS = 4
EXPERTS_PER_GROUP = 8
N_EXPERTS = N_GROUPS * EXPERTS_PER_GROUP
TOP_K_IN_GROUP = 2
EXPERT_DIM = 256
MOE_CHUNK = 128

NSA_Q_W = NSA_HEADS * NSA_HEAD_DIM
NSA_KV_W = 3 * 2 * NSA_KV_GROUPS * NSA_HEAD_DIM
NSA_G_W = 3 * NSA_HEADS
FOX_QKV_W = 3 * FOX_HEADS * FOX_HEAD_DIM
MERGE_W = N_BRANCHES * D_MODEL
IN_SPLITS = (NSA_Q_W, NSA_KV_W, NSA_G_W, MLA_Q_RANK, MLA_KV_RANK, MLA_ROPE_DIM, FOX_QKV_W, FOX_HEADS, MERGE_W)
D_IN = sum(IN_SPLITS)

kernel_name = "hybrid_nsa_mla_fox_hmoe"


def rmsnorm(x, g):
    xf = x.astype(jnp.float32)
    y = xf * lax.rsqrt(jnp.mean(xf * xf, axis=-1, keepdims=True) + EPS)
    return (y * g.astype(jnp.float32)).astype(x.dtype)


def rel_bucket(dist):
    n = jnp.maximum(dist, 0)
    max_exact = REL_BUCKETS // 2
    log_ratio = jnp.log(jnp.maximum(n, 1).astype(jnp.float32) / max_exact) / math.log(REL_MAX_DIST / max_exact)
    large = max_exact + (log_ratio * (REL_BUCKETS - max_exact)).astype(jnp.int32)
    return jnp.where(n < max_exact, n, jnp.minimum(large, REL_BUCKETS - 1))


def rope(x, pos):
    half = x.shape[-1] // 2
    inv_freq = ROPE_THETA ** (-jnp.arange(half, dtype=jnp.float32) / half)
    ang = pos.astype(jnp.float32)[..., None] * inv_freq
    ang = ang.reshape(ang.shape[:2] + (1,) * (x.ndim - 3) + (half,))
    cos, sin = jnp.cos(ang), jnp.sin(ang)
    x1 = x[..., :half].astype(jnp.float32)
    x2 = x[..., half:].astype(jnp.float32)
    return jnp.concatenate([x1 * cos - x2 * sin, x1 * sin + x2 * cos], axis=-1).astype(x.dtype)


def compress_blocks(raw, pe, w1, b1, w2):
    B, T, G, dh = raw.shape
    n_sub = T // CMP_STRIDE
    sub = raw.reshape(B, n_sub, CMP_STRIDE, G, dh)
    blocks = jnp.concatenate([sub[:, r:n_sub - CMP_RATIO + 1 + r] for r in range(CMP_RATIO)], axis=2)
    blocks = blocks + pe[None, None, :, None, :]
    flat = blocks.transpose(0, 1, 3, 2, 4).reshape(B, n_sub - CMP_RATIO + 1, G, CMP_LEN * dh)
    return jax.nn.gelu(flat @ w1 + b1) @ w2


def causal_block_attention(q, k, v, scale, cum_log_f=None):
    B, T, H, _ = q.shape
    kpos = jnp.arange(T)
    cum_k = None if cum_log_f is None else cum_log_f.transpose(0, 2, 1)[:, :, None, :]

    def one_block(qb):
        t0 = qb * Q_BLOCK
        qpos = t0 + jnp.arange(Q_BLOCK)
        q_blk = lax.dynamic_slice_in_dim(q, t0, Q_BLOCK, axis=1)
        s = jnp.einsum('bqhd,bkhd->bhqk', q_blk, k, preferred_element_type=jnp.float32) * scale
        if cum_log_f is not None:
            cum_q = lax.dynamic_slice_in_dim(cum_log_f, t0, Q_BLOCK, axis=1).transpose(0, 2, 1)[..., None]
            s = s + cum_q - cum_k
        s = jnp.where(kpos[None, :] <= qpos[:, None], s, NEG_INF)
        p = jax.nn.softmax(s, axis=-1).astype(v.dtype)
        return jnp.einsum('bhqk,bkhd->bqhd', p, v)

    out = lax.map(one_block, jnp.arange(T // Q_BLOCK))
    return out.transpose(1, 0, 2, 3, 4).reshape(B, T, H, v.shape[-1])


def nsa_attention(q, kc_raw, vc_raw, ks, vs, kw, vw, gate_logits, rel_bias, pe,
                  ck_w1, ck_b1, ck_w2, cv_w1, cv_b1, cv_w2):
    B, T, H, dh = q.shape
    G, HPG = NSA_KV_GROUPS, NSA_HPG
    scale = dh ** -0.5
    kc = compress_blocks(kc_raw, pe, ck_w1, ck_b1, ck_w2)
    vc = compress_blocks(vc_raw, pe, cv_w1, cv_b1, cv_w2)
    n_cmp = kc.shape[1]
    cmp_end = jnp.arange(n_cmp) * CMP_STRIDE + CMP_LEN - 1
    n_slc = T // SLC_BLOCK
    ks_blk = ks.reshape(B, n_slc, SLC_BLOCK, G, dh).transpose(0, 3, 1, 2, 4)
    vs_blk = vs.reshape(B, n_slc, SLC_BLOCK, G, dh).transpose(0, 3, 1, 2, 4)
    topk = min(SLC_TOPK, n_slc)
    max_start = SLC_RATIO + CMP_RATIO - 2
    right_pad = max(0, max_start + SLC_RATIO * (n_slc - 1) + 1 - n_cmp)
    kw_pad = jnp.pad(kw, ((0, 0), (WINDOW, 0), (0, 0), (0, 0)))
    vw_pad = jnp.pad(vw, ((0, 0), (WINDOW, 0), (0, 0), (0, 0)))
    gates = jax.nn.sigmoid(gate_logits.astype(jnp.float32))
    tab_g = rel_bias.reshape(REL_BUCKETS, G, HPG)
    n_win = WINDOW + Q_BLOCK

    def one_block(qb):
        t0 = qb * Q_BLOCK
        tq = t0 + jnp.arange(Q_BLOCK)
        qg = lax.dynamic_slice_in_dim(q, t0, Q_BLOCK, axis=1).reshape(B, Q_BLOCK, G, HPG, dh)

        dist_c = tq[:, None] - cmp_end[None, :]
        ok_c = dist_c >= 0
        bias_c = rel_bias[rel_bucket(dist_c)].reshape(Q_BLOCK, n_cmp, G, HPG).transpose(2, 3, 0, 1)
        s_c = jnp.einsum('bqghd,bngd->bghqn', qg, kc, preferred_element_type=jnp.float32) * scale + bias_c
        s_c = jnp.where(ok_c, s_c, NEG_INF)
        p_c = jnp.where(ok_c, jax.nn.softmax(s_c, axis=-1), 0.0)
        o_c = jnp.einsum('bghqn,bngd->bqghd', p_c.astype(vc.dtype), vc)

        imp = jnp.pad(p_c.sum(axis=2), ((0, 0), (0, 0), (0, 0), (0, right_pad)))
        p_slc = jnp.zeros(imp.shape[:3] + (n_slc,), jnp.float32)
        for m in range(SLC_RATIO):
            for n in range(CMP_RATIO):
                st = m + n
                p_slc = p_slc + lax.slice_in_dim(imp, st, st + SLC_RATIO * (n_slc - 1) + 1, stride=SLC_RATIO, axis=3)
        blk_cur = tq // SLC_BLOCK
        j = jnp.arange(n_slc)
        causal_b = j[None, :] <= blk_cur[:, None]
        forced = (j[None, :] == 0) | (causal_b & (j[None, :] > blk_cur[:, None] - SLC_LOCAL))
        score = jnp.where(causal_b, jnp.where(forced, FORCE_SEL, p_slc), NEG_INF)
        _, idx = lax.top_k(score, topk)
        sel_ok = idx <= blk_cur[None, None, :, None]
        flat_idx = idx.reshape(B, G, Q_BLOCK * topk)[..., None, None]
        k_sel = jnp.take_along_axis(ks_blk, flat_idx, axis=2).reshape(B, G, Q_BLOCK, topk * SLC_BLOCK, dh)
        v_sel = jnp.take_along_axis(vs_blk, flat_idx, axis=2).reshape(B, G, Q_BLOCK, topk * SLC_BLOCK, dh)
        kpos_s = idx[..., None] * SLC_BLOCK + jnp.arange(SLC_BLOCK)
        ok_s = ((kpos_s <= tq[None, None, :, None, None]) & sel_ok[..., None]).reshape(B, G, Q_BLOCK, -1)
        dist_s = tq[None, None, :, None] - kpos_s.reshape(B, G, Q_BLOCK, -1)
        bias_s = jax.vmap(lambda tb, bk: tb[bk], in_axes=(1, 1), out_axes=1)(tab_g, rel_bucket(dist_s))
        bias_s = bias_s.transpose(0, 1, 4, 2, 3)
        s_s = jnp.einsum('bqghd,bgqkd->bghqk', qg, k_sel, preferred_element_type=jnp.float32) * scale + bias_s
        s_s = jnp.where(ok_s[:, :, None], s_s, NEG_INF)
        p_s = jax.nn.softmax(s_s, axis=-1).astype(v_sel.dtype)
        o_s = jnp.einsum('bghqk,bgqkd->bqghd', p_s, v_sel)

        k_win = lax.dynamic_slice_in_dim(kw_pad, t0, n_win, axis=1)
        v_win = lax.dynamic_slice_in_dim(vw_pad, t0, n_win, axis=1)
        kpos_w = t0 - WINDOW + jnp.arange(n_win)
        dist_w = tq[:, None] - kpos_w[None, :]
        ok_w = (dist_w >= 0) & (dist_w < WINDOW) & (kpos_w[None, :] >= 0)
        bias_w = rel_bias[rel_bucket(dist_w)].reshape(Q_BLOCK, n_win, G, HPG).transpose(2, 3, 0, 1)
        s_w = jnp.einsum('bqghd,bkgd->bghqk', qg, k_win, preferred_element_type=jnp.float32) * scale + bias_w
        s_w = jnp.where(ok_w, s_w, NEG_INF)
        p_w = jax.nn.softmax(s_w, axis=-1).astype(v_win.dtype)
        o_w = jnp.einsum('bghqk,bkgd->bqghd', p_w, v_win)

        g = lax.dynamic_slice_in_dim(gates, t0, Q_BLOCK, axis=1).reshape(B, Q_BLOCK, G, HPG, 3)
        out = g[..., 0:1] * o_c + g[..., 1:2] * o_s + g[..., 2:3] * o_w
        return out.reshape(B, Q_BLOCK, H * dh).astype(q.dtype)

    out = lax.map(one_block, jnp.arange(T // Q_BLOCK))
    return out.transpose(1, 0, 2, 3).reshape(B, T, H * dh)


def hier_moe(h, w_grp, b_grp, w_router, b_router, w_gate, w_up, w_down):
    B, T, D = h.shape
    tok = h.reshape(-1, MOE_CHUNK, D)

    def chunk(t):
        grp_p = jax.nn.softmax((t @ w_grp).astype(jnp.float32) + b_grp.astype(jnp.float32), axis=-1)
        grp_top, g_idx = lax.top_k(grp_p, 1)
        e_logit = ((t @ w_router).astype(jnp.float32) + b_router.astype(jnp.float32)).reshape(-1, N_GROUPS, EXPERTS_PER_GROUP)
        e_logit = jnp.take_along_axis(e_logit, g_idx[:, :, None], axis=1)[:, 0]
        e_p = jax.nn.softmax(e_logit, axis=-1)
        e_top, e_idx = lax.top_k(e_p, TOP_K_IN_GROUP)
        wts = grp_top * e_top / jnp.sum(e_top, axis=-1, keepdims=True)
        eid = g_idx * EXPERTS_PER_GROUP + e_idx
        combine = jnp.sum(jax.nn.one_hot(eid, N_EXPERTS, dtype=jnp.float32) * wts[..., None], axis=1)
        a = jnp.einsum('cd,edf->cef', t, w_gate)
        u = jnp.einsum('cd,edf->cef', t, w_up)
        hid = (jax.nn.silu(a) * u * combine[..., None]).astype(t.dtype)
        return jnp.einsum('cef,efd->cd', hid, w_down)

    return lax.map(chunk, tok).reshape(B, T, D)


def setup_inputs(seed: int = 0) -> dict:
    key = jax.random.key(seed)
    ks = jax.random.split(key, 48)
    counter = [0]

    def nk():
        counter[0] += 1
        return ks[counter[0] - 1]

    def nrm(shape, scale):
        return jax.random.normal(nk(), shape, jnp.float32) * scale

    def gain(shape):
        return 1.0 + nrm(shape, 0.02)

    L, D = DEPTH, D_MODEL
    cmp_in = CMP_LEN * NSA_HEAD_DIM
    x = nrm((BATCH, SEQ, D), 1.0)
    offset = jax.random.randint(nk(), (BATCH, 1), 0, 1024, dtype=jnp.int32)
    positions = offset + jnp.arange(SEQ, dtype=jnp.int32)[None, :]
    return {
        "x": x,
        "positions": positions,
        "rel_bias": nrm((REL_BUCKETS, NSA_HEADS), 0.5),
        "norm_attn": gain((L, D)),
        "w_in": nrm((L, D, D_IN), D ** -0.5),
        "b_merge": nrm((L, MERGE_W), 0.1),
        "cmp_pe": nrm((L, CMP_LEN, NSA_HEAD_DIM), 0.1),
        "cmp_k_w1": nrm((L, cmp_in, CMP_HIDDEN), cmp_in ** -0.5),
        "cmp_k_b1": nrm((L, CMP_HIDDEN), 0.02),
        "cmp_k_w2": nrm((L, CMP_HIDDEN, NSA_HEAD_DIM), CMP_HIDDEN ** -0.5),
        "cmp_v_w1": nrm((L, cmp_in, CMP_HIDDEN), cmp_in ** -0.5),
        "cmp_v_b1": nrm((L, CMP_HIDDEN), 0.02),
        "cmp_v_w2": nrm((L, CMP_HIDDEN, NSA_HEAD_DIM), CMP_HIDDEN ** -0.5),
        "mla_q_norm": gain((L, MLA_Q_RANK)),
        "mla_kv_norm": gain((L, MLA_KV_RANK)),
        "mla_w_uq": nrm((L, MLA_Q_RANK, MLA_HEADS * MLA_QK_DIM), MLA_Q_RANK ** -0.5),
        "mla_w_uk": nrm((L, MLA_KV_RANK, MLA_HEADS * MLA_NOPE_DIM), MLA_KV_RANK ** -0.5),
        "mla_w_uv": nrm((L, MLA_KV_RANK, MLA_HEADS * MLA_V_DIM), MLA_KV_RANK ** -0.5),
        "fox_b_f": jax.random.uniform(nk(), (L, FOX_HEADS), jnp.float32, 1.0, 4.0),
        "w_o_nsa": nrm((L, NSA_HEADS * NSA_HEAD_DIM, D), (NSA_HEADS * NSA_HEAD_DIM) ** -0.5),
        "w_o_mla": nrm((L, MLA_HEADS * MLA_V_DIM, D), (MLA_HEADS * MLA_V_DIM) ** -0.5),
        "w_o_fox": nrm((L, FOX_HEADS * FOX_HEAD_DIM, D), (FOX_HEADS * FOX_HEAD_DIM) ** -0.5),
        "w_out": nrm((L, D, D), D ** -0.5),
        "norm_ffn": gain((L, D)),
        "moe_w_grp": nrm((L, D, N_GROUPS), D ** -0.5),
        "moe_b_grp": nrm((L, N_GROUPS), 0.01),
        "moe_w_router": nrm((L, D, N_EXPERTS), D ** -0.5),
        "moe_b_router": nrm((L, N_EXPERTS), 0.01),
        "moe_w_gate": nrm((L, N_EXPERTS, D, EXPERT_DIM), D ** -0.5),
        "moe_w_up": nrm((L, N_EXPERTS, D, EXPERT_DIM), D ** -0.5),
        "moe_w_down": nrm((L, N_EXPERTS, EXPERT_DIM, D), EXPERT_DIM ** -0.5),
        "norm_final": gain((D,)),
    }


def reference(x, positions, rel_bias, norm_attn, w_in, b_merge, cmp_pe, cmp_k_w1, cmp_k_b1, cmp_k_w2,
              cmp_v_w1, cmp_v_b1, cmp_v_w2, mla_q_norm, mla_kv_norm, mla_w_uq, mla_w_uk, mla_w_uv, fox_b_f,
              w_o_nsa, w_o_mla, w_o_fox, w_out, norm_ffn, moe_w_grp, moe_b_grp, moe_w_router, moe_b_router,
              moe_w_gate, moe_w_up, moe_w_down, norm_final):
    B, T, D = x.shape
    split_points = [int(c) for c in np.cumsum(IN_SPLITS)[:-1]]
    for l in range(DEPTH):
        h = rmsnorm(x, norm_attn[l])
        proj = h @ w_in[l]
        nsa_q, nsa_kv, nsa_g, mla_cq, mla_ckv, mla_kr, fox_qkv, fox_f, merge_logits = jnp.split(proj, split_points, axis=-1)

        q_a = nsa_q.reshape(B, T, NSA_HEADS, NSA_HEAD_DIM)
        kv_a = nsa_kv.reshape(B, T, 3, 2, NSA_KV_GROUPS, NSA_HEAD_DIM)
        out_a = nsa_attention(q_a, kv_a[:, :, 0, 0], kv_a[:, :, 0, 1], kv_a[:, :, 1, 0], kv_a[:, :, 1, 1],
                              kv_a[:, :, 2, 0], kv_a[:, :, 2, 1], nsa_g.reshape(B, T, NSA_HEADS, 3), rel_bias,
                              cmp_pe[l], cmp_k_w1[l], cmp_k_b1[l], cmp_k_w2[l], cmp_v_w1[l], cmp_v_b1[l], cmp_v_w2[l])

        c_q = rmsnorm(mla_cq, mla_q_norm[l])
        q_b = (c_q @ mla_w_uq[l]).reshape(B, T, MLA_HEADS, MLA_QK_DIM)
        q_b = jnp.concatenate([q_b[..., :MLA_NOPE_DIM], rope(q_b[..., MLA_NOPE_DIM:], positions)], axis=-1)
        c_kv = rmsnorm(mla_ckv, mla_kv_norm[l])
        k_nope = (c_kv @ mla_w_uk[l]).reshape(B, T, MLA_HEADS, MLA_NOPE_DIM)
        v_b = (c_kv @ mla_w_uv[l]).reshape(B, T, MLA_HEADS, MLA_V_DIM)
        k_rope = rope(mla_kr, positions)
        k_b = jnp.concatenate([k_nope, jnp.broadcast_to(k_rope[:, :, None, :], (B, T, MLA_HEADS, MLA_ROPE_DIM))], axis=-1)
        out_b = causal_block_attention(q_b, k_b, v_b, MLA_QK_DIM ** -0.5).reshape(B, T, MLA_HEADS * MLA_V_DIM)

        qkv_c = fox_qkv.reshape(B, T, 3, FOX_HEADS, FOX_HEAD_DIM)
        log_f = jax.nn.log_sigmoid(fox_f.astype(jnp.float32) + fox_b_f[l].astype(jnp.float32))
        cum_log_f = jnp.cumsum(log_f, axis=1)
        out_c = causal_block_attention(qkv_c[:, :, 0], qkv_c[:, :, 1], qkv_c[:, :, 2], FOX_HEAD_DIM ** -0.5,
                                       cum_log_f).reshape(B, T, FOX_HEADS * FOX_HEAD_DIM)

        gates = jax.nn.sigmoid((merge_logits + b_merge[l]).astype(jnp.float32)).reshape(B, T, N_BRANCHES, D)
        mixed = (gates[:, :, 0] * (out_a @ w_o_nsa[l]) + gates[:, :, 1] * (out_b @ w_o_mla[l])
                 + gates[:, :, 2] * (out_c @ w_o_fox[l]))
        x = x + mixed.astype(x.dtype) @ w_out[l]

        x = x + hier_moe(rmsnorm(x, norm_ffn[l]), moe_w_grp[l], moe_b_grp[l], moe_w_router[l], moe_b_router[l],
                         moe_w_gate[l], moe_w_up[l], moe_w_down[l])
    return rmsnorm(x, norm_final)
```

```python
import functools
import math

import numpy as np
import jax
import jax.numpy as jnp
from jax import lax
from jax.experimental import pallas as pl
from jax.experimental.pallas import tpu as pltpu

F32 = jnp.float32
BF16 = jnp.bfloat16

D_MODEL = 1024
DEPTH = 2
EPS = 1e-6
NEG = -1e30
REMOVED = -3e38

NSA_HEADS = 8
NSA_G = 2
NSA_HPG = 4
DH = 64
CMP_STRIDE = 16
CMP_HIDDEN = 128
SLC_BLOCK = 64
SLC_TOPK = 16
SLC_LOCAL = 2
WINDOW = 512
QB = 128
SEL_TK = 256
CMP_BAND = 32
MLA_HEADS = 8
MLA_Q_RANK = 256
MLA_KV_RANK = 128
MLA_NOPE = 64
MLA_ROPE = 32
MLA_QK = 96
MLA_QK_PAD = 128
ROPE_THETA = 10000.0
FOX_HEADS = 8
REL_BUCKETS = 32
REL_MAX_DIST = 128
N_GROUPS = 4
EPG = 8
N_EXPERTS = 32
EXPERT_DIM = 256

LANES = 128
VMEM_LIMIT = 56 * 1024 * 1024


def _bucket_thresholds():
    n = np.arange(0, 4 * REL_MAX_DIST)
    max_exact = REL_BUCKETS // 2
    lr = np.log(np.maximum(n, 1).astype(np.float32) / np.float32(max_exact)) / np.float32(math.log(REL_MAX_DIST / max_exact))
    large = max_exact + (lr.astype(np.float32) * np.float32(REL_BUCKETS - max_exact)).astype(np.int32)
    bucket = np.where(n < max_exact, n, np.minimum(large, REL_BUCKETS - 1))
    return [int(np.argmax(bucket >= b)) for b in range(1, REL_BUCKETS)]


BUCKET_THR = _bucket_thresholds()
FAR_DIST = BUCKET_THR[-1]
assert FAR_DIST <= REL_MAX_DIST


def _cparams(sem):
    return pltpu.CompilerParams(dimension_semantics=sem, vmem_limit_bytes=VMEM_LIMIT)


def _split3(x):
    hi = x.astype(BF16)
    r = x - hi.astype(F32)
    mid = r.astype(BF16)
    lo = (r - mid.astype(F32)).astype(BF16)
    return hi, mid, lo


def _lane_pick(x, idx):
    lane = lax.broadcasted_iota(jnp.int32, x.shape, x.ndim - 1)
    return jnp.sum(jnp.where(lane == idx, x, 0.0), axis=-1, keepdims=True)


def _bias_of_dist(dist, tab_ref, h):
    val = jnp.full(dist.shape, tab_ref[0, h], F32)
    for b in range(1, REL_BUCKETS):
        val = jnp.where(dist >= BUCKET_THR[b - 1], tab_ref[b, h], val)
    return val


def _bias_table_kernel(tab_ref, tbc_ref, tbs_ref, tbw_ref):
    h = pl.program_id(0)
    far = tab_ref[REL_BUCKETS - 1, h]
    q = lax.broadcasted_iota(jnp.int32, (QB, CMP_BAND), 0)
    j = lax.broadcasted_iota(jnp.int32, (QB, CMP_BAND), 1)
    tbc_ref[...] = _bias_of_dist(q - CMP_STRIDE * j + (2 * QB - CMP_STRIDE - 15), tab_ref, h) - far
    q = lax.broadcasted_iota(jnp.int32, (QB, SEL_TK), 0)
    c = lax.broadcasted_iota(jnp.int32, (QB, SEL_TK), 1)
    for var, off in enumerate((SEL_TK, 0, QB)):
        dist = q + off - c
        tbs_ref[var] = jnp.where(dist >= 0, _bias_of_dist(dist, tab_ref, h), NEG)
    tbs_ref[3] = jnp.full((QB, SEL_TK), far, F32)
    q = lax.broadcasted_iota(jnp.int32, (QB, QB), 0)
    c = lax.broadcasted_iota(jnp.int32, (QB, QB), 1)
    for r in range(WINDOW // QB + 1):
        dist = q + (WINDOW - r * QB) - c
        tbw_ref[r] = jnp.where((dist >= 0) & (dist < WINDOW), _bias_of_dist(dist, tab_ref, h), NEG)


def _bias_tables(rel_bias):
    nwin = WINDOW // QB + 1
    return pl.pallas_call(
        _bias_table_kernel,
        grid=(NSA_HEADS,),
        in_specs=[pl.BlockSpec(memory_space=pltpu.SMEM)],
        out_specs=[
            pl.BlockSpec((None, QB, CMP_BAND), lambda h: (h, 0, 0)),
            pl.BlockSpec((4, None, QB, SEL_TK), lambda h: (0, h, 0, 0)),
            pl.BlockSpec((None, nwin, QB, QB), lambda h: (h, 0, 0, 0)),
        ],
        out_shape=[
            jax.ShapeDtypeStruct((NSA_HEADS, QB, CMP_BAND), F32),
            jax.ShapeDtypeStruct((4, NSA_HEADS, QB, SEL_TK), F32),
            jax.ShapeDtypeStruct((NSA_HEADS, nwin, QB, QB), F32),
        ],
        compiler_params=_cparams(("arbitrary",)),
        name="bias_tables",
    )(rel_bias)


SEG_NQ = (0, 512)
SEG_NKV = (512, 1280)
SEG_NG = (1280, 1408)
SEG_CQ = (1408, 1664)
SEG_CKV = (1664, 1792)
SEG_KR1 = (1792, 1920)
SEG_KR2 = (1920, 2048)
SEG_FOX = (2048, 3584)
SEG_FF = (3584, 3712)
W1_COLS = 3712


def _rms(x, g):
    return x * lax.rsqrt(jnp.mean(x * x, axis=-1, keepdims=True) + EPS) * g


def _inproj_kernel(x_ref, pos_ref, g_ref, w1_ref, wuq_ref, wuk_ref, wuv_ref, qn_ref, kvn_ref, fb_ref, invf_ref,
                   tri_ref, nq_ref, nkv_ref, ng_ref, mq_ref, mk_ref, mv_ref, mkr_ref, fqkv_ref, fcum_ref, carry_ref):
    i = pl.program_id(1)
    h = _rms(x_ref[...], g_ref[...]).astype(BF16)

    def seg(ab):
        return jnp.dot(h, w1_ref[:, ab[0]:ab[1]], preferred_element_type=F32)

    nq_ref[...] = (seg(SEG_NQ) * (DH ** -0.5)).astype(BF16)
    nkv_ref[...] = seg(SEG_NKV).astype(BF16)
    ng_ref[...] = jax.nn.sigmoid(seg(SEG_NG))

    ang = pos_ref[...].astype(F32) * invf_ref[...]
    cos, sin = jnp.cos(ang), jnp.sin(ang)
    sc = MLA_QK ** -0.5
    c_q = _rms(seg(SEG_CQ), qn_ref[...]).astype(BF16)
    qb = jnp.dot(c_q, wuq_ref[...], preferred_element_type=F32)
    x1, x2 = qb[:, 512:640], qb[:, 640:768]
    mq_ref[:, 0:512] = (qb[:, 0:512] * sc).astype(BF16)
    mq_ref[:, 512:640] = ((x1 * cos - x2 * sin) * sc).astype(BF16)
    mq_ref[:, 640:768] = ((x1 * sin + x2 * cos) * sc).astype(BF16)
    c_kv = _rms(seg(SEG_CKV), kvn_ref[...]).astype(BF16)
    mk_ref[...] = jnp.dot(c_kv, wuk_ref[...], preferred_element_type=F32).astype(BF16)
    mv_ref[...] = jnp.dot(c_kv, wuv_ref[...], preferred_element_type=F32).astype(BF16)
    k1, k2 = seg(SEG_KR1), seg(SEG_KR2)
    mkr_ref[:, 0:128] = (k1 * cos - k2 * sin).astype(BF16)
    mkr_ref[:, 128:256] = (k1 * sin + k2 * cos).astype(BF16)

    fox = seg(SEG_FOX)
    fqkv_ref[:, 0:512] = (fox[:, 0:512] * (DH ** -0.5)).astype(BF16)
    fqkv_ref[:, 512:1536] = fox[:, 512:1536].astype(BF16)
    log_f = jax.nn.log_sigmoid(seg(SEG_FF) + fb_ref[...])

    @pl.when(i == 0)
    def _():
        carry_ref[...] = jnp.zeros_like(carry_ref)

    tri = tri_ref[...]
    cum = carry_ref[...]
    for part in _split3(log_f):
        cum = cum + jnp.dot(tri, part, preferred_element_type=F32)
    fcum_ref[...] = cum
    carry_ref[...] = cum[cum.shape[0] - 1:cum.shape[0], :]


def _inproj(x, pos, g, w1, wuq, wuk, wuv, qn, kvn, fb, invf, tri, tm):
    B, T, D = x.shape
    row = lambda w: pl.BlockSpec((None, tm, w), lambda b, i: (b, i, 0))
    full = lambda a: pl.BlockSpec(a.shape, lambda b, i: (0,) * a.ndim)
    outs = [(512, BF16), (768, BF16), (128, F32), (768, BF16), (512, BF16), (512, BF16), (256, BF16), (1536, BF16), (128, F32)]
    return pl.pallas_call(
        _inproj_kernel,
        grid=(B, T // tm),
        in_specs=[row(D), row(1), full(g), full(w1), full(wuq), full(wuk), full(wuv), full(qn), full(kvn), full(fb),
                  full(invf), full(tri)],
        out_specs=[row(w) for w, _ in outs],
        out_shape=[jax.ShapeDtypeStruct((B, T, w), dt) for w, dt in outs],
        scratch_shapes=[pltpu.VMEM((1, LANES), F32)],
        compiler_params=_cparams(("arbitrary", "arbitrary")),
        name="inproj",
    )(x, pos, g, w1, wuq, wuk, wuv, qn, kvn, fb, invf, tri)


def _flash_update(s, v, m_sc, l_sc, acc_sc):
    m_prev = m_sc[...]
    m_new = jnp.maximum(m_prev, jnp.max(s, axis=-1, keepdims=True))
    alpha = jnp.exp(m_prev - m_new)
    p = jnp.exp(s - m_new)
    l_sc[...] = alpha * l_sc[...] + jnp.sum(p, axis=-1, keepdims=True)
    acc_sc[...] = alpha * acc_sc[...] + jnp.dot(p.astype(BF16), v, preferred_element_type=F32)
    m_sc[...] = m_new


def _flash_kernel(*refs, tq, tk, use_cum):
    if use_cum:
        q_ref, kt_ref, v_ref, cq_ref, ck_ref, o_ref, m_sc, l_sc, acc_sc = refs
    else:
        q_ref, kt_ref, v_ref, o_ref, m_sc, l_sc, acc_sc = refs
    hh = pl.program_id(1)
    qi = pl.program_id(2)
    q = q_ref[...]
    m_sc[...] = jnp.full_like(m_sc, NEG)
    l_sc[...] = jnp.zeros_like(l_sc)
    acc_sc[...] = jnp.zeros_like(acc_sc)
    if use_cum:
        cq = _lane_pick(cq_ref[...], hh)

    def step(kt, masked):
        s = jnp.dot(q, kt_ref[kt], preferred_element_type=F32)
        if use_cum:
            s = s + (cq - ck_ref[kt])
        if masked:
            qpos = qi * tq + lax.broadcasted_iota(jnp.int32, (tq, tk), 0)
            kpos = kt * tk + lax.broadcasted_iota(jnp.int32, (tq, tk), 1)
            s = jnp.where(kpos <= qpos, s, NEG)
        start = pl.multiple_of(kt * tk, tk)
        _flash_update(s, v_ref[pl.ds(start, tk), :], m_sc, l_sc, acc_sc)

    n_full = (qi * tq) // tk

    def body(kt, c):
        step(kt, False)
        return c

    lax.fori_loop(0, n_full, body, 0)
    step(n_full, True)
    o_ref[...] = (acc_sc[...] / l_sc[...]).astype(o_ref.dtype)


def _flash(q, kt, v, cum_q=None, cum_k=None, *, tq, tk):
    B, H, T, dq = q.shape
    dv = v.shape[-1]
    nk = T // tk
    use_cum = cum_q is not None
    in_specs = [
        pl.BlockSpec((None, None, tq, dq), lambda b, h, i: (b, h, i, 0)),
        pl.BlockSpec((None, None, nk, dq, tk), lambda b, h, i: (b, h, 0, 0, 0)),
        pl.BlockSpec((None, None, T, dv), lambda b, h, i: (b, h, 0, 0)),
    ]
    args = [q, kt, v]
    if use_cum:
        in_specs += [pl.BlockSpec((None, tq, LANES), lambda b, h, i: (b, i, 0)),
                     pl.BlockSpec((None, None, nk, 1, tk), lambda b, h, i: (b, h, 0, 0, 0))]
        args += [cum_q, cum_k]
    return pl.pallas_call(
        functools.partial(_flash_kernel, tq=tq, tk=tk, use_cum=use_cum),
        grid=(B, H, T // tq),
        in_specs=in_specs,
        out_specs=pl.BlockSpec((None, None, tq, dv), lambda b, h, i: (b, h, i, 0)),
        out_shape=jax.ShapeDtypeStruct((B, H, T, dv), BF16),
        scratch_shapes=[pltpu.VMEM((tq, 1), F32), pltpu.VMEM((tq, 1), F32), pltpu.VMEM((tq, dv), F32)],
        compiler_params=_cparams(("arbitrary", "arbitrary", "arbitrary")),
        name="flash_fox" if use_cum else "flash_mla",
    )(*args)


def _compress_kernel(sub_ref, pe_ref, w1_ref, b1_ref, w2_ref, o_ref):
    sub = sub_ref[...].astype(F32)
    n_sub = sub.shape[0]
    a = jnp.dot((sub + pe_ref[0:1, :]).astype(BF16), w1_ref[0], preferred_element_type=F32)
    b = jnp.dot((sub + pe_ref[1:2, :]).astype(BF16), w1_ref[1], preferred_element_type=F32)
    hid = jax.nn.gelu(a + pltpu.roll(b, n_sub - 1, 0) + b1_ref[...])
    o_ref[...] = jnp.dot(hid.astype(BF16), w2_ref[...], preferred_element_type=F32).astype(o_ref.dtype)


def _compress(sub, pe, w1, b1, w2):
    B, G, _, n_sub, _ = sub.shape
    return pl.pallas_call(
        _compress_kernel,
        grid=(B, G, 2),
        in_specs=[
            pl.BlockSpec((None, None, None, n_sub, 1024), lambda b, g, s: (b, g, s, 0, 0)),
            pl.BlockSpec((None, 2, 1024), lambda b, g, s: (s, 0, 0)),
            pl.BlockSpec((None, 2, 1024, CMP_HIDDEN), lambda b, g, s: (s, 0, 0, 0)),
            pl.BlockSpec((None, 1, CMP_HIDDEN), lambda b, g, s: (s, 0, 0)),
            pl.BlockSpec((None, CMP_HIDDEN, DH), lambda b, g, s: (s, 0, 0)),
        ],
        out_specs=pl.BlockSpec((None, None, None, n_sub, DH), lambda b, g, s: (b, g, s, 0, 0)),
        out_shape=jax.ShapeDtypeStruct((B, G, 2, n_sub, DH), BF16),
        compiler_params=_cparams(("arbitrary", "arbitrary", "arbitrary")),
        name="nsa_compress",
    )(sub, pe, w1, b1, w2)


def _nsa_kernel(q_ref, kct_ref, vc_ref, kst_ref, vs_ref, kwt_ref, vw_ref, gate_ref, tbc_hi_ref, tbc_lo_ref, far_ref,
                tbs_ref, tbw_ref, ssum_ref, o_ref, m_sc, l_sc, acc_sc):
    g = pl.program_id(1)
    qb = pl.program_id(2)
    t0 = qb * QB
    rows = NSA_HPG * QB
    ncp = kct_ref.shape[-1]
    q = q_ref[...].reshape(rows, DH)

    s = jnp.dot(q, kct_ref[...], preferred_element_type=F32)
    n0 = (QB // CMP_STRIDE) * qb - 16
    place = (lax.broadcasted_iota(jnp.int32, (CMP_BAND, ncp), 0) + n0
             == lax.broadcasted_iota(jnp.int32, (CMP_BAND, ncp), 1)).astype(BF16)
    band = (jnp.dot(tbc_hi_ref[...], place, preferred_element_type=F32)
            + jnp.dot(tbc_lo_ref[...], place, preferred_element_type=F32))
    s = s + far_ref[...] + band
    tq = t0 + (lax.broadcasted_iota(jnp.int32, (rows, ncp), 0) & (QB - 1))
    cmp_end = lax.broadcasted_iota(jnp.int32, (rows, ncp), 1) * CMP_STRIDE + (2 * CMP_STRIDE - 1)
    ok = cmp_end <= tq
    s = jnp.where(ok, s, NEG)
    e = jnp.exp(s - jnp.max(s, axis=-1, keepdims=True))
    p_c = jnp.where(ok, e / jnp.sum(e, axis=-1, keepdims=True), 0.0)
    o_c = jnp.dot(p_c.astype(BF16), vc_ref[...], preferred_element_type=F32)

    imp = jnp.sum(p_c.reshape(NSA_HPG, QB, ncp), axis=0)
    ssum = ssum_ref[...]
    p_slc = sum(jnp.dot(part, ssum, preferred_element_type=F32) for part in _split3(imp))
    jj = lax.broadcasted_iota(jnp.int32, (QB, LANES), 1)
    blk_cur = (t0 + lax.broadcasted_iota(jnp.int32, (QB, LANES), 0)) // SLC_BLOCK
    causal_b = jj <= blk_cur
    forced = (jj == 0) | (causal_b & (jj > blk_cur - SLC_LOCAL))
    score = jnp.where(causal_b, jnp.where(forced, 1e9, p_slc), NEG)
    jf = jj.astype(F32)
    sel = jnp.zeros((QB, LANES), F32)
    for _ in range(SLC_TOPK):
        mx = jnp.max(score, axis=-1, keepdims=True)
        first = jnp.min(jnp.where(score == mx, jf, float(LANES)), axis=-1, keepdims=True)
        hit = jf == first
        sel = jnp.where(hit, 1.0, sel)
        score = jnp.where(hit, REMOVED, score)
    sel = jnp.where(causal_b, sel, 0.0).astype(BF16)

    def reset():
        m_sc[...] = jnp.full_like(m_sc, NEG)
        l_sc[...] = jnp.zeros_like(l_sc)
        acc_sc[...] = jnp.zeros_like(acc_sc)

    n_tiles = qb // 2 + 1
    even = (qb % 2) == 0
    reset()

    def sel_step(kt, c):
        s = jnp.dot(q, kst_ref[kt], preferred_element_type=F32).reshape(NSA_HPG, QB, SEL_TK)
        expand = (lax.broadcasted_iota(jnp.int32, (LANES, SEL_TK), 0)
                  == (SEL_TK // SLC_BLOCK) * kt + lax.broadcasted_iota(jnp.int32, (LANES, SEL_TK), 1) // SLC_BLOCK)
        chosen = jnp.dot(sel, expand.astype(BF16), preferred_element_type=F32)
        last = kt == n_tiles - 1
        var = jnp.where(last, jnp.where(even, 1, 2), jnp.where(even & (kt == n_tiles - 2), 0, 3))
        s = s + tbs_ref[var] + ((chosen - 1.0) * 1e30)[None]
        start = pl.multiple_of(kt * SEL_TK, SEL_TK)
        _flash_update(s.reshape(rows, SEL_TK), vs_ref[pl.ds(start, SEL_TK), :], m_sc, l_sc, acc_sc)
        return c

    lax.fori_loop(0, n_tiles, sel_step, 0)
    o_s = acc_sc[...] / l_sc[...]

    reset()
    n_wt = WINDOW // QB
    for r in range(n_wt, -1, -1):
        kt = qb - n_wt + r
        ktc = jnp.maximum(kt, 0)
        s = jnp.dot(q, kwt_ref[ktc], preferred_element_type=F32).reshape(NSA_HPG, QB, QB)
        s = s + tbw_ref[:, r] + jnp.where(kt >= 0, 0.0, NEG)
        start = pl.multiple_of(ktc * QB, QB)
        _flash_update(s.reshape(rows, QB), vw_ref[pl.ds(start, QB), :], m_sc, l_sc, acc_sc)
    o_w = acc_sc[...] / l_sc[...]

    gates = gate_ref[...]
    for hh in range(NSA_HPG):
        col = (g * NSA_HPG + hh) * 3
        r0 = hh * QB
        out = (_lane_pick(gates, col) * o_c[r0:r0 + QB] + _lane_pick(gates, col + 1) * o_s[r0:r0 + QB]
               + _lane_pick(gates, col + 2) * o_w[r0:r0 + QB])
        o_ref[hh] = out.astype(o_ref.dtype)


def _nsa(q, kct, vc, kst, vs, kwt, vw, gates, tbc_hi, tbc_lo, far_col, tbs, tbw, ssum):
    B, G, HPG, T, _ = q.shape
    ncp = kct.shape[-1]
    rows = HPG * QB
    nwin = WINDOW // QB + 1
    bg = lambda *shape: pl.BlockSpec((None, None) + shape, lambda b, g, i: (b, g) + (0,) * len(shape))
    return pl.pallas_call(
        _nsa_kernel,
        grid=(B, G, T // QB),
        in_specs=[
            pl.BlockSpec((None, None, HPG, QB, DH), lambda b, g, i: (b, g, 0, i, 0)),
            bg(DH, ncp), bg(ncp, DH),
            bg(T // SEL_TK, DH, SEL_TK), bg(T, DH),
            bg(T // QB, DH, QB), bg(T, DH),
            pl.BlockSpec((None, QB, LANES), lambda b, g, i: (b, i, 0)),
            pl.BlockSpec((None, rows, CMP_BAND), lambda b, g, i: (g, 0, 0)),
            pl.BlockSpec((None, rows, CMP_BAND), lambda b, g, i: (g, 0, 0)),
            pl.BlockSpec((None, rows, 1), lambda b, g, i: (g, 0, 0)),
            pl.BlockSpec((4, HPG, QB, SEL_TK), lambda b, g, i: (0, g, 0, 0)),
            pl.BlockSpec((HPG, nwin, QB, QB), lambda b, g, i: (g, 0, 0, 0)),
            pl.BlockSpec(ssum.shape, lambda b, g, i: (0, 0)),
        ],
        out_specs=pl.BlockSpec((None, None, HPG, QB, DH), lambda b, g, i: (b, g, 0, i, 0)),
        out_shape=jax.ShapeDtypeStruct((B, G, HPG, T, DH), BF16),
        scratch_shapes=[pltpu.VMEM((rows, 1), F32), pltpu.VMEM((rows, 1), F32), pltpu.VMEM((rows, DH), F32)],
        compiler_params=_cparams(("arbitrary", "arbitrary", "arbitrary")),
        name="nsa_attention",
    )(q, kct, vc, kst, vs, kwt, vw, gates, tbc_hi, tbc_lo, far_col, tbs, tbw, ssum)


def _merge_kernel(x_ref, oa_ref, ob_ref, oc_ref, ga_ref, wm_ref, bm_ref, wo_ref, wout_ref, gf_ref, wgrp_ref, bgrp_ref,
                  wrt_ref, brt_ref, xo_ref, h2_ref, comb_ref):
    x = x_ref[...]
    h = _rms(x, ga_ref[...]).astype(BF16)
    mixed = None
    for i, o_ref in enumerate((oa_ref, ob_ref, oc_ref)):
        gate = jax.nn.sigmoid(jnp.dot(h, wm_ref[:, i * D_MODEL:(i + 1) * D_MODEL], preferred_element_type=F32)
                              + bm_ref[:, i * D_MODEL:(i + 1) * D_MODEL])
        term = gate * jnp.dot(o_ref[...], wo_ref[i], preferred_element_type=F32)
        mixed = term if mixed is None else mixed + term
    xn = x + jnp.dot(mixed.astype(BF16), wout_ref[...], preferred_element_type=F32)
    xo_ref[...] = xn
    h2 = _rms(xn, gf_ref[...])
    h2_ref[...] = h2.astype(BF16)

    lane = lax.broadcasted_iota(jnp.int32, (x.shape[0], LANES), 1).astype(F32)
    gl = jnp.dot(h2, wgrp_ref[...], preferred_element_type=F32, precision=lax.Precision.HIGHEST) + bgrp_ref[...]
    ge = jnp.exp(gl - jnp.max(gl, axis=-1, keepdims=True))
    gp = ge / jnp.sum(ge, axis=-1, keepdims=True)
    grp_top = jnp.max(gp, axis=-1, keepdims=True)
    g_idx = jnp.min(jnp.where(gp == grp_top, lane, float(LANES)), axis=-1, keepdims=True)
    el = jnp.dot(h2, wrt_ref[...], preferred_element_type=F32, precision=lax.Precision.HIGHEST) + brt_ref[...]
    in_grp = jnp.floor(lane * (1.0 / EPG)) == g_idx
    el = jnp.where(in_grp, el, NEG)
    ee = jnp.exp(el - jnp.max(el, axis=-1, keepdims=True))
    ep = jnp.where(in_grp, ee / jnp.sum(ee, axis=-1, keepdims=True), -1.0)
    p1 = jnp.max(ep, axis=-1, keepdims=True)
    i1 = jnp.min(jnp.where(ep == p1, lane, float(LANES)), axis=-1, keepdims=True)
    ep2 = jnp.where(lane == i1, -1.0, ep)
    p2 = jnp.max(ep2, axis=-1, keepdims=True)
    i2 = jnp.min(jnp.where(ep2 == p2, lane, float(LANES)), axis=-1, keepdims=True)
    den = p1 + p2
    comb_ref[...] = jnp.where(lane == i1, grp_top * p1 / den, 0.0) + jnp.where(lane == i2, grp_top * p2 / den, 0.0)


def _merge(x, oa, ob, oc, ga, wm, bm, wo, wout, gf, wgrp, bgrp, wrt, brt, tm):
    N, D = x.shape
    row = lambda w: pl.BlockSpec((tm, w), lambda i: (i, 0))
    full = lambda a: pl.BlockSpec(a.shape, lambda i: (0,) * a.ndim)
    return pl.pallas_call(
        _merge_kernel,
        grid=(N // tm,),
        in_specs=[row(D), row(512), row(512), row(512), full(ga), full(wm), full(bm), full(wo), full(wout), full(gf),
                  full(wgrp), full(bgrp), full(wrt), full(brt)],
        out_specs=[row(D), row(D), row(LANES)],
        out_shape=[jax.ShapeDtypeStruct((N, D), F32), jax.ShapeDtypeStruct((N, D), BF16),
                   jax.ShapeDtypeStruct((N, LANES), F32)],
        compiler_params=_cparams(("arbitrary",)),
        name="merge_router",
    )(x, oa, ob, oc, ga, wm, bm, wo, wout, gf, wgrp, bgrp, wrt, brt)


def _moe_kernel(x_ref, h_ref, comb_ref, wg_ref, wu_ref, wd_ref, gfin_ref, o_ref, acc_sc, *, final_norm):
    e = pl.program_id(1)

    @pl.when(e == 0)
    def _():
        acc_sc[...] = jnp.zeros_like(acc_sc)

    h = h_ref[...]
    a = jnp.dot(h, wg_ref[...], preferred_element_type=F32)
    u = jnp.dot(h, wu_ref[...], preferred_element_type=F32)
    c = _lane_pick(comb_ref[...], e)
    hid = (jax.nn.silu(a) * u * c).astype(BF16)
    acc_sc[...] += jnp.dot(hid, wd_ref[...], preferred_element_type=F32)

    @pl.when(e == pl.num_programs(1) - 1)
    def _():
        y = x_ref[...] + acc_sc[...]
        o_ref[...] = _rms(y, gfin_ref[...]) if final_norm else y


def _moe(x, h2, comb, wg, wu, wd, gfin, tm, final_norm):
    N, D = x.shape
    row = lambda w: pl.BlockSpec((tm, w), lambda i, e: (i, 0))
    return pl.pallas_call(
        functools.partial(_moe_kernel, final_norm=final_norm),
        grid=(N // tm, N_EXPERTS),
        in_specs=[row(D), row(D), row(LANES),
                  pl.BlockSpec((None, D, EXPERT_DIM), lambda i, e: (e, 0, 0)),
                  pl.BlockSpec((None, D, EXPERT_DIM), lambda i, e: (e, 0, 0)),
                  pl.BlockSpec((None, EXPERT_DIM, D), lambda i, e: (e, 0, 0)),
                  pl.BlockSpec((1, D), lambda i, e: (0, 0))],
        out_specs=row(D),
        out_shape=jax.ShapeDtypeStruct((N, D), F32),
        scratch_shapes=[pltpu.VMEM((tm, D), F32)],
        compiler_params=_cparams(("arbitrary", "arbitrary")),
        name="moe_experts",
    )(x, h2, comb, wg, wu, wd, gfin)


def _pad_cols(w, n):
    return jnp.pad(w, ((0, 0), (0, n - w.shape[1])))


def _pack_w1(w_in_l):
    o = np.cumsum([0, 512, 768, 24, MLA_Q_RANK, MLA_KV_RANK, MLA_ROPE, 1536, FOX_HEADS])
    nq, nkv, ng, cq, ckv, kr, fox, ff = (w_in_l[:, o[i]:o[i + 1]] for i in range(8))
    half = MLA_ROPE // 2
    kr1 = jnp.tile(kr[:, :half], (1, MLA_HEADS))
    kr2 = jnp.tile(kr[:, half:], (1, MLA_HEADS))
    w1 = jnp.concatenate([nq, nkv, _pad_cols(ng, LANES), cq, ckv, kr1, kr2, fox, _pad_cols(ff, LANES)], axis=1)
    return w1.astype(BF16), w_in_l[:, o[8]:]


def _pack_wuq(w_uq_l):
    w = w_uq_l.reshape(MLA_Q_RANK, MLA_HEADS, MLA_QK)
    half = MLA_ROPE // 2
    nope = w[:, :, :MLA_NOPE].reshape(MLA_Q_RANK, -1)
    r1 = w[:, :, MLA_NOPE:MLA_NOPE + half].reshape(MLA_Q_RANK, -1)
    r2 = w[:, :, MLA_NOPE + half:].reshape(MLA_Q_RANK, -1)
    return jnp.concatenate([nope, r1, r2], axis=1).astype(BF16)


def _heads(a, B, T, H):
    return a.reshape(B, T, H, -1).transpose(0, 2, 1, 3)


def _tile_major_t(k, tk):
    B, H, T, d = k.shape
    return k.reshape(B, H, T // tk, tk, d).transpose(0, 1, 2, 4, 3)


def _ssum_matrix(ncp):
    n = np.arange(ncp)[:, None]
    j = np.arange(LANES)[None, :]
    d = n - 4 * j
    m = np.where((d == 0) | (d == 4), 1.0, np.where((d >= 1) & (d <= 3), 2.0, 0.0))
    return jnp.asarray(m, BF16)


def kernel(x, positions, rel_bias, norm_attn, w_in, b_merge, cmp_pe, cmp_k_w1, cmp_k_b1, cmp_k_w2, cmp_v_w1, cmp_v_b1,
           cmp_v_w2, mla_q_norm, mla_kv_norm, mla_w_uq, mla_w_uk, mla_w_uv, fox_b_f, w_o_nsa, w_o_mla, w_o_fox, w_out,
           norm_ffn, moe_w_grp, moe_b_grp, moe_w_router, moe_b_router, moe_w_gate, moe_w_up, moe_w_down, norm_final):
    B, T, D = x.shape
    N = B * T
    n_sub = T // CMP_STRIDE
    tm_in = min(512, T)
    fl_tq, fl_tk = min(256, T), min(512, T)
    tm_merge = 512
    tm_moe = 1024

    tbc, tbs, tbw = _bias_tables(rel_bias)
    tbc = tbc.reshape(NSA_G, NSA_HPG * QB, CMP_BAND)
    tbc_hi = tbc.astype(BF16)
    tbc_lo = (tbc - tbc_hi.astype(F32)).astype(BF16)
    far_col = jnp.repeat(rel_bias[REL_BUCKETS - 1], QB).reshape(NSA_G, NSA_HPG * QB, 1)
    ssum = _ssum_matrix(n_sub)
    half = MLA_ROPE // 2
    invf = jnp.tile(ROPE_THETA ** (-jnp.arange(half, dtype=F32) / half), MLA_HEADS).reshape(1, LANES)
    tri = jnp.asarray(np.tril(np.ones((tm_in, tm_in), np.float32)), BF16)
    pos3 = positions.reshape(B, T, 1)

    for l in range(DEPTH):
        w1, w_merge = _pack_w1(w_in[l])
        fb = _pad_cols(fox_b_f[l].reshape(1, -1), LANES)
        nq, nkv, ng, mq, mk, mv, mkr, fqkv, fcum = _inproj(
            x, pos3, norm_attn[l].reshape(1, D), w1, _pack_wuq(mla_w_uq[l]), mla_w_uk[l].astype(BF16),
            mla_w_uv[l].astype(BF16), mla_q_norm[l].reshape(1, -1), mla_kv_norm[l].reshape(1, -1), fb, invf, tri, tm_in)

        kv6 = nkv.reshape(B, T, 3, 2, NSA_G, DH)
        sub = kv6[:, :, 0].transpose(0, 3, 2, 1, 4).reshape(B, NSA_G, 2, n_sub, CMP_STRIDE * DH)
        pe = jnp.broadcast_to(cmp_pe[l].reshape(1, 2, CMP_STRIDE * DH), (2, 2, CMP_STRIDE * DH))
        cw1 = jnp.stack([cmp_k_w1[l], cmp_v_w1[l]]).reshape(2, 2, CMP_STRIDE * DH, CMP_HIDDEN).astype(BF16)
        cb1 = jnp.stack([cmp_k_b1[l], cmp_v_b1[l]]).reshape(2, 1, CMP_HIDDEN)
        cw2 = jnp.stack([cmp_k_w2[l], cmp_v_w2[l]]).astype(BF16)
        cmp = _compress(sub, pe, cw1, cb1, cw2)
        kct = cmp[:, :, 0].transpose(0, 1, 3, 2)
        vc = cmp[:, :, 1]
        grp = lambda br, s: kv6[:, :, br, s].transpose(0, 2, 1, 3)
        q_a = nq.reshape(B, T, NSA_G, NSA_HPG, DH).transpose(0, 2, 3, 1, 4)
        o_a = _nsa(q_a, kct, vc, _tile_major_t(grp(1, 0), SEL_TK), grp(1, 1), _tile_major_t(grp(2, 0), QB), grp(2, 1),
                   ng, tbc_hi, tbc_lo, far_col, tbs, tbw, ssum)
        out_a = o_a.transpose(0, 3, 1, 2, 4).reshape(N, NSA_HEADS * DH)

        q_b = jnp.concatenate([
            mq[..., :512].reshape(B, T, MLA_HEADS, MLA_NOPE), mq[..., 512:640].reshape(B, T, MLA_HEADS, half),
            mq[..., 640:768].reshape(B, T, MLA_HEADS, half),
            jnp.zeros((B, T, MLA_HEADS, MLA_QK_PAD - MLA_QK), BF16)], axis=-1).transpose(0, 2, 1, 3)
        k_b = jnp.concatenate([
            mk.reshape(B, T, MLA_HEADS, MLA_NOPE), mkr[..., :128].reshape(B, T, MLA_HEADS, half),
            mkr[..., 128:].reshape(B, T, MLA_HEADS, half),
            jnp.zeros((B, T, MLA_HEADS, MLA_QK_PAD - MLA_QK), BF16)], axis=-1).transpose(0, 2, 1, 3)
        o_b = _flash(q_b, _tile_major_t(k_b, fl_tk), _heads(mv, B, T, MLA_HEADS), tq=fl_tq, tk=fl_tk)
        out_b = o_b.transpose(0, 2, 1, 3).reshape(N, -1)

        cum_k = fcum[..., :FOX_HEADS].transpose(0, 2, 1).reshape(B, FOX_HEADS, T // fl_tk, 1, fl_tk)
        o_c = _flash(_heads(fqkv[..., :512], B, T, FOX_HEADS), _tile_major_t(_heads(fqkv[..., 512:1024], B, T, FOX_HEADS), fl_tk),
                     _heads(fqkv[..., 1024:], B, T, FOX_HEADS), fcum, cum_k, tq=fl_tq, tk=fl_tk)
        out_c = o_c.transpose(0, 2, 1, 3).reshape(N, -1)

        wo = jnp.stack([w_o_nsa[l], w_o_mla[l], w_o_fox[l]]).astype(BF16)
        pad_b = lambda b, n: jnp.concatenate([b, jnp.full((LANES - n,), NEG, F32)]).reshape(1, LANES)
        x2, h2, comb = _merge(
            x.reshape(N, D), out_a, out_b, out_c, norm_attn[l].reshape(1, D), w_merge.astype(BF16),
            b_merge[l].reshape(1, -1), wo, w_out[l].astype(BF16), norm_ffn[l].reshape(1, D),
            _pad_cols(moe_w_grp[l], LANES), pad_b(moe_b_grp[l], N_GROUPS),
            _pad_cols(moe_w_router[l], LANES), pad_b(moe_b_router[l], N_EXPERTS), tm_merge)

        x = _moe(x2, h2, comb, moe_w_gate[l].astype(BF16), moe_w_up[l].astype(BF16), moe_w_down[l].astype(BF16),
                 norm_final.reshape(1, D), tm_moe, final_norm=(l == DEPTH - 1)).reshape(B, T, D)
    return x
```

```python
import functools
import math

import numpy as np
import jax
import jax.numpy as jnp
from jax import lax
from jax.experimental import pallas as pl
from jax.experimental.pallas import tpu as pltpu

F32 = jnp.float32
BF16 = jnp.bfloat16

D_MODEL = 1024
DEPTH = 2
EPS = 1e-6
NEG = -1e30
REMOVED = -3e38

NSA_HEADS = 8
NSA_G = 2
NSA_HPG = 4
DH = 64
CMP_STRIDE = 16
CMP_HIDDEN = 128
SLC_BLOCK = 64
SLC_TOPK = 16
SLC_LOCAL = 2
WINDOW = 512
QB = 128
GQ = NSA_HPG * QB
SEL_TK = 256
CHAIN_W = 256
CMP_BAND = 32
MLA_HEADS = 8
MLA_Q_RANK = 256
MLA_KV_RANK = 128
MLA_NOPE = 64
MLA_ROPE = 32
MLA_QK = 96
ROPE_THETA = 10000.0
FOX_HEADS = 8
REL_BUCKETS = 32
REL_MAX_DIST = 128
N_GROUPS = 4
EPG = 8
N_EXPERTS = 32
EXPERT_DIM = 256

LANES = 128
LOG2E = 1.4426950408889634
VMEM_LIMIT = 56 * 1024 * 1024


def _bucket_thresholds():
    n = np.arange(0, 4 * REL_MAX_DIST)
    max_exact = REL_BUCKETS // 2
    lr = np.log(np.maximum(n, 1).astype(np.float32) / np.float32(max_exact)) / np.float32(math.log(REL_MAX_DIST / max_exact))
    large = max_exact + (lr.astype(np.float32) * np.float32(REL_BUCKETS - max_exact)).astype(np.int32)
    bucket = np.where(n < max_exact, n, np.minimum(large, REL_BUCKETS - 1))
    return [int(np.argmax(bucket >= b)) for b in range(1, REL_BUCKETS)]


BUCKET_THR = _bucket_thresholds()
assert BUCKET_THR[-1] <= REL_MAX_DIST


def _cparams(sem):
    return pltpu.CompilerParams(dimension_semantics=sem, vmem_limit_bytes=VMEM_LIMIT)


def _split3(x):
    hi = x.astype(BF16)
    r = x - hi.astype(F32)
    mid = r.astype(BF16)
    lo = (r - mid.astype(F32)).astype(BF16)
    return hi, mid, lo


def _lane_pick(x, idx):
    lane = lax.broadcasted_iota(jnp.int32, x.shape, x.ndim - 1)
    return jnp.sum(jnp.where(lane == idx, x, 0.0), axis=-1, keepdims=True)


def _online_softmax_steps(logits, vts, carries):
    stats = []
    for s, (m, l, _) in zip(logits, carries):
        m_new = jnp.maximum(m, jnp.max(s, axis=0, keepdims=True))
        p = jnp.exp2(s - m_new)
        alpha = jnp.exp2(m - m_new)
        stats.append((m_new, alpha * l + jnp.sum(p, axis=0, keepdims=True), alpha, p.astype(BF16)))
    out = []
    for (m_new, l_new, alpha, p), vt, (_, _, acc) in zip(stats, vts, carries):
        out.append((m_new, l_new, alpha * acc + jnp.dot(vt, p, preferred_element_type=F32)))
    return tuple(out)


def _bias_table_kernel(tab_ref, tbc_ref, tbs_ref, tbw_ref):
    h = pl.program_id(0)
    far = tab_ref[REL_BUCKETS - 1, h]

    def delta(dist):
        val = jnp.full(dist.shape, tab_ref[0, h], F32)
        for b in range(1, REL_BUCKETS):
            val = jnp.where(dist >= BUCKET_THR[b - 1], tab_ref[b, h], val)
        return (val - far) * LOG2E

    j = lax.broadcasted_iota(jnp.int32, (CMP_BAND, QB), 0)
    q = lax.broadcasted_iota(jnp.int32, (CMP_BAND, QB), 1)
    band = delta(q - CMP_STRIDE * j + (2 * QB - 2 * CMP_STRIDE + 1))
    band_hi = band.astype(BF16)
    tbc_ref[0] = band_hi
    tbc_ref[1] = (band - band_hi.astype(F32)).astype(BF16)
    c = lax.broadcasted_iota(jnp.int32, (SEL_TK, QB), 0)
    q = lax.broadcasted_iota(jnp.int32, (SEL_TK, QB), 1)
    for var, off in enumerate((SEL_TK, 0, QB)):
        dist = q + off - c
        tbs_ref[var] = jnp.where(dist >= 0, delta(dist), NEG)
    tbs_ref[3] = jnp.zeros((SEL_TK, QB), F32)
    c = lax.broadcasted_iota(jnp.int32, (WINDOW + QB, QB), 0)
    q = lax.broadcasted_iota(jnp.int32, (WINDOW + QB, QB), 1)
    dist = q + WINDOW - c
    tbw_ref[...] = jnp.where((dist >= 0) & (dist < WINDOW), delta(dist), NEG)


def _bias_tables(rel_bias):
    hq = NSA_HEADS * QB
    return pl.pallas_call(
        _bias_table_kernel,
        grid=(NSA_HEADS,),
        in_specs=[pl.BlockSpec(memory_space=pltpu.SMEM)],
        out_specs=[
            pl.BlockSpec((2, CMP_BAND, QB), lambda h: (0, 0, h)),
            pl.BlockSpec((4, SEL_TK, QB), lambda h: (0, 0, h)),
            pl.BlockSpec((WINDOW + QB, QB), lambda h: (0, h)),
        ],
        out_shape=[
            jax.ShapeDtypeStruct((2, CMP_BAND, hq), BF16),
            jax.ShapeDtypeStruct((4, SEL_TK, hq), F32),
            jax.ShapeDtypeStruct((WINDOW + QB, hq), F32),
        ],
        compiler_params=_cparams(("arbitrary",)),
        name="bias_tables",
    )(rel_bias)


SEG_NQ = (0, 512)
SEG_NKV = (512, 1280)
SEG_NG = (1280, 1408)
SEG_CQ = (1408, 1664)
SEG_CKV = (1664, 1792)
SEG_KR1 = (1792, 1920)
SEG_KR2 = (1920, 2048)
SEG_FOX = (2048, 3584)
SEG_FF = (3584, 3712)


def _rms(x, g):
    return x * lax.rsqrt(jnp.mean(x * x, axis=-1, keepdims=True) + EPS) * g


def _inproj_kernel(x_ref, pos_ref, g_ref, w1_ref, wuq_ref, wuk_ref, wuv_ref, qn_ref, kvn_ref, fb_ref, invf_ref,
                   tri_ref, nq_ref, nkv_ref, ng_ref, mq_ref, mk_ref, mv_ref, mkr_ref, fqkv_ref, fcum_ref, carry_ref):
    i = pl.program_id(1)
    h = _rms(x_ref[...], g_ref[...]).astype(BF16)

    def seg(ab):
        return jnp.dot(h, w1_ref[:, ab[0]:ab[1]], preferred_element_type=F32)

    nq_ref[...] = (seg(SEG_NQ) * (DH ** -0.5 * LOG2E)).astype(BF16)
    nkv_ref[...] = seg(SEG_NKV).astype(BF16)
    ng_ref[...] = jax.nn.sigmoid(seg(SEG_NG))

    ang = pos_ref[...].astype(F32) * invf_ref[...]
    cos, sin = jnp.cos(ang), jnp.sin(ang)
    sc = MLA_QK ** -0.5 * LOG2E
    c_q = _rms(seg(SEG_CQ), qn_ref[...]).astype(BF16)
    qb = jnp.dot(c_q, wuq_ref[...], preferred_element_type=F32)
    x1, x2 = qb[:, 512:640], qb[:, 640:768]
    mq_ref[:, 0:512] = (qb[:, 0:512] * sc).astype(BF16)
    mq_ref[:, 512:640] = ((x1 * cos - x2 * sin) * sc).astype(BF16)
    mq_ref[:, 640:768] = ((x1 * sin + x2 * cos) * sc).astype(BF16)
    c_kv = _rms(seg(SEG_CKV), kvn_ref[...]).astype(BF16)
    mk_ref[...] = jnp.dot(c_kv, wuk_ref[...], preferred_element_type=F32).astype(BF16)
    mv_ref[...] = jnp.dot(c_kv, wuv_ref[...], preferred_element_type=F32).astype(BF16)
    k1, k2 = seg(SEG_KR1), seg(SEG_KR2)
    mkr_ref[:, 0:128] = (k1 * cos - k2 * sin).astype(BF16)
    mkr_ref[:, 128:256] = (k1 * sin + k2 * cos).astype(BF16)

    fox = seg(SEG_FOX)
    fqkv_ref[:, 0:512] = (fox[:, 0:512] * (DH ** -0.5 * LOG2E)).astype(BF16)
    fqkv_ref[:, 512:1536] = fox[:, 512:1536].astype(BF16)
    log_f = jax.nn.log_sigmoid(seg(SEG_FF) + fb_ref[...])

    @pl.when(i == 0)
    def _():
        carry_ref[...] = jnp.zeros_like(carry_ref)

    tri = tri_ref[...]
    cum = carry_ref[...]
    for part in _split3(log_f):
        cum = cum + jnp.dot(tri, part, preferred_element_type=F32)
    for n, part in enumerate(_split3(cum * LOG2E)):
        fcum_ref[:, n * LANES:(n + 1) * LANES] = part
    carry_ref[...] = cum[cum.shape[0] - 1:cum.shape[0], :]


def _inproj(x, pos, g, w1, wuq, wuk, wuv, qn, kvn, fb, invf, tri, tm):
    B, T, D = x.shape
    row = lambda w: pl.BlockSpec((None, tm, w), lambda b, i: (b, i, 0))
    full = lambda a: pl.BlockSpec(a.shape, lambda b, i: (0,) * a.ndim)
    outs = [(512, BF16), (768, BF16), (128, F32), (768, BF16), (512, BF16), (512, BF16), (256, BF16), (1536, BF16), (384, BF16)]
    return pl.pallas_call(
        _inproj_kernel,
        grid=(B, T // tm),
        in_specs=[row(D), row(1), full(g), full(w1), full(wuq), full(wuk), full(wuv), full(qn), full(kvn), full(fb),
                  full(invf), full(tri)],
        out_specs=[row(w) for w, _ in outs],
        out_shape=[jax.ShapeDtypeStruct((B, T, w), dt) for w, dt in outs],
        scratch_shapes=[pltpu.VMEM((1, LANES), F32)],
        compiler_params=_cparams(("arbitrary", "arbitrary")),
        name="inproj",
    )(x, pos, g, w1, wuq, wuk, wuv, qn, kvn, fb, invf, tri)


def _flash_kernel(qt_ref, k_ref, vt_ref, o_ref, *, tq, hb):
    qi = pl.program_id(2)
    dv = vt_ref.shape[2]
    chains = [(j, c) for j in range(hb) for c in range(tq // CHAIN_W)]
    qts = [qt_ref[j, :, c * CHAIN_W:(c + 1) * CHAIN_W] for j, c in chains]
    start = pl.multiple_of(qi * tq, tq)

    def full_step(kt, carries):
        st = pl.multiple_of(kt * tq, tq)
        logits = [jnp.dot(k_ref[j, pl.ds(st, tq), :], qt, preferred_element_type=F32)
                  for (j, _), qt in zip(chains, qts)]
        return _online_softmax_steps(logits, [vt_ref[j, kt] for j, _ in chains], carries)

    init = tuple((jnp.full((1, CHAIN_W), NEG, F32), jnp.zeros((1, CHAIN_W), F32), jnp.zeros((dv, CHAIN_W), F32))
                 for _ in chains)
    carries = lax.fori_loop(0, qi, full_step, init)
    logits, vts = [], []
    for (j, c), qt in zip(chains, qts):
        nk = (c + 1) * CHAIN_W
        s = jnp.dot(k_ref[j, pl.ds(start, nk), :], qt, preferred_element_type=F32)
        kpos = lax.broadcasted_iota(jnp.int32, (nk, CHAIN_W), 0)
        qpos = c * CHAIN_W + lax.broadcasted_iota(jnp.int32, (nk, CHAIN_W), 1)
        logits.append(jnp.where(kpos <= qpos, s, NEG))
        vts.append(vt_ref[j, qi][:, 0:nk])
    for (j, c), (_, l, acc) in zip(chains, _online_softmax_steps(logits, vts, carries)):
        o_ref[j, :, c * CHAIN_W:(c + 1) * CHAIN_W] = (acc * (1.0 / l)).astype(o_ref.dtype)


def _flash(qt, k, vt, *, tq, hb):
    B, H, dk, T = qt.shape
    dv = vt.shape[3]
    return pl.pallas_call(
        functools.partial(_flash_kernel, tq=tq, hb=hb),
        grid=(B, H // hb, T // tq),
        in_specs=[
            pl.BlockSpec((None, hb, dk, tq), lambda b, h, i: (b, h, 0, i)),
            pl.BlockSpec((None, hb, T, dk), lambda b, h, i: (b, h, 0, 0)),
            pl.BlockSpec((None, hb, T // tq, dv, tq), lambda b, h, i: (b, h, 0, 0, 0)),
        ],
        out_specs=pl.BlockSpec((None, hb, dv, tq), lambda b, h, i: (b, h, 0, i)),
        out_shape=jax.ShapeDtypeStruct((B, H, dv, T), BF16),
        compiler_params=_cparams(("arbitrary", "arbitrary", "arbitrary")),
        name="flash_attention",
    )(qt, k, vt)


def _compress_kernel(sub_ref, pe_ref, w1_ref, b1_ref, w2_ref, o_ref):
    sub = sub_ref[...].astype(F32)
    n_sub = sub.shape[0]
    a = jnp.dot((sub + pe_ref[0:1, :]).astype(BF16), w1_ref[0], preferred_element_type=F32)
    b = jnp.dot((sub + pe_ref[1:2, :]).astype(BF16), w1_ref[1], preferred_element_type=F32)
    hid = jax.nn.gelu(a + pltpu.roll(b, n_sub - 1, 0) + b1_ref[...])
    o_ref[...] = jnp.dot(hid.astype(BF16), w2_ref[...], preferred_element_type=F32).astype(o_ref.dtype)


def _compress(sub, pe, w1, b1, w2):
    B, G, _, n_sub, _ = sub.shape
    return pl.pallas_call(
        _compress_kernel,
        grid=(B, G, 2),
        in_specs=[
            pl.BlockSpec((None, None, None, n_sub, 1024), lambda b, g, s: (b, g, s, 0, 0)),
            pl.BlockSpec((None, 2, 1024), lambda b, g, s: (s, 0, 0)),
            pl.BlockSpec((None, 2, 1024, CMP_HIDDEN), lambda b, g, s: (s, 0, 0, 0)),
            pl.BlockSpec((None, 1, CMP_HIDDEN), lambda b, g, s: (s, 0, 0)),
            pl.BlockSpec((None, CMP_HIDDEN, DH), lambda b, g, s: (s, 0, 0)),
        ],
        out_specs=pl.BlockSpec((None, None, None, n_sub, DH), lambda b, g, s: (b, g, s, 0, 0)),
        out_shape=jax.ShapeDtypeStruct((B, G, 2, n_sub, DH), BF16),
        compiler_params=_cparams(("arbitrary", "arbitrary", "arbitrary")),
        name="nsa_compress",
    )(sub, pe, w1, b1, w2)


def _nsa_kernel(qt_ref, kc_ref, vct_ref, ksa_ref, vst_ref, kwa_ref, vwt_ref, gate_ref, tbc_hi_ref, tbc_lo_ref,
                tbs_ref, tbw_ref, ssum_ref, o_ref):
    qb = pl.program_id(2)
    t0 = qb * QB
    ncp = kc_ref.shape[0]
    qt = qt_ref[...]

    s = jnp.dot(kc_ref[...], qt, preferred_element_type=F32)
    n0 = (QB // CMP_STRIDE) * qb - 16
    place = (lax.broadcasted_iota(jnp.int32, (ncp, CMP_BAND), 0)
             == lax.broadcasted_iota(jnp.int32, (ncp, CMP_BAND), 1) + n0).astype(BF16)
    s = s + (jnp.dot(place, tbc_hi_ref[...], preferred_element_type=F32)
             + jnp.dot(place, tbc_lo_ref[...], preferred_element_type=F32))
    tq = t0 + (lax.broadcasted_iota(jnp.int32, (ncp, GQ), 1) & (QB - 1))
    cmp_end = lax.broadcasted_iota(jnp.int32, (ncp, GQ), 0) * CMP_STRIDE + (2 * CMP_STRIDE - 1)
    ok = cmp_end <= tq
    s = jnp.where(ok, s, NEG)
    e = jnp.exp2(s - jnp.max(s, axis=0, keepdims=True))
    p_c = jnp.where(ok, e * (1.0 / jnp.sum(e, axis=0, keepdims=True)), 0.0)
    o_c = jnp.dot(vct_ref[...], p_c.astype(BF16), preferred_element_type=F32)

    imp = p_c[:, 0:QB]
    for hh in range(1, NSA_HPG):
        imp = imp + p_c[:, hh * QB:(hh + 1) * QB]
    ssum = ssum_ref[...]
    p_slc = sum(jnp.dot(ssum, part, preferred_element_type=F32) for part in _split3(imp))
    jj = lax.broadcasted_iota(jnp.int32, (LANES, QB), 0)
    blk_cur = (t0 + lax.broadcasted_iota(jnp.int32, (LANES, QB), 1)) // SLC_BLOCK
    causal_b = jj <= blk_cur
    forced = (jj == 0) | (causal_b & (jj > blk_cur - SLC_LOCAL))
    score = jnp.where(causal_b, jnp.where(forced, 1e9, p_slc), NEG)
    jf = jj.astype(F32)
    sel = jnp.zeros((LANES, QB), F32)
    for _ in range(SLC_TOPK):
        mx = jnp.max(score, axis=0, keepdims=True)
        first = jnp.min(jnp.where(score == mx, jf, float(LANES)), axis=0, keepdims=True)
        hit = jf == first
        sel = jnp.where(hit, 1.0, sel)
        score = jnp.where(hit, REMOVED, score)
    unsel = jnp.where(causal_b & (sel > 0.0), 0.0, NEG).astype(BF16)

    qa = jnp.concatenate([qt] + [jnp.concatenate([unsel] * NSA_HPG, axis=1)], axis=0)
    n_tiles = qb // 2 + 1
    even = (qb % 2) == 0

    chunks = [(c * CHAIN_W, (c + 1) * CHAIN_W) for c in range(GQ // CHAIN_W)]
    qas = [qa[:, a:b] for a, b in chunks]

    def sel_step(kt, carries, var=None):
        start = pl.multiple_of(kt * SEL_TK, SEL_TK)
        logits = [jnp.dot(ksa_ref[pl.ds(start, SEL_TK), :], qc, preferred_element_type=F32)
                  for qc in qas]
        if var is not None:
            logits = [s + tbs_ref[var, :, a:b] for s, (a, b) in zip(logits, chunks)]
        return _online_softmax_steps(logits, [vst_ref[kt]] * len(chunks), carries)

    carries = tuple((jnp.full((1, CHAIN_W), NEG, F32), jnp.zeros((1, CHAIN_W), F32), jnp.zeros((DH, CHAIN_W), F32))
                    for _ in chunks)
    carries = lax.fori_loop(0, jnp.maximum(n_tiles - 2, 0), sel_step, carries)
    carries = lax.cond(n_tiles >= 2,
                       lambda c: sel_step(n_tiles - 2, c, jnp.where(even, 0, 3)),
                       lambda c: c, carries)
    carries = sel_step(n_tiles - 1, carries, jnp.where(even, 1, 2))
    o_s = jnp.concatenate([acc * (1.0 / l) for _, l, acc in carries], axis=1)

    n_win = WINDOW + QB
    pad_row = jnp.where(lax.broadcasted_iota(jnp.int32, (DH, GQ), 0) == 0, NEG, 0.0).astype(BF16)
    qw = jnp.concatenate([qt, pad_row], axis=0)
    start = pl.multiple_of(t0, QB)
    s = jnp.dot(kwa_ref[pl.ds(start, n_win), :], qw, preferred_element_type=F32) + tbw_ref[...]
    e = jnp.exp2(s - jnp.max(s, axis=0, keepdims=True))
    inv = 1.0 / jnp.sum(e, axis=0, keepdims=True)
    e = e.astype(BF16)
    o_w = jnp.zeros((DH, GQ), F32)
    for r in range(n_win // QB):
        o_w = o_w + jnp.dot(vwt_ref[qb + r], e[r * QB:(r + 1) * QB], preferred_element_type=F32)
    o_w = o_w * inv

    gates = gate_ref[...]
    o_ref[...] = (gates[0:1] * o_c + gates[1:2] * o_s + gates[2:3] * o_w).astype(o_ref.dtype)


def _nsa(qt, kc, vct, ksa, vst, kwa, vwt, gates, tbc_hi, tbc_lo, tbs, tbw, ssum):
    B, G, nqb, _, _ = qt.shape
    bg = lambda a: pl.BlockSpec((None, None) + a.shape[2:], lambda b, g, i: (b, g) + (0,) * (a.ndim - 2))
    per_g = lambda a: pl.BlockSpec((None,) + a.shape[1:], lambda b, g, i: (g,) + (0,) * (a.ndim - 1))
    per_blk = lambda a: pl.BlockSpec((None, None, None) + a.shape[3:], lambda b, g, i: (b, g, i) + (0,) * (a.ndim - 3))
    return pl.pallas_call(
        _nsa_kernel,
        grid=(B, G, nqb),
        in_specs=[per_blk(qt), bg(kc), bg(vct), bg(ksa), bg(vst), bg(kwa), bg(vwt), per_blk(gates),
                  per_g(tbc_hi), per_g(tbc_lo), per_g(tbs), per_g(tbw),
                  pl.BlockSpec(ssum.shape, lambda b, g, i: (0, 0))],
        out_specs=pl.BlockSpec((None, None, None, DH, GQ), lambda b, g, i: (b, g, i, 0, 0)),
        out_shape=jax.ShapeDtypeStruct((B, G, nqb, DH, GQ), BF16),
        compiler_params=_cparams(("arbitrary", "arbitrary", "arbitrary")),
        name="nsa_attention",
    )(qt, kc, vct, ksa, vst, kwa, vwt, gates, tbc_hi, tbc_lo, tbs, tbw, ssum)


def _merge_kernel(x_ref, oa_ref, ob_ref, oc_ref, ga_ref, wm_ref, bm_ref, wo_ref, wout_ref, gf_ref, wgrp_ref, bgrp_ref,
                  wrt_ref, brt_ref, xo_ref, h2_ref, comb_ref):
    x = x_ref[...]
    h = _rms(x, ga_ref[...]).astype(BF16)
    mixed = None
    for i, o_ref in enumerate((oa_ref, ob_ref, oc_ref)):
        gate = jax.nn.sigmoid(jnp.dot(h, wm_ref[:, i * D_MODEL:(i + 1) * D_MODEL], preferred_element_type=F32)
                              + bm_ref[:, i * D_MODEL:(i + 1) * D_MODEL])
        term = gate * jnp.dot(o_ref[...], wo_ref[i], preferred_element_type=F32)
        mixed = term if mixed is None else mixed + term
    xn = x + jnp.dot(mixed.astype(BF16), wout_ref[...], preferred_element_type=F32)
    xo_ref[...] = xn
    h2 = _rms(xn, gf_ref[...])
    h2_ref[...] = h2.astype(BF16)

    lane = lax.broadcasted_iota(jnp.int32, (x.shape[0], LANES), 1).astype(F32)
    gl = jnp.dot(h2, wgrp_ref[...], preferred_element_type=F32, precision=lax.Precision.HIGHEST) + bgrp_ref[...]
    ge = jnp.exp(gl - jnp.max(gl, axis=-1, keepdims=True))
    gp = ge / jnp.sum(ge, axis=-1, keepdims=True)
    grp_top = jnp.max(gp, axis=-1, keepdims=True)
    g_idx = jnp.min(jnp.where(gp == grp_top, lane, float(LANES)), axis=-1, keepdims=True)
    el = jnp.dot(h2, wrt_ref[...], preferred_element_type=F32, precision=lax.Precision.HIGHEST) + brt_ref[...]
    in_grp = jnp.floor(lane * (1.0 / EPG)) == g_idx
    el = jnp.where(in_grp, el, NEG)
    ee = jnp.exp(el - jnp.max(el, axis=-1, keepdims=True))
    ep = jnp.where(in_grp, ee / jnp.sum(ee, axis=-1, keepdims=True), -1.0)
    p1 = jnp.max(ep, axis=-1, keepdims=True)
    i1 = jnp.min(jnp.where(ep == p1, lane, float(LANES)), axis=-1, keepdims=True)
    ep2 = jnp.where(lane == i1, -1.0, ep)
    p2 = jnp.max(ep2, axis=-1, keepdims=True)
    i2 = jnp.min(jnp.where(ep2 == p2, lane, float(LANES)), axis=-1, keepdims=True)
    den = p1 + p2
    comb_ref[...] = jnp.where(lane == i1, grp_top * p1 / den, 0.0) + jnp.where(lane == i2, grp_top * p2 / den, 0.0)


def _merge(x, oa, ob, oc, ga, wm, bm, wo, wout, gf, wgrp, bgrp, wrt, brt, tm):
    N, D = x.shape
    row = lambda w: pl.BlockSpec((tm, w), lambda i: (i, 0))
    full = lambda a: pl.BlockSpec(a.shape, lambda i: (0,) * a.ndim)
    return pl.pallas_call(
        _merge_kernel,
        grid=(N // tm,),
        in_specs=[row(D), row(512), row(512), row(512), full(ga), full(wm), full(bm), full(wo), full(wout), full(gf),
                  full(wgrp), full(bgrp), full(wrt), full(brt)],
        out_specs=[row(D), row(D), row(LANES)],
        out_shape=[jax.ShapeDtypeStruct((N, D), F32), jax.ShapeDtypeStruct((N, D), BF16),
                   jax.ShapeDtypeStruct((N, LANES), F32)],
        compiler_params=_cparams(("arbitrary",)),
        name="merge_router",
    )(x, oa, ob, oc, ga, wm, bm, wo, wout, gf, wgrp, bgrp, wrt, brt)


def _moe_kernel(x_ref, h_ref, comb_ref, wg_ref, wu_ref, wd_ref, gfin_ref, o_ref, acc_sc, *, final_norm):
    e = pl.program_id(1)

    @pl.when(e == 0)
    def _():
        acc_sc[...] = jnp.zeros_like(acc_sc)

    h = h_ref[...]
    a = jnp.dot(h, wg_ref[...], preferred_element_type=F32)
    u = jnp.dot(h, wu_ref[...], preferred_element_type=F32)
    c = _lane_pick(comb_ref[...], e)
    hid = (jax.nn.silu(a) * u * c).astype(BF16)
    acc_sc[...] += jnp.dot(hid, wd_ref[...], preferred_element_type=F32)

    @pl.when(e == pl.num_programs(1) - 1)
    def _():
        y = x_ref[...] + acc_sc[...]
        o_ref[...] = _rms(y, gfin_ref[...]) if final_norm else y


def _moe(x, h2, comb, wg, wu, wd, gfin, tm, final_norm):
    N, D = x.shape
    row = lambda w: pl.BlockSpec((tm, w), lambda i, e: (i, 0))
    return pl.pallas_call(
        functools.partial(_moe_kernel, final_norm=final_norm),
        grid=(N // tm, N_EXPERTS),
        in_specs=[row(D), row(D), row(LANES),
                  pl.BlockSpec((None, D, EXPERT_DIM), lambda i, e: (e, 0, 0)),
                  pl.BlockSpec((None, D, EXPERT_DIM), lambda i, e: (e, 0, 0)),
                  pl.BlockSpec((None, EXPERT_DIM, D), lambda i, e: (e, 0, 0)),
                  pl.BlockSpec((1, D), lambda i, e: (0, 0))],
        out_specs=row(D),
        out_shape=jax.ShapeDtypeStruct((N, D), F32),
        scratch_shapes=[pltpu.VMEM((tm, D), F32)],
        compiler_params=_cparams(("arbitrary", "arbitrary")),
        name="moe_experts",
    )(x, h2, comb, wg, wu, wd, gfin)


def _pad_cols(w, n):
    return jnp.pad(w, ((0, 0), (0, n - w.shape[1])))


def _pack_w1(w_in_l):
    o = np.cumsum([0, 512, 768, 24, MLA_Q_RANK, MLA_KV_RANK, MLA_ROPE, 1536, FOX_HEADS])
    nq, nkv, ng, cq, ckv, kr, fox, ff = (w_in_l[:, o[i]:o[i + 1]] for i in range(8))
    half = MLA_ROPE // 2
    kr1 = jnp.tile(kr[:, :half], (1, MLA_HEADS))
    kr2 = jnp.tile(kr[:, half:], (1, MLA_HEADS))
    w1 = jnp.concatenate([nq, nkv, _pad_cols(ng, LANES), cq, ckv, kr1, kr2, fox, _pad_cols(ff, LANES)], axis=1)
    return w1.astype(BF16), w_in_l[:, o[8]:]


def _pack_wuq(w_uq_l):
    w = w_uq_l.reshape(MLA_Q_RANK, MLA_HEADS, MLA_QK)
    half = MLA_ROPE // 2
    nope = w[:, :, :MLA_NOPE].reshape(MLA_Q_RANK, -1)
    r1 = w[:, :, MLA_NOPE:MLA_NOPE + half].reshape(MLA_Q_RANK, -1)
    r2 = w[:, :, MLA_NOPE + half:].reshape(MLA_Q_RANK, -1)
    return jnp.concatenate([nope, r1, r2], axis=1).astype(BF16)


def _heads(a, B, T, H):
    return a.reshape(B, T, H, -1).transpose(0, 2, 1, 3)


def _tile_major_t(k, tk):
    B, H, T, d = k.shape
    return k.reshape(B, H, T // tk, tk, d).transpose(0, 1, 2, 4, 3)


def _pad_feat(a, n):
    return jnp.pad(a, ((0, 0),) * (a.ndim - 1) + ((0, n - a.shape[-1]),))


def _ssum_matrix(ncp):
    n = np.arange(ncp)[None, :]
    j = np.arange(LANES)[:, None]
    d = n - 4 * j
    m = np.where((d == 0) | (d == 4), 1.0, np.where((d >= 1) & (d <= 3), 2.0, 0.0))
    return jnp.asarray(m, BF16)


def kernel(x, positions, rel_bias, norm_attn, w_in, b_merge, cmp_pe, cmp_k_w1, cmp_k_b1, cmp_k_w2, cmp_v_w1, cmp_v_b1,
           cmp_v_w2, mla_q_norm, mla_kv_norm, mla_w_uq, mla_w_uk, mla_w_uv, fox_b_f, w_o_nsa, w_o_mla, w_o_fox, w_out,
           norm_ffn, moe_w_grp, moe_b_grp, moe_w_router, moe_b_router, moe_w_gate, moe_w_up, moe_w_down, norm_final):
    B, T, D = x.shape
    N = B * T
    n_sub = T // CMP_STRIDE
    nqb = T // QB
    tm_in = min(512, T)
    fl_tq, fl_hb = min(512, T), 2
    tm_merge = 512
    tm_moe = 1024
    half = MLA_ROPE // 2

    tbc, tbs, tbw = _bias_tables(rel_bias)
    tbc = tbc.reshape(2, CMP_BAND, NSA_G, GQ).transpose(0, 2, 1, 3)
    tbc_hi, tbc_lo = tbc[0], tbc[1]
    tbs = tbs.reshape(4, SEL_TK, NSA_G, GQ).transpose(2, 0, 1, 3)
    tbw = tbw.reshape(WINDOW + QB, NSA_G, GQ).transpose(1, 0, 2)
    ssum = _ssum_matrix(n_sub)
    blk_onehot = jnp.asarray(np.arange(T)[:, None] // SLC_BLOCK == np.arange(LANES)[None, :], BF16)
    pad_ind = jnp.asarray((np.arange(T + WINDOW) < WINDOW)[:, None], BF16)
    invf = jnp.tile(ROPE_THETA ** (-jnp.arange(half, dtype=F32) / half), MLA_HEADS).reshape(1, LANES)
    tri = jnp.asarray(np.tril(np.ones((tm_in, tm_in), np.float32)), BF16)
    pos3 = positions.reshape(B, T, 1)
    ones3 = jnp.ones((B, T, FOX_HEADS, 3), BF16)

    for l in range(DEPTH):
        w1, w_merge = _pack_w1(w_in[l])
        fb = _pad_cols(fox_b_f[l].reshape(1, -1), LANES)
        nq, nkv, ng, mq, mk, mv, mkr, fqkv, fcum = _inproj(
            x, pos3, norm_attn[l].reshape(1, D), w1, _pack_wuq(mla_w_uq[l]), mla_w_uk[l].astype(BF16),
            mla_w_uv[l].astype(BF16), mla_q_norm[l].reshape(1, -1), mla_kv_norm[l].reshape(1, -1), fb, invf, tri, tm_in)

        kv6 = nkv.reshape(B, T, 3, 2, NSA_G, DH)
        sub = kv6[:, :, 0].transpose(0, 3, 2, 1, 4).reshape(B, NSA_G, 2, n_sub, CMP_STRIDE * DH)
        pe = jnp.broadcast_to(cmp_pe[l].reshape(1, 2, CMP_STRIDE * DH), (2, 2, CMP_STRIDE * DH))
        cw1 = jnp.stack([cmp_k_w1[l], cmp_v_w1[l]]).reshape(2, 2, CMP_STRIDE * DH, CMP_HIDDEN).astype(BF16)
        cb1 = jnp.stack([cmp_k_b1[l], cmp_v_b1[l]]).reshape(2, 1, CMP_HIDDEN)
        cw2 = jnp.stack([cmp_k_w2[l], cmp_v_w2[l]]).astype(BF16)
        cmp = _compress(sub, pe, cw1, cb1, cw2)
        grp = lambda br, s: kv6[:, :, br, s].transpose(0, 2, 1, 3)
        qt_a = nq.reshape(B, nqb, QB, NSA_G, NSA_HPG, DH).transpose(0, 3, 1, 5, 4, 2).reshape(B, NSA_G, nqb, DH, GQ)
        ksa = jnp.concatenate([grp(1, 0), jnp.broadcast_to(blk_onehot, (B, NSA_G, T, LANES))], axis=-1)
        front = lambda a: jnp.pad(a, ((0, 0), (0, 0), (WINDOW, 0), (0, 0)))
        kwa = jnp.concatenate([front(grp(2, 0)), jnp.broadcast_to(pad_ind, (B, NSA_G, T + WINDOW, 1)),
                               jnp.zeros((B, NSA_G, T + WINDOW, DH - 1), BF16)], axis=-1)
        gates_t = ng[..., :3 * NSA_HEADS].reshape(B, nqb, QB, NSA_G, NSA_HPG, 3).transpose(0, 3, 1, 5, 4, 2)
        gates_t = jnp.pad(gates_t.reshape(B, NSA_G, nqb, 3, GQ), ((0, 0), (0, 0), (0, 0), (0, 5), (0, 0)))
        o_a = _nsa(qt_a, cmp[:, :, 0], cmp[:, :, 1].transpose(0, 1, 3, 2), ksa, _tile_major_t(grp(1, 1), SEL_TK), kwa,
                   _tile_major_t(front(grp(2, 1)), QB), gates_t, tbc_hi, tbc_lo, tbs, tbw, ssum)
        out_a = o_a.reshape(B, NSA_G, nqb, DH, NSA_HPG, QB).transpose(0, 2, 5, 1, 4, 3).reshape(N, NSA_HEADS * DH)

        q_b = jnp.concatenate([mq[..., :512].reshape(B, T, MLA_HEADS, MLA_NOPE),
                               mq[..., 512:640].reshape(B, T, MLA_HEADS, half),
                               mq[..., 640:768].reshape(B, T, MLA_HEADS, half)], axis=-1)
        k_b = jnp.concatenate([mk.reshape(B, T, MLA_HEADS, MLA_NOPE), mkr[..., :128].reshape(B, T, MLA_HEADS, half),
                               mkr[..., 128:].reshape(B, T, MLA_HEADS, half)], axis=-1)
        o_b = _flash(_pad_feat(q_b, LANES).transpose(0, 2, 3, 1), _pad_feat(k_b, LANES).transpose(0, 2, 1, 3),
                     _tile_major_t(_heads(mv, B, T, MLA_HEADS), fl_tq), tq=fl_tq, hb=fl_hb)
        out_b = o_b.transpose(0, 3, 1, 2).reshape(N, -1)

        cum3 = fcum.reshape(B, T, 3, LANES)[..., :FOX_HEADS].transpose(0, 1, 3, 2)
        q_c = jnp.concatenate([fqkv[..., :512].reshape(B, T, FOX_HEADS, DH), cum3, ones3], axis=-1)
        k_c = jnp.concatenate([fqkv[..., 512:1024].reshape(B, T, FOX_HEADS, DH), ones3, -cum3], axis=-1)
        o_c = _flash(_pad_feat(q_c, LANES).transpose(0, 2, 3, 1), _pad_feat(k_c, LANES).transpose(0, 2, 1, 3),
                     _tile_major_t(_heads(fqkv[..., 1024:], B, T, FOX_HEADS), fl_tq), tq=fl_tq, hb=fl_hb)
        out_c = o_c.transpose(0, 3, 1, 2).reshape(N, -1)

        wo = jnp.stack([w_o_nsa[l], w_o_mla[l], w_o_fox[l]]).astype(BF16)
        pad_b = lambda b, n: jnp.concatenate([b, jnp.full((LANES - n,), NEG, F32)]).reshape(1, LANES)
        x2, h2, comb = _merge(
            x.reshape(N, D), out_a, out_b, out_c, norm_attn[l].reshape(1, D), w_merge.astype(BF16),
            b_merge[l].reshape(1, -1), wo, w_out[l].astype(BF16), norm_ffn[l].reshape(1, D),
            _pad_cols(moe_w_grp[l], LANES), pad_b(moe_b_grp[l], N_GROUPS),
            _pad_cols(moe_w_router[l], LANES), pad_b(moe_b_router[l], N_EXPERTS), tm_merge)

        x = _moe(x2, h2, comb, moe_w_gate[l].astype(BF16), moe_w_up[l].astype(BF16), moe_w_down[l].astype(BF16),
                 norm_final.reshape(1, D), tm_moe, final_norm=(l == DEPTH - 1)).reshape(B, T, D)
    return x
```

```python
import functools
import math

import numpy as np
import jax
import jax.numpy as jnp
from jax import lax
from jax.experimental import pallas as pl
from jax.experimental.pallas import tpu as pltpu

F32 = jnp.float32
BF16 = jnp.bfloat16

D_MODEL = 1024
DEPTH = 2
EPS = 1e-6
NEG = -1e30
REMOVED = -3e38

NSA_HEADS = 8
NSA_G = 2
NSA_HPG = 4
DH = 64
CMP_STRIDE = 16
CMP_HIDDEN = 128
SLC_BLOCK = 64
SLC_TOPK = 16
SLC_LOCAL = 2
WINDOW = 512
QB = 128
GQ = NSA_HPG * QB
SEL_TK = 256
CHAIN_W = 256
CMP_BAND = 32
CMP_TAB = 48
MLA_HEADS = 8
MLA_Q_RANK = 256
MLA_KV_RANK = 128
MLA_NOPE = 64
MLA_ROPE = 32
MLA_QK = 96
ROPE_THETA = 10000.0
FOX_HEADS = 8
REL_BUCKETS = 32
REL_MAX_DIST = 128
N_GROUPS = 4
EPG = 8
N_EXPERTS = 32
EXPERT_DIM = 256

LANES = 128
LOG2E = 1.4426950408889634
VMEM_LIMIT = 56 * 1024 * 1024


def _bucket_thresholds():
    n = np.arange(0, 4 * REL_MAX_DIST)
    max_exact = REL_BUCKETS // 2
    lr = np.log(np.maximum(n, 1).astype(np.float32) / np.float32(max_exact)) / np.float32(math.log(REL_MAX_DIST / max_exact))
    large = max_exact + (lr.astype(np.float32) * np.float32(REL_BUCKETS - max_exact)).astype(np.int32)
    bucket = np.where(n < max_exact, n, np.minimum(large, REL_BUCKETS - 1))
    return [int(np.argmax(bucket >= b)) for b in range(1, REL_BUCKETS)]


BUCKET_THR = _bucket_thresholds()
assert BUCKET_THR[-1] <= REL_MAX_DIST


def _cparams(sem):
    return pltpu.CompilerParams(dimension_semantics=sem, vmem_limit_bytes=VMEM_LIMIT)


def _split3(x):
    hi = x.astype(BF16)
    r = x - hi.astype(F32)
    mid = r.astype(BF16)
    lo = (r - mid.astype(F32)).astype(BF16)
    return hi, mid, lo


def _lane_pick(x, idx):
    lane = lax.broadcasted_iota(jnp.int32, x.shape, x.ndim - 1)
    return jnp.sum(jnp.where(lane == idx, x, 0.0), axis=-1, keepdims=True)


def _online_softmax_steps(logits, vts, carries):
    stats = []
    for s, (m, l, _) in zip(logits, carries):
        m_new = jnp.maximum(m, jnp.max(s, axis=0, keepdims=True))
        p = jnp.exp2(s - m_new)
        alpha = jnp.exp2(m - m_new)
        stats.append((m_new, alpha * l + jnp.sum(p, axis=0, keepdims=True), alpha, p.astype(BF16)))
    out = []
    for (m_new, l_new, alpha, p), vt, (_, _, acc) in zip(stats, vts, carries):
        out.append((m_new, l_new, alpha * acc + jnp.dot(vt, p, preferred_element_type=F32)))
    return tuple(out)


def _pipelined_softmax_loop(n_steps, qk_fn, vt_fn, carries, logit_sc, prob_sc):
    n_chains = len(carries)

    def value_products(kt, slot):
        return [jnp.dot(vt, prob_sc[slot, i], preferred_element_type=F32) for i, vt in enumerate(vt_fn(kt))]

    def stage(kt, cur, nxt, carries):
        upcoming = qk_fn(kt + 1)
        pvs = value_products(jnp.maximum(kt - 1, 0), nxt)
        out = []
        for i, (pv, (m, l, acc)) in enumerate(zip(pvs, carries)):
            m_new = jnp.maximum(m, jnp.max(logit_sc[cur, i], axis=0, keepdims=True))
            p = jnp.exp2(logit_sc[cur, i] - m_new)
            alpha = jnp.exp2(m - m_new)
            out.append((m_new, alpha * l + jnp.sum(p, axis=0, keepdims=True), alpha * (acc + pv)))
            prob_sc[cur, i] = p.astype(BF16)
        for i, s in enumerate(upcoming):
            logit_sc[nxt, i] = s
        return tuple(out)

    for i, s in enumerate(qk_fn(0)):
        logit_sc[0, i] = s
    prob_sc[1] = jnp.zeros(prob_sc.shape[1:], BF16)
    carries = lax.fori_loop(0, n_steps // 2, lambda i, c: stage(2 * i + 1, 1, 0, stage(2 * i, 0, 1, c)), tuple(carries))
    odd = n_steps % 2
    carries = lax.cond(odd == 1, lambda c: stage(n_steps - 1, 0, 1, c), lambda c: c, carries)
    pvs = value_products(jnp.maximum(n_steps - 1, 0), 1 - odd)
    logits = tuple(logit_sc[odd, i] for i in range(n_chains))
    return logits, tuple((m, l, acc + pv) for (m, l, acc), pv in zip(carries, pvs))


def _bias_table_kernel(tab_ref, tbc_ref, tbs_ref, tbw_ref):
    h = pl.program_id(0)
    far = tab_ref[REL_BUCKETS - 1, h]

    def delta(dist):
        val = jnp.full(dist.shape, tab_ref[0, h], F32)
        for b in range(1, REL_BUCKETS):
            val = jnp.where(dist >= BUCKET_THR[b - 1], tab_ref[b, h], val)
        return (val - far) * LOG2E

    j = lax.broadcasted_iota(jnp.int32, (CMP_TAB, QB), 0)
    q = lax.broadcasted_iota(jnp.int32, (CMP_TAB, QB), 1)
    dist = q - CMP_STRIDE * j + (2 * QB - 2 * CMP_STRIDE + 1)
    band = jnp.where(j < CMP_BAND, jnp.where(dist >= 0, delta(dist), NEG), jnp.where(j == CMP_BAND, NEG, 0.0))
    band_hi = band.astype(BF16)
    tbc_ref[0] = band_hi
    tbc_ref[1] = (band - band_hi.astype(F32)).astype(BF16)
    c = lax.broadcasted_iota(jnp.int32, (SEL_TK, QB), 0)
    q = lax.broadcasted_iota(jnp.int32, (SEL_TK, QB), 1)
    for var, off in enumerate((SEL_TK, 0, QB)):
        dist = q + off - c
        tbs_ref[var] = jnp.where(dist >= 0, delta(dist), NEG)
    tbs_ref[3] = jnp.zeros((SEL_TK, QB), F32)
    c = lax.broadcasted_iota(jnp.int32, (WINDOW + QB, QB), 0)
    q = lax.broadcasted_iota(jnp.int32, (WINDOW + QB, QB), 1)
    dist = q + WINDOW - c
    tbw_ref[...] = jnp.where((dist >= 0) & (dist < WINDOW), delta(dist), NEG)


def _bias_tables(rel_bias):
    hq = NSA_HEADS * QB
    return pl.pallas_call(
        _bias_table_kernel,
        grid=(NSA_HEADS,),
        in_specs=[pl.BlockSpec(memory_space=pltpu.SMEM)],
        out_specs=[
            pl.BlockSpec((2, CMP_TAB, QB), lambda h: (0, 0, h)),
            pl.BlockSpec((4, SEL_TK, QB), lambda h: (0, 0, h)),
            pl.BlockSpec((WINDOW + QB, QB), lambda h: (0, h)),
        ],
        out_shape=[
            jax.ShapeDtypeStruct((2, CMP_TAB, hq), BF16),
            jax.ShapeDtypeStruct((4, SEL_TK, hq), F32),
            jax.ShapeDtypeStruct((WINDOW + QB, hq), F32),
        ],
        compiler_params=_cparams(("arbitrary",)),
        name="bias_tables",
    )(rel_bias)


SEG_NQ = (0, 512)
SEG_NKV = (512, 1280)
SEG_NG = (1280, 1408)
SEG_CQ = (1408, 1664)
SEG_CKV = (1664, 1792)
SEG_KR1 = (1792, 1920)
SEG_KR2 = (1920, 2048)
SEG_FOX = (2048, 3584)
SEG_FF = (3584, 3712)


def _rms(x, g):
    return x * lax.rsqrt(jnp.mean(x * x, axis=-1, keepdims=True) + EPS) * g


def _inproj_kernel(x_ref, pos_ref, g_ref, w1_ref, wuq_ref, wuk_ref, wuv_ref, qn_ref, kvn_ref, fb_ref, invf_ref,
                   tri_ref, nq_ref, nkv_ref, ng_ref, mq_ref, mk_ref, mv_ref, mkr_ref, fqkv_ref, fcum_ref, carry_ref):
    i = pl.program_id(1)
    h = _rms(x_ref[...], g_ref[...]).astype(BF16)

    def seg(ab):
        return jnp.dot(h, w1_ref[:, ab[0]:ab[1]], preferred_element_type=F32)

    nq_ref[...] = (seg(SEG_NQ) * (DH ** -0.5 * LOG2E)).astype(BF16)
    nkv_ref[...] = seg(SEG_NKV).astype(BF16)
    ng_ref[...] = jax.nn.sigmoid(seg(SEG_NG))

    ang = pos_ref[...].astype(F32) * invf_ref[...]
    cos, sin = jnp.cos(ang), jnp.sin(ang)
    sc = MLA_QK ** -0.5 * LOG2E
    c_q = _rms(seg(SEG_CQ), qn_ref[...]).astype(BF16)
    qb = jnp.dot(c_q, wuq_ref[...], preferred_element_type=F32)
    x1, x2 = qb[:, 512:640], qb[:, 640:768]
    mq_ref[:, 0:512] = (qb[:, 0:512] * sc).astype(BF16)
    mq_ref[:, 512:640] = ((x1 * cos - x2 * sin) * sc).astype(BF16)
    mq_ref[:, 640:768] = ((x1 * sin + x2 * cos) * sc).astype(BF16)
    c_kv = _rms(seg(SEG_CKV), kvn_ref[...]).astype(BF16)
    mk_ref[...] = jnp.dot(c_kv, wuk_ref[...], preferred_element_type=F32).astype(BF16)
    mv_ref[...] = jnp.dot(c_kv, wuv_ref[...], preferred_element_type=F32).astype(BF16)
    k1, k2 = seg(SEG_KR1), seg(SEG_KR2)
    mkr_ref[:, 0:128] = (k1 * cos - k2 * sin).astype(BF16)
    mkr_ref[:, 128:256] = (k1 * sin + k2 * cos).astype(BF16)

    fox = seg(SEG_FOX)
    fqkv_ref[:, 0:512] = (fox[:, 0:512] * (DH ** -0.5 * LOG2E)).astype(BF16)
    fqkv_ref[:, 512:1536] = fox[:, 512:1536].astype(BF16)
    log_f = jax.nn.log_sigmoid(seg(SEG_FF) + fb_ref[...])

    @pl.when(i == 0)
    def _():
        carry_ref[...] = jnp.zeros_like(carry_ref)

    tri = tri_ref[...]
    cum = carry_ref[...]
    for part in _split3(log_f):
        cum = cum + jnp.dot(tri, part, preferred_element_type=F32)
    for n, part in enumerate(_split3(cum * LOG2E)):
        fcum_ref[:, n * LANES:(n + 1) * LANES] = part
    carry_ref[...] = cum[cum.shape[0] - 1:cum.shape[0], :]


def _inproj(x, pos, g, w1, wuq, wuk, wuv, qn, kvn, fb, invf, tri, tm):
    B, T, D = x.shape
    row = lambda w: pl.BlockSpec((None, tm, w), lambda b, i: (b, i, 0))
    full = lambda a: pl.BlockSpec(a.shape, lambda b, i: (0,) * a.ndim)
    outs = [(512, BF16), (768, BF16), (128, F32), (768, BF16), (512, BF16), (512, BF16), (256, BF16), (1536, BF16), (384, BF16)]
    return pl.pallas_call(
        _inproj_kernel,
        grid=(B, T // tm),
        in_specs=[row(D), row(1), full(g), full(w1), full(wuq), full(wuk), full(wuv), full(qn), full(kvn), full(fb),
                  full(invf), full(tri)],
        out_specs=[row(w) for w, _ in outs],
        out_shape=[jax.ShapeDtypeStruct((B, T, w), dt) for w, dt in outs],
        scratch_shapes=[pltpu.VMEM((1, LANES), F32)],
        compiler_params=_cparams(("arbitrary", "arbitrary")),
        name="inproj",
    )(x, pos, g, w1, wuq, wuk, wuv, qn, kvn, fb, invf, tri)


def _flash_kernel(qt_ref, k_ref, vt_ref, o_ref, logit_sc, prob_sc, *, tq, hb):
    qi = pl.program_id(2)
    dv = vt_ref.shape[2]
    chains = [(j, c) for j in range(hb) for c in range(tq // CHAIN_W)]
    qts = [qt_ref[j, :, c * CHAIN_W:(c + 1) * CHAIN_W] for j, c in chains]

    def qk(kt):
        st = pl.multiple_of(kt * tq, tq)
        return [jnp.dot(k_ref[j, pl.ds(st, tq), :], qt, preferred_element_type=F32)
                for (j, _), qt in zip(chains, qts)]

    init = tuple((jnp.full((1, CHAIN_W), NEG, F32), jnp.zeros((1, CHAIN_W), F32), jnp.zeros((dv, CHAIN_W), F32))
                 for _ in chains)
    diag, carries = _pipelined_softmax_loop(qi, qk, lambda kt: [vt_ref[j, kt] for j, _ in chains], init,
                                            logit_sc, prob_sc)
    kpos = lax.broadcasted_iota(jnp.int32, (tq, CHAIN_W), 0)
    qpos = lax.broadcasted_iota(jnp.int32, (tq, CHAIN_W), 1)
    logits = [jnp.where(kpos <= qpos + c * CHAIN_W, s, NEG) for (_, c), s in zip(chains, diag)]
    final = _online_softmax_steps(logits, [vt_ref[j, qi] for j, _ in chains], carries)
    for (j, c), (_, l, acc) in zip(chains, final):
        o_ref[j, :, c * CHAIN_W:(c + 1) * CHAIN_W] = (acc * (1.0 / l)).astype(o_ref.dtype)


def _flash(qt, k, vt, *, tq, hb):
    B, H, dk, T = qt.shape
    dv = vt.shape[3]
    return pl.pallas_call(
        functools.partial(_flash_kernel, tq=tq, hb=hb),
        grid=(B, H // hb, T // tq),
        in_specs=[
            pl.BlockSpec((None, hb, dk, tq), lambda b, h, i: (b, h, 0, i)),
            pl.BlockSpec((None, hb, T, dk), lambda b, h, i: (b, h, 0, 0)),
            pl.BlockSpec((None, hb, T // tq, dv, tq), lambda b, h, i: (b, h, 0, 0, 0)),
        ],
        out_specs=pl.BlockSpec((None, hb, dv, tq), lambda b, h, i: (b, h, 0, i)),
        out_shape=jax.ShapeDtypeStruct((B, H, dv, T), BF16),
        scratch_shapes=[pltpu.VMEM((2, hb * tq // CHAIN_W, tq, CHAIN_W), F32),
                        pltpu.VMEM((2, hb * tq // CHAIN_W, tq, CHAIN_W), BF16)],
        compiler_params=_cparams(("arbitrary", "arbitrary", "arbitrary")),
        name="flash_attention",
    )(qt, k, vt)


def _compress_kernel(sub_ref, pe_ref, w1_ref, b1_ref, w2_ref, o_ref):
    sub = sub_ref[...].astype(F32)
    n_sub = sub.shape[0]
    a = jnp.dot((sub + pe_ref[0:1, :]).astype(BF16), w1_ref[0], preferred_element_type=F32)
    b = jnp.dot((sub + pe_ref[1:2, :]).astype(BF16), w1_ref[1], preferred_element_type=F32)
    hid = jax.nn.gelu(a + pltpu.roll(b, n_sub - 1, 0) + b1_ref[...])
    o_ref[...] = jnp.dot(hid.astype(BF16), w2_ref[...], preferred_element_type=F32).astype(o_ref.dtype)


def _compress(sub, pe, w1, b1, w2):
    B, G, _, n_sub, _ = sub.shape
    return pl.pallas_call(
        _compress_kernel,
        grid=(B, G, 2),
        in_specs=[
            pl.BlockSpec((None, None, None, n_sub, 1024), lambda b, g, s: (b, g, s, 0, 0)),
            pl.BlockSpec((None, 2, 1024), lambda b, g, s: (s, 0, 0)),
            pl.BlockSpec((None, 2, 1024, CMP_HIDDEN), lambda b, g, s: (s, 0, 0, 0)),
            pl.BlockSpec((None, 1, CMP_HIDDEN), lambda b, g, s: (s, 0, 0)),
            pl.BlockSpec((None, CMP_HIDDEN, DH), lambda b, g, s: (s, 0, 0)),
        ],
        out_specs=pl.BlockSpec((None, None, None, n_sub, DH), lambda b, g, s: (b, g, s, 0, 0)),
        out_shape=jax.ShapeDtypeStruct((B, G, 2, n_sub, DH), BF16),
        compiler_params=_cparams(("arbitrary", "arbitrary", "arbitrary")),
        name="nsa_compress",
    )(sub, pe, w1, b1, w2)


def _nsa_kernel(qt_ref, kc_ref, vct_ref, ksa_ref, vst_ref, kwa_ref, vwt_ref, gate_ref, tbc_hi_ref, tbc_lo_ref,
                tbs_ref, tbw_ref, ssum_ref, o_ref, logit_sc, prob_sc):
    qb = pl.program_id(2)
    t0 = qb * QB
    ncp = kc_ref.shape[0]
    qt = qt_ref[...]

    s = jnp.dot(kc_ref[...], qt, preferred_element_type=F32)
    n0 = (QB // CMP_STRIDE) * qb - 16
    rel = lax.broadcasted_iota(jnp.int32, (ncp, CMP_TAB), 0) - n0
    col = lax.broadcasted_iota(jnp.int32, (ncp, CMP_TAB), 1)
    place = jnp.where(jnp.minimum(rel, CMP_BAND) == col, 1.0, 0.0).astype(BF16)
    s = s + (jnp.dot(place, tbc_hi_ref[...], preferred_element_type=F32)
             + jnp.dot(place, tbc_lo_ref[...], preferred_element_type=F32))
    m = jnp.max(s, axis=0, keepdims=True)
    e = jnp.exp2(s - m)
    p_c = e * jnp.where(m > 0.5 * NEG, 1.0 / jnp.sum(e, axis=0, keepdims=True), 0.0)
    o_c = jnp.dot(vct_ref[...], p_c.astype(BF16), preferred_element_type=F32)

    imp = p_c[:, 0:QB]
    for hh in range(1, NSA_HPG):
        imp = imp + p_c[:, hh * QB:(hh + 1) * QB]
    ssum = ssum_ref[...]
    p_slc = sum(jnp.dot(ssum, part, preferred_element_type=F32) for part in _split3(imp))
    jj = lax.broadcasted_iota(jnp.int32, (LANES, QB), 0)
    blk_cur = (t0 + lax.broadcasted_iota(jnp.int32, (LANES, QB), 1)) // SLC_BLOCK
    causal_b = jj <= blk_cur
    forced = (jj == 0) | (causal_b & (jj > blk_cur - SLC_LOCAL))
    score = jnp.where(causal_b, jnp.where(forced, 1e9, p_slc), NEG)
    jf = jj.astype(F32)
    sel = jnp.zeros((LANES, QB), F32)
    for _ in range(SLC_TOPK):
        mx = jnp.max(score, axis=0, keepdims=True)
        first = jnp.min(jnp.where(score == mx, jf, float(LANES)), axis=0, keepdims=True)
        hit = jf == first
        sel = jnp.where(hit, 1.0, sel)
        score = jnp.where(hit, REMOVED, score)
    unsel = jnp.where(causal_b & (sel > 0.0), 0.0, NEG).astype(BF16)

    n_win = WINDOW + QB
    pad_row = jnp.where(lax.broadcasted_iota(jnp.int32, (DH, GQ), 0) == 0, NEG, 0.0).astype(BF16)
    qw = jnp.concatenate([qt, pad_row], axis=0)
    start = pl.multiple_of(t0, QB)
    s = jnp.dot(kwa_ref[pl.ds(start, n_win), :], qw, preferred_element_type=F32) + tbw_ref[...]
    e = jnp.exp2(s - jnp.max(s, axis=0, keepdims=True))
    inv = 1.0 / jnp.sum(e, axis=0, keepdims=True)
    e = e.astype(BF16)
    o_w = jnp.zeros((DH, GQ), F32)
    for r in range(n_win // QB):
        o_w = o_w + jnp.dot(vwt_ref[qb + r], e[r * QB:(r + 1) * QB], preferred_element_type=F32)
    o_w = o_w * inv

    qa = jnp.concatenate([qt] + [jnp.concatenate([unsel] * NSA_HPG, axis=1)], axis=0)
    n_tiles = qb // 2 + 1
    even = (qb % 2) == 0

    chunks = [(c * CHAIN_W, (c + 1) * CHAIN_W) for c in range(GQ // CHAIN_W)]
    qas = [qa[:, a:b] for a, b in chunks]

    def sel_qk(kt):
        start = pl.multiple_of(kt * SEL_TK, SEL_TK)
        return [jnp.dot(ksa_ref[pl.ds(start, SEL_TK), :], qc, preferred_element_type=F32)
                for qc in qas]

    def sel_vt(kt):
        return [vst_ref[kt]] * len(chunks)

    def near_step(kt, logits, var, carries):
        logits = [s + tbs_ref[var, :, a:b] for s, (a, b) in zip(logits, chunks)]
        return _online_softmax_steps(logits, sel_vt(kt), carries)

    carries = tuple((jnp.full((1, CHAIN_W), NEG, F32), jnp.zeros((1, CHAIN_W), F32), jnp.zeros((DH, CHAIN_W), F32))
                    for _ in chunks)
    n_far = jnp.maximum(n_tiles - 2, 0)
    logits, carries = _pipelined_softmax_loop(n_far, sel_qk, sel_vt, carries, logit_sc, prob_sc)
    logits, carries = lax.cond(
        n_tiles >= 2,
        lambda lg, c: (tuple(sel_qk(n_tiles - 1)), near_step(n_tiles - 2, lg, jnp.where(even, 0, 3), c)),
        lambda lg, c: (lg, c), logits, carries)
    carries = near_step(n_tiles - 1, logits, jnp.where(even, 1, 2), carries)
    o_s = jnp.concatenate([acc * (1.0 / l) for _, l, acc in carries], axis=1)

    gates = gate_ref[...]
    o_ref[...] = (gates[0:1] * o_c + gates[1:2] * o_s + gates[2:3] * o_w).astype(o_ref.dtype)


def _nsa(qt, kc, vct, ksa, vst, kwa, vwt, gates, tbc_hi, tbc_lo, tbs, tbw, ssum):
    B, G, nqb, _, _ = qt.shape
    bg = lambda a: pl.BlockSpec((None, None) + a.shape[2:], lambda b, g, i: (b, g) + (0,) * (a.ndim - 2))
    per_g = lambda a: pl.BlockSpec((None,) + a.shape[1:], lambda b, g, i: (g,) + (0,) * (a.ndim - 1))
    per_blk = lambda a: pl.BlockSpec((None, None, None) + a.shape[3:], lambda b, g, i: (b, g, i) + (0,) * (a.ndim - 3))
    return pl.pallas_call(
        _nsa_kernel,
        grid=(B, G, nqb),
        in_specs=[per_blk(qt), bg(kc), bg(vct), bg(ksa), bg(vst), bg(kwa), bg(vwt), per_blk(gates),
                  per_g(tbc_hi), per_g(tbc_lo), per_g(tbs), per_g(tbw),
                  pl.BlockSpec(ssum.shape, lambda b, g, i: (0, 0))],
        out_specs=pl.BlockSpec((None, None, None, DH, GQ), lambda b, g, i: (b, g, i, 0, 0)),
        out_shape=jax.ShapeDtypeStruct((B, G, nqb, DH, GQ), BF16),
        scratch_shapes=[pltpu.VMEM((2, GQ // CHAIN_W, SEL_TK, CHAIN_W), F32),
                        pltpu.VMEM((2, GQ // CHAIN_W, SEL_TK, CHAIN_W), BF16)],
        compiler_params=_cparams(("arbitrary", "arbitrary", "arbitrary")),
        name="nsa_attention",
    )(qt, kc, vct, ksa, vst, kwa, vwt, gates, tbc_hi, tbc_lo, tbs, tbw, ssum)


def _merge_kernel(x_ref, oa_ref, ob_ref, oc_ref, ga_ref, wm_ref, bm_ref, wo_ref, wout_ref, gf_ref, wgrp_ref, bgrp_ref,
                  wrt_ref, brt_ref, xo_ref, h2_ref, comb_ref):
    x = x_ref[...]
    h = _rms(x, ga_ref[...]).astype(BF16)
    mixed = None
    for i, o_ref in enumerate((oa_ref, ob_ref, oc_ref)):
        gate = jax.nn.sigmoid(jnp.dot(h, wm_ref[:, i * D_MODEL:(i + 1) * D_MODEL], preferred_element_type=F32)
                              + bm_ref[:, i * D_MODEL:(i + 1) * D_MODEL])
        term = gate * jnp.dot(o_ref[...], wo_ref[i], preferred_element_type=F32)
        mixed = term if mixed is None else mixed + term
    xn = x + jnp.dot(mixed.astype(BF16), wout_ref[...], preferred_element_type=F32)
    xo_ref[...] = xn
    h2 = _rms(xn, gf_ref[...])
    h2_ref[...] = h2.astype(BF16)

    lane = lax.broadcasted_iota(jnp.int32, (x.shape[0], LANES), 1).astype(F32)
    gl = jnp.dot(h2, wgrp_ref[...], preferred_element_type=F32, precision=lax.Precision.HIGHEST) + bgrp_ref[...]
    ge = jnp.exp(gl - jnp.max(gl, axis=-1, keepdims=True))
    gp = ge / jnp.sum(ge, axis=-1, keepdims=True)
    grp_top = jnp.max(gp, axis=-1, keepdims=True)
    g_idx = jnp.min(jnp.where(gp == grp_top, lane, float(LANES)), axis=-1, keepdims=True)
    el = jnp.dot(h2, wrt_ref[...], preferred_element_type=F32, precision=lax.Precision.HIGHEST) + brt_ref[...]
    in_grp = jnp.floor(lane * (1.0 / EPG)) == g_idx
    el = jnp.where(in_grp, el, NEG)
    ee = jnp.exp(el - jnp.max(el, axis=-1, keepdims=True))
    ep = jnp.where(in_grp, ee / jnp.sum(ee, axis=-1, keepdims=True), -1.0)
    p1 = jnp.max(ep, axis=-1, keepdims=True)
    i1 = jnp.min(jnp.where(ep == p1, lane, float(LANES)), axis=-1, keepdims=True)
    ep2 = jnp.where(lane == i1, -1.0, ep)
    p2 = jnp.max(ep2, axis=-1, keepdims=True)
    i2 = jnp.min(jnp.where(ep2 == p2, lane, float(LANES)), axis=-1, keepdims=True)
    den = p1 + p2
    comb_ref[...] = jnp.where(lane == i1, grp_top * p1 / den, 0.0) + jnp.where(lane == i2, grp_top * p2 / den, 0.0)


def _merge(x, oa, ob, oc, ga, wm, bm, wo, wout, gf, wgrp, bgrp, wrt, brt, tm):
    N, D = x.shape
    row = lambda w: pl.BlockSpec((tm, w), lambda i: (i, 0))
    full = lambda a: pl.BlockSpec(a.shape, lambda i: (0,) * a.ndim)
    return pl.pallas_call(
        _merge_kernel,
        grid=(N // tm,),
        in_specs=[row(D), row(512), row(512), row(512), full(ga), full(wm), full(bm), full(wo), full(wout), full(gf),
                  full(wgrp), full(bgrp), full(wrt), full(brt)],
        out_specs=[row(D), row(D), row(LANES)],
        out_shape=[jax.ShapeDtypeStruct((N, D), F32), jax.ShapeDtypeStruct((N, D), BF16),
                   jax.ShapeDtypeStruct((N, LANES), F32)],
        compiler_params=_cparams(("arbitrary",)),
        name="merge_router",
    )(x, oa, ob, oc, ga, wm, bm, wo, wout, gf, wgrp, bgrp, wrt, brt)


def _moe_kernel(x_ref, h_ref, comb_ref, wg_ref, wu_ref, wd_ref, gfin_ref, o_ref, acc_sc, *, final_norm):
    e = pl.program_id(1)

    @pl.when(e == 0)
    def _():
        acc_sc[...] = jnp.zeros_like(acc_sc)

    h = h_ref[...]
    a = jnp.dot(h, wg_ref[...], preferred_element_type=F32)
    u = jnp.dot(h, wu_ref[...], preferred_element_type=F32)
    c = _lane_pick(comb_ref[...], e)
    hid = (jax.nn.silu(a) * u * c).astype(BF16)
    acc_sc[...] += jnp.dot(hid, wd_ref[...], preferred_element_type=F32)

    @pl.when(e == pl.num_programs(1) - 1)
    def _():
        y = x_ref[...] + acc_sc[...]
        o_ref[...] = _rms(y, gfin_ref[...]) if final_norm else y


def _moe(x, h2, comb, wg, wu, wd, gfin, tm, final_norm):
    N, D = x.shape
    row = lambda w: pl.BlockSpec((tm, w), lambda i, e: (i, 0))
    return pl.pallas_call(
        functools.partial(_moe_kernel, final_norm=final_norm),
        grid=(N // tm, N_EXPERTS),
        in_specs=[row(D), row(D), row(LANES),
                  pl.BlockSpec((None, D, EXPERT_DIM), lambda i, e: (e, 0, 0)),
                  pl.BlockSpec((None, D, EXPERT_DIM), lambda i, e: (e, 0, 0)),
                  pl.BlockSpec((None, EXPERT_DIM, D), lambda i, e: (e, 0, 0)),
                  pl.BlockSpec((1, D), lambda i, e: (0, 0))],
        out_specs=row(D),
        out_shape=jax.ShapeDtypeStruct((N, D), F32),
        scratch_shapes=[pltpu.VMEM((tm, D), F32)],
        compiler_params=_cparams(("arbitrary", "arbitrary")),
        name="moe_experts",
    )(x, h2, comb, wg, wu, wd, gfin)


def _pad_cols(w, n):
    return jnp.pad(w, ((0, 0), (0, n - w.shape[1])))


def _pack_w1(w_in_l):
    o = np.cumsum([0, 512, 768, 24, MLA_Q_RANK, MLA_KV_RANK, MLA_ROPE, 1536, FOX_HEADS])
    nq, nkv, ng, cq, ckv, kr, fox, ff = (w_in_l[:, o[i]:o[i + 1]] for i in range(8))
    half = MLA_ROPE // 2
    kr1 = jnp.tile(kr[:, :half], (1, MLA_HEADS))
    kr2 = jnp.tile(kr[:, half:], (1, MLA_HEADS))
    w1 = jnp.concatenate([nq, nkv, _pad_cols(ng, LANES), cq, ckv, kr1, kr2, fox, _pad_cols(ff, LANES)], axis=1)
    return w1.astype(BF16), w_in_l[:, o[8]:]


def _pack_wuq(w_uq_l):
    w = w_uq_l.reshape(MLA_Q_RANK, MLA_HEADS, MLA_QK)
    half = MLA_ROPE // 2
    nope = w[:, :, :MLA_NOPE].reshape(MLA_Q_RANK, -1)
    r1 = w[:, :, MLA_NOPE:MLA_NOPE + half].reshape(MLA_Q_RANK, -1)
    r2 = w[:, :, MLA_NOPE + half:].reshape(MLA_Q_RANK, -1)
    return jnp.concatenate([nope, r1, r2], axis=1).astype(BF16)


def _heads(a, B, T, H):
    return a.reshape(B, T, H, -1).transpose(0, 2, 1, 3)


def _tile_major_t(k, tk):
    B, H, T, d = k.shape
    return k.reshape(B, H, T // tk, tk, d).transpose(0, 1, 2, 4, 3)


def _pad_feat(a, n):
    return jnp.pad(a, ((0, 0),) * (a.ndim - 1) + ((0, n - a.shape[-1]),))


def _ssum_matrix(ncp):
    n = np.arange(ncp)[None, :]
    j = np.arange(LANES)[:, None]
    d = n - 4 * j
    m = np.where((d == 0) | (d == 4), 1.0, np.where((d >= 1) & (d <= 3), 2.0, 0.0))
    return jnp.asarray(m, BF16)


def kernel(x, positions, rel_bias, norm_attn, w_in, b_merge, cmp_pe, cmp_k_w1, cmp_k_b1, cmp_k_w2, cmp_v_w1, cmp_v_b1,
           cmp_v_w2, mla_q_norm, mla_kv_norm, mla_w_uq, mla_w_uk, mla_w_uv, fox_b_f, w_o_nsa, w_o_mla, w_o_fox, w_out,
           norm_ffn, moe_w_grp, moe_b_grp, moe_w_router, moe_b_router, moe_w_gate, moe_w_up, moe_w_down, norm_final):
    B, T, D = x.shape
    N = B * T
    n_sub = T // CMP_STRIDE
    nqb = T // QB
    tm_in = min(512, T)
    fl_tq, fl_hb = min(512, T), 2
    tm_merge = 512
    tm_moe = 1024
    half = MLA_ROPE // 2

    tbc, tbs, tbw = _bias_tables(rel_bias)
    tbc = tbc.reshape(2, CMP_TAB, NSA_G, GQ).transpose(0, 2, 1, 3)
    tbc_hi, tbc_lo = tbc[0], tbc[1]
    tbs = tbs.reshape(4, SEL_TK, NSA_G, GQ).transpose(2, 0, 1, 3)
    tbw = tbw.reshape(WINDOW + QB, NSA_G, GQ).transpose(1, 0, 2)
    ssum = _ssum_matrix(n_sub)
    blk_onehot = jnp.asarray(np.arange(T)[:, None] // SLC_BLOCK == np.arange(LANES)[None, :], BF16)
    pad_ind = jnp.asarray((np.arange(T + WINDOW) < WINDOW)[:, None], BF16)
    invf = jnp.tile(ROPE_THETA ** (-jnp.arange(half, dtype=F32) / half), MLA_HEADS).reshape(1, LANES)
    tri = jnp.asarray(np.tril(np.ones((tm_in, tm_in), np.float32)), BF16)
    pos3 = positions.reshape(B, T, 1)
    ones3 = jnp.ones((B, T, FOX_HEADS, 3), BF16)

    for l in range(DEPTH):
        w1, w_merge = _pack_w1(w_in[l])
        fb = _pad_cols(fox_b_f[l].reshape(1, -1), LANES)
        nq, nkv, ng, mq, mk, mv, mkr, fqkv, fcum = _inproj(
            x, pos3, norm_attn[l].reshape(1, D), w1, _pack_wuq(mla_w_uq[l]), mla_w_uk[l].astype(BF16),
            mla_w_uv[l].astype(BF16), mla_q_norm[l].reshape(1, -1), mla_kv_norm[l].reshape(1, -1), fb, invf, tri, tm_in)

        kv6 = nkv.reshape(B, T, 3, 2, NSA_G, DH)
        sub = kv6[:, :, 0].transpose(0, 3, 2, 1, 4).reshape(B, NSA_G, 2, n_sub, CMP_STRIDE * DH)
        pe = jnp.broadcast_to(cmp_pe[l].reshape(1, 2, CMP_STRIDE * DH), (2, 2, CMP_STRIDE * DH))
        cw1 = jnp.stack([cmp_k_w1[l], cmp_v_w1[l]]).reshape(2, 2, CMP_STRIDE * DH, CMP_HIDDEN).astype(BF16)
        cb1 = jnp.stack([cmp_k_b1[l], cmp_v_b1[l]]).reshape(2, 1, CMP_HIDDEN)
        cw2 = jnp.stack([cmp_k_w2[l], cmp_v_w2[l]]).astype(BF16)
        cmp = _compress(sub, pe, cw1, cb1, cw2)
        grp = lambda br, s: kv6[:, :, br, s].transpose(0, 2, 1, 3)
        qt_a = nq.reshape(B, nqb, QB, NSA_G, NSA_HPG, DH).transpose(0, 3, 1, 5, 4, 2).reshape(B, NSA_G, nqb, DH, GQ)
        ksa = jnp.concatenate([grp(1, 0), jnp.broadcast_to(blk_onehot, (B, NSA_G, T, LANES))], axis=-1)
        front = lambda a: jnp.pad(a, ((0, 0), (0, 0), (WINDOW, 0), (0, 0)))
        kwa = jnp.concatenate([front(grp(2, 0)), jnp.broadcast_to(pad_ind, (B, NSA_G, T + WINDOW, 1)),
                               jnp.zeros((B, NSA_G, T + WINDOW, DH - 1), BF16)], axis=-1)
        gates_t = ng[..., :3 * NSA_HEADS].reshape(B, nqb, QB, NSA_G, NSA_HPG, 3).transpose(0, 3, 1, 5, 4, 2)
        gates_t = jnp.pad(gates_t.reshape(B, NSA_G, nqb, 3, GQ), ((0, 0), (0, 0), (0, 0), (0, 5), (0, 0)))
        o_a = _nsa(qt_a, cmp[:, :, 0], cmp[:, :, 1].transpose(0, 1, 3, 2), ksa, _tile_major_t(grp(1, 1), SEL_TK), kwa,
                   _tile_major_t(front(grp(2, 1)), QB), gates_t, tbc_hi, tbc_lo, tbs, tbw, ssum)
        out_a = o_a.reshape(B, NSA_G, nqb, DH, NSA_HPG, QB).transpose(0, 2, 5, 1, 4, 3).reshape(N, NSA_HEADS * DH)

        q_b = jnp.concatenate([mq[..., :512].reshape(B, T, MLA_HEADS, MLA_NOPE),
                               mq[..., 512:640].reshape(B, T, MLA_HEADS, half),
                               mq[..., 640:768].reshape(B, T, MLA_HEADS, half)], axis=-1)
        k_b = jnp.concatenate([mk.reshape(B, T, MLA_HEADS, MLA_NOPE), mkr[..., :128].reshape(B, T, MLA_HEADS, half),
                               mkr[..., 128:].reshape(B, T, MLA_HEADS, half)], axis=-1)
        o_b = _flash(_pad_feat(q_b, LANES).transpose(0, 2, 3, 1), _pad_feat(k_b, LANES).transpose(0, 2, 1, 3),
                     _tile_major_t(_heads(mv, B, T, MLA_HEADS), fl_tq), tq=fl_tq, hb=fl_hb)
        out_b = o_b.transpose(0, 3, 1, 2).reshape(N, -1)

        cum3 = fcum.reshape(B, T, 3, LANES)[..., :FOX_HEADS].transpose(0, 1, 3, 2)
        q_c = jnp.concatenate([fqkv[..., :512].reshape(B, T, FOX_HEADS, DH), cum3, ones3], axis=-1)
        k_c = jnp.concatenate([fqkv[..., 512:1024].reshape(B, T, FOX_HEADS, DH), ones3, -cum3], axis=-1)
        o_c = _flash(_pad_feat(q_c, LANES).transpose(0, 2, 3, 1), _pad_feat(k_c, LANES).transpose(0, 2, 1, 3),
                     _tile_major_t(_heads(fqkv[..., 1024:], B, T, FOX_HEADS), fl_tq), tq=fl_tq, hb=fl_hb)
        out_c = o_c.transpose(0, 3, 1, 2).reshape(N, -1)

        wo = jnp.stack([w_o_nsa[l], w_o_mla[l], w_o_fox[l]]).astype(BF16)
        pad_b = lambda b, n: jnp.concatenate([b, jnp.full((LANES - n,), NEG, F32)]).reshape(1, LANES)
        x2, h2, comb = _merge(
            x.reshape(N, D), out_a, out_b, out_c, norm_attn[l].reshape(1, D), w_merge.astype(BF16),
            b_merge[l].reshape(1, -1), wo, w_out[l].astype(BF16), norm_ffn[l].reshape(1, D),
            _pad_cols(moe_w_grp[l], LANES), pad_b(moe_b_grp[l], N_GROUPS),
            _pad_cols(moe_w_router[l], LANES), pad_b(moe_b_router[l], N_EXPERTS), tm_merge)

        x = _moe(x2, h2, comb, moe_w_gate[l].astype(BF16), moe_w_up[l].astype(BF16), moe_w_down[l].astype(BF16),
                 norm_final.reshape(1, D), tm_moe, final_norm=(l == DEPTH - 1)).reshape(B, T, D)
    return x
```

```python
import functools
import math

import numpy as np
import jax
import jax.numpy as jnp
from jax import lax
from jax.experimental import pallas as pl
from jax.experimental.pallas import tpu as pltpu

F32 = jnp.float32
BF16 = jnp.bfloat16

D_MODEL = 1024
DEPTH = 2
EPS = 1e-6
NEG = -1e30
REMOVED = -3e38

NSA_HEADS = 8
NSA_G = 2
NSA_HPG = 4
DH = 64
CMP_STRIDE = 16
CMP_HIDDEN = 128
SLC_BLOCK = 64
SLC_TOPK = 16
SLC_LOCAL = 2
WINDOW = 512
QB = 128
GQ = NSA_HPG * QB
N_WIN = WINDOW + QB
WIN_TAB = WINDOW + N_WIN
SEL_TK = 256
CHAIN_W = 256
SUM_ROWS = 16
CMP_BAND = 32
CMP_TAB = 48
MLA_HEADS = 8
MLA_Q_RANK = 256
MLA_KV_RANK = 128
MLA_NOPE = 64
MLA_ROPE = 32
MLA_QK = 96
ROPE_THETA = 10000.0
FOX_HEADS = 8
REL_BUCKETS = 32
REL_MAX_DIST = 128
N_GROUPS = 4
EPG = 8
N_EXPERTS = 32
EXPERT_DIM = 256

LANES = 128
LOG2E = 1.4426950408889634
VMEM_LIMIT = 56 * 1024 * 1024


def _bucket_thresholds():
    n = np.arange(0, 4 * REL_MAX_DIST)
    max_exact = REL_BUCKETS // 2
    lr = np.log(np.maximum(n, 1).astype(np.float32) / np.float32(max_exact)) / np.float32(math.log(REL_MAX_DIST / max_exact))
    large = max_exact + (lr.astype(np.float32) * np.float32(REL_BUCKETS - max_exact)).astype(np.int32)
    bucket = np.where(n < max_exact, n, np.minimum(large, REL_BUCKETS - 1))
    return [int(np.argmax(bucket >= b)) for b in range(1, REL_BUCKETS)]


BUCKET_THR = _bucket_thresholds()
assert BUCKET_THR[-1] <= REL_MAX_DIST


def _cparams(sem):
    return pltpu.CompilerParams(dimension_semantics=sem, vmem_limit_bytes=VMEM_LIMIT)


def _split3(x):
    hi = x.astype(BF16)
    r = x - hi.astype(F32)
    mid = r.astype(BF16)
    lo = (r - mid.astype(F32)).astype(BF16)
    return hi, mid, lo


def _lane_pick(x, idx):
    lane = lax.broadcasted_iota(jnp.int32, x.shape, x.ndim - 1)
    return jnp.sum(jnp.where(lane == idx, x, 0.0), axis=-1, keepdims=True)


def _online_softmax_steps(logits, vts, carries):
    stats = []
    for s, (m, _) in zip(logits, carries):
        m_new = jnp.maximum(m, jnp.max(s, axis=0, keepdims=True))
        stats.append((m_new, jnp.exp2(m - m_new), jnp.exp2(s - m_new).astype(BF16)))
    out = []
    for (m_new, alpha, p), vt, (_, acc) in zip(stats, vts, carries):
        out.append((m_new, alpha * acc + jnp.dot(vt, p, preferred_element_type=F32)))
    return tuple(out)


def _softmax_carry(dv, width):
    return jnp.full((1, width), NEG, F32), jnp.zeros((dv + SUM_ROWS, width), F32)


def _softmax_result(carry, dv):
    _, acc = carry
    return acc[0:dv] * (1.0 / acc[dv:dv + 1])


def _pipelined_softmax_loop(n_steps, qk_fn, vt_fn, carries, logit_sc, prob_sc):
    n_chains = len(carries)

    def value_products(kt, slot):
        return [jnp.dot(vt, prob_sc[slot, i], preferred_element_type=F32) for i, vt in enumerate(vt_fn(kt))]

    def keep(logits, slot):
        for i, s in enumerate(logits):
            logit_sc[slot, i] = s
        return tuple(jnp.max(s, axis=0, keepdims=True) for s in logits)

    def stage(kt, cur, nxt, state):
        carries, tile_max = state
        upcoming = qk_fn(kt + 1)
        pvs = value_products(jnp.maximum(kt - 1, 0), nxt)
        out = []
        for i, (pv, tm, (m, acc)) in enumerate(zip(pvs, tile_max, carries)):
            m_new = jnp.maximum(m, tm)
            prob_sc[cur, i] = jnp.exp2(logit_sc[cur, i] - m_new).astype(BF16)
            out.append((m_new, jnp.exp2(m - m_new) * (acc + pv)))
        return tuple(out), keep(upcoming, nxt)

    prob_sc[1] = jnp.zeros(prob_sc.shape[1:], BF16)
    state = (tuple(carries), keep(qk_fn(0), 0))
    state = lax.fori_loop(0, n_steps // 2, lambda i, c: stage(2 * i + 1, 1, 0, stage(2 * i, 0, 1, c)), state)
    odd = n_steps % 2
    carries, _ = lax.cond(odd == 1, lambda c: stage(n_steps - 1, 0, 1, c), lambda c: c, state)
    pvs = value_products(jnp.maximum(n_steps - 1, 0), 1 - odd)
    logits = tuple(logit_sc[odd, i] for i in range(n_chains))
    return logits, tuple((m, acc + pv) for (m, acc), pv in zip(carries, pvs))


def _bias_table_kernel(tab_ref, tbc_ref, tbs_ref, tbw_ref):
    h = pl.program_id(0)
    far = tab_ref[REL_BUCKETS - 1, h]

    def delta(dist):
        val = jnp.full(dist.shape, tab_ref[0, h], F32)
        for b in range(1, REL_BUCKETS):
            val = jnp.where(dist >= BUCKET_THR[b - 1], tab_ref[b, h], val)
        return (val - far) * LOG2E

    j = lax.broadcasted_iota(jnp.int32, (CMP_TAB, QB), 0)
    q = lax.broadcasted_iota(jnp.int32, (CMP_TAB, QB), 1)
    dist = q - CMP_STRIDE * j + (2 * QB - 2 * CMP_STRIDE + 1)
    band = jnp.where(j < CMP_BAND, jnp.where(dist >= 0, delta(dist), NEG), jnp.where(j == CMP_BAND, NEG, 0.0))
    band_hi = band.astype(BF16)
    tbc_ref[0] = band_hi
    tbc_ref[1] = (band - band_hi.astype(F32)).astype(BF16)
    c = lax.broadcasted_iota(jnp.int32, (SEL_TK, QB), 0)
    q = lax.broadcasted_iota(jnp.int32, (SEL_TK, QB), 1)
    for var, off in enumerate((SEL_TK, 0, QB)):
        dist = q + off - c
        tbs_ref[var] = jnp.where(dist >= 0, delta(dist), NEG)
    tbs_ref[3] = jnp.zeros((SEL_TK, QB), F32)
    c = lax.broadcasted_iota(jnp.int32, (WIN_TAB, QB), 0)
    q = lax.broadcasted_iota(jnp.int32, (WIN_TAB, QB), 1)
    dist = q + WINDOW - c
    tbw_ref[...] = jnp.where((dist >= 0) & (dist < WINDOW), delta(dist), NEG)


def _bias_tables(rel_bias):
    hq = NSA_HEADS * QB
    return pl.pallas_call(
        _bias_table_kernel,
        grid=(NSA_HEADS,),
        in_specs=[pl.BlockSpec(memory_space=pltpu.SMEM)],
        out_specs=[
            pl.BlockSpec((2, CMP_TAB, QB), lambda h: (0, 0, h)),
            pl.BlockSpec((4, SEL_TK, QB), lambda h: (0, 0, h)),
            pl.BlockSpec((WIN_TAB, QB), lambda h: (0, h)),
        ],
        out_shape=[
            jax.ShapeDtypeStruct((2, CMP_TAB, hq), BF16),
            jax.ShapeDtypeStruct((4, SEL_TK, hq), F32),
            jax.ShapeDtypeStruct((WIN_TAB, hq), F32),
        ],
        compiler_params=_cparams(("arbitrary",)),
        name="bias_tables",
    )(rel_bias)


HSEG = 128
_SEG_WIDTHS = (("NQ", 512), ("NCMP", 256), ("KS", NSA_G * HSEG), ("VS", 128), ("KW", NSA_G * HSEG), ("VW", 128),
               ("NG", 128), ("CQ", MLA_Q_RANK), ("CKV", MLA_KV_RANK), ("KRA", HSEG), ("KRB", HSEG), ("FQ", 512),
               ("FK", FOX_HEADS * HSEG), ("FV", 512), ("FF", 128))
SEG = {}
_off = 0
for _name, _w in _SEG_WIDTHS:
    SEG[_name] = (_off, _off + _w)
    _off += _w


def _rms(x, g):
    return x * lax.rsqrt(jnp.mean(x * x, axis=-1, keepdims=True) + EPS) * g


def _inproj_kernel(x_ref, pos_ref, g_ref, w1_ref, wuq_ref, wuk_ref, wuv_ref, qn_ref, kvn_ref, fb_ref, rot_ref,
                   tri_ref, nq_ref, ncmp_ref, ksa_ref, vs_ref, kw_ref, vw_ref, ng_ref, mq_ref, mk_ref, mv_ref,
                   fq_ref, fk_ref, fv_ref, fcum_ref, carry_ref):
    i = pl.program_id(1)
    tm = x_ref.shape[0]
    h = _rms(x_ref[...], g_ref[...]).astype(BF16)

    def seg(name):
        a, b = SEG[name]
        return jnp.dot(h, w1_ref[:, a:b], preferred_element_type=F32)

    nq_ref[...] = (seg("NQ") * (DH ** -0.5 * LOG2E)).astype(BF16)
    ncmp_ref[...] = seg("NCMP").astype(BF16)
    vs_ref[...] = seg("VS").astype(BF16)
    kw_ref[...] = seg("KW").astype(BF16)
    vw_ref[...] = seg("VW").astype(BF16)
    ng_ref[...] = jax.nn.sigmoid(seg("NG"))
    ks = seg("KS").astype(BF16)
    tok = i * tm + lax.broadcasted_iota(jnp.int32, (tm, LANES), 0)
    blk = (tok // SLC_BLOCK == lax.broadcasted_iota(jnp.int32, (tm, LANES), 1))
    blk = jnp.where(blk, 1.0, 0.0).astype(BF16)
    for g in range(NSA_G):
        ksa_ref[:, 2 * g * HSEG:(2 * g + 1) * HSEG] = ks[:, g * HSEG:(g + 1) * HSEG]
        ksa_ref[:, (2 * g + 1) * HSEG:(2 * g + 2) * HSEG] = blk

    ang = pos_ref[...].astype(F32) * rot_ref[0:1, :]
    cos = jnp.cos(ang)
    sin = jnp.sin(ang) * rot_ref[1:2, :]
    cos_h = jnp.concatenate([cos] * MLA_HEADS, axis=1)
    sin_h = jnp.concatenate([sin] * MLA_HEADS, axis=1)
    width = MLA_HEADS * HSEG
    c_q = _rms(seg("CQ"), qn_ref[...]).astype(BF16)
    q_a = jnp.dot(c_q, wuq_ref[:, 0:width], preferred_element_type=F32)
    q_b = jnp.dot(c_q, wuq_ref[:, width:2 * width], preferred_element_type=F32)
    mq_ref[...] = ((q_a * cos_h + q_b * sin_h) * (MLA_QK ** -0.5 * LOG2E)).astype(BF16)
    c_kv = _rms(seg("CKV"), kvn_ref[...]).astype(BF16)
    k_rope = seg("KRA") * cos + seg("KRB") * sin
    k_nope = jnp.dot(c_kv, wuk_ref[...], preferred_element_type=F32)
    mk_ref[...] = (k_nope + jnp.concatenate([k_rope] * MLA_HEADS, axis=1)).astype(BF16)
    mv_ref[...] = jnp.dot(c_kv, wuv_ref[...], preferred_element_type=F32).astype(BF16)

    fq_ref[...] = (seg("FQ") * (DH ** -0.5 * LOG2E)).astype(BF16)
    fk_ref[...] = seg("FK").astype(BF16)
    fv_ref[...] = seg("FV").astype(BF16)
    log_f = jax.nn.log_sigmoid(seg("FF") + fb_ref[...])

    @pl.when(i == 0)
    def _():
        carry_ref[...] = jnp.zeros_like(carry_ref)

    tri = tri_ref[...]
    cum = carry_ref[...]
    for part in _split3(log_f):
        cum = cum + jnp.dot(tri, part, preferred_element_type=F32)
    fcum_ref[...] = cum * LOG2E
    carry_ref[...] = cum[tm - 1:tm, :]


_INPROJ_OUTS = ((512, BF16), (256, BF16), (2 * NSA_G * HSEG, BF16), (128, BF16), (NSA_G * HSEG, BF16), (128, BF16),
                (128, F32), (MLA_HEADS * HSEG, BF16), (MLA_HEADS * HSEG, BF16), (512, BF16), (512, BF16),
                (FOX_HEADS * HSEG, BF16), (512, BF16), (128, F32))


def _inproj(x, pos, g, w1, wuq, wuk, wuv, qn, kvn, fb, rot, tri, tm):
    B, T, D = x.shape
    row = lambda w: pl.BlockSpec((None, tm, w), lambda b, i: (b, i, 0))
    full = lambda a: pl.BlockSpec(a.shape, lambda b, i: (0,) * a.ndim)
    return pl.pallas_call(
        _inproj_kernel,
        grid=(B, T // tm),
        in_specs=[row(D), row(1), full(g), full(w1), full(wuq), full(wuk), full(wuv), full(qn), full(kvn), full(fb),
                  full(rot), full(tri)],
        out_specs=[row(w) for w, _ in _INPROJ_OUTS],
        out_shape=[jax.ShapeDtypeStruct((B, T, w), dt) for w, dt in _INPROJ_OUTS],
        scratch_shapes=[pltpu.VMEM((1, LANES), F32)],
        compiler_params=_cparams(("arbitrary", "arbitrary")),
        name="inproj",
    )(x, pos, g, w1, wuq, wuk, wuv, qn, kvn, fb, rot, tri)


def _flash_kernel(*refs, tq, hb, key_bias):
    if key_bias:
        qt_ref, k_ref, vt_ref, kb_ref, o_ref, logit_sc, prob_sc = refs
    else:
        qt_ref, k_ref, vt_ref, o_ref, logit_sc, prob_sc = refs
    qi = pl.program_id(2)
    dk = qt_ref.shape[1]
    dv = vt_ref.shape[2] - SUM_ROWS
    chains = [(j, c) for j in range(hb) for c in range(tq // CHAIN_W)]
    qts = [qt_ref[j, :, c * CHAIN_W:(c + 1) * CHAIN_W] for j, c in chains]

    def qk(kt):
        st = pl.multiple_of(kt * tq, tq)
        out = []
        for (j, _), qt in zip(chains, qts):
            s = jnp.dot(k_ref[pl.ds(st, tq), j * HSEG:j * HSEG + dk], qt, preferred_element_type=F32)
            out.append(s + kb_ref[j, pl.ds(st, tq), :] if key_bias else s)
        return out

    init = tuple(_softmax_carry(dv, CHAIN_W) for _ in chains)
    diag, carries = _pipelined_softmax_loop(qi, qk, lambda kt: [vt_ref[j, kt] for j, _ in chains], init,
                                            logit_sc, prob_sc)
    kpos = lax.broadcasted_iota(jnp.int32, (tq, CHAIN_W), 0)
    qpos = lax.broadcasted_iota(jnp.int32, (tq, CHAIN_W), 1)
    logits = [jnp.where(kpos <= qpos + c * CHAIN_W, s, NEG) for (_, c), s in zip(chains, diag)]
    final = _online_softmax_steps(logits, [vt_ref[j, qi] for j, _ in chains], carries)
    for (j, c), carry in zip(chains, final):
        o_ref[j, :, c * CHAIN_W:(c + 1) * CHAIN_W] = _softmax_result(carry, dv).astype(o_ref.dtype)


def _flash(qt, k, vt, key_bias=None, *, tq, hb):
    B, H, dk, T = qt.shape
    dv = vt.shape[3] - SUM_ROWS
    in_specs = [
        pl.BlockSpec((None, hb, dk, tq), lambda b, h, i: (b, h, 0, i)),
        pl.BlockSpec((None, T, hb * HSEG), lambda b, h, i: (b, 0, h)),
        pl.BlockSpec((None, hb, T // tq, dv + SUM_ROWS, tq), lambda b, h, i: (b, h, 0, 0, 0)),
    ]
    args = [qt, k, vt]
    if key_bias is not None:
        in_specs.append(pl.BlockSpec((None, hb, T, 1), lambda b, h, i: (b, h, 0, 0)))
        args.append(key_bias)
    return pl.pallas_call(
        functools.partial(_flash_kernel, tq=tq, hb=hb, key_bias=key_bias is not None),
        grid=(B, H // hb, T // tq),
        in_specs=in_specs,
        out_specs=pl.BlockSpec((None, hb, dv, tq), lambda b, h, i: (b, h, 0, i)),
        out_shape=jax.ShapeDtypeStruct((B, H, dv, T), BF16),
        scratch_shapes=[pltpu.VMEM((2, hb * tq // CHAIN_W, tq, CHAIN_W), F32),
                        pltpu.VMEM((2, hb * tq // CHAIN_W, tq, CHAIN_W), BF16)],
        compiler_params=_cparams(("arbitrary", "arbitrary", "arbitrary")),
        name="flash_attention",
    )(*args)


def _compress_kernel(sub_ref, pe_ref, w1_ref, b1_ref, w2_ref, o_ref):
    sub = sub_ref[...].astype(F32)
    n_sub = sub.shape[0]
    a = jnp.dot((sub + pe_ref[0:1, :]).astype(BF16), w1_ref[0], preferred_element_type=F32)
    b = jnp.dot((sub + pe_ref[1:2, :]).astype(BF16), w1_ref[1], preferred_element_type=F32)
    hid = jax.nn.gelu(a + pltpu.roll(b, n_sub - 1, 0) + b1_ref[...])
    o_ref[...] = jnp.dot(hid.astype(BF16), w2_ref[...], preferred_element_type=F32).astype(o_ref.dtype)


def _compress(sub, pe, w1, b1, w2):
    B, G, _, n_sub, _ = sub.shape
    return pl.pallas_call(
        _compress_kernel,
        grid=(B, G, 2),
        in_specs=[
            pl.BlockSpec((None, None, None, n_sub, 1024), lambda b, g, s: (b, g, s, 0, 0)),
            pl.BlockSpec((None, 2, 1024), lambda b, g, s: (s, 0, 0)),
            pl.BlockSpec((None, 2, 1024, CMP_HIDDEN), lambda b, g, s: (s, 0, 0, 0)),
            pl.BlockSpec((None, 1, CMP_HIDDEN), lambda b, g, s: (s, 0, 0)),
            pl.BlockSpec((None, CMP_HIDDEN, DH), lambda b, g, s: (s, 0, 0)),
        ],
        out_specs=pl.BlockSpec((None, None, None, n_sub, DH), lambda b, g, s: (b, g, s, 0, 0)),
        out_shape=jax.ShapeDtypeStruct((B, G, 2, n_sub, DH), BF16),
        compiler_params=_cparams(("arbitrary", "arbitrary", "arbitrary")),
        name="nsa_compress",
    )(sub, pe, w1, b1, w2)


def _nsa_kernel(qt_ref, kc_ref, vct_ref, ksa_ref, vst_ref, kw_ref, vwt_ref, gate_ref, tbc_hi_ref, tbc_lo_ref,
                tbs_ref, tbw_ref, ssum_ref, o_ref, logit_sc, prob_sc):
    qb = pl.program_id(2)
    t0 = qb * QB
    ncp = kc_ref.shape[0]
    qt = qt_ref[...]

    s = jnp.dot(kc_ref[...], qt, preferred_element_type=F32)
    n0 = (QB // CMP_STRIDE) * qb - 16
    rel = lax.broadcasted_iota(jnp.int32, (ncp, CMP_TAB), 0) - n0
    col = lax.broadcasted_iota(jnp.int32, (ncp, CMP_TAB), 1)
    place = jnp.where(jnp.minimum(rel, CMP_BAND) == col, 1.0, 0.0).astype(BF16)
    s = s + (jnp.dot(place, tbc_hi_ref[...], preferred_element_type=F32)
             + jnp.dot(place, tbc_lo_ref[...], preferred_element_type=F32))
    m = jnp.max(s, axis=0, keepdims=True)
    e = jnp.exp2(s - m)
    p_c = e * jnp.where(m > 0.5 * NEG, 1.0 / jnp.sum(e, axis=0, keepdims=True), 0.0)
    o_c = jnp.dot(vct_ref[...], p_c.astype(BF16), preferred_element_type=F32)

    imp = p_c[:, 0:QB]
    for hh in range(1, NSA_HPG):
        imp = imp + p_c[:, hh * QB:(hh + 1) * QB]
    ssum = ssum_ref[...]
    p_slc = sum(jnp.dot(ssum, part, preferred_element_type=F32) for part in _split3(imp))
    jj = lax.broadcasted_iota(jnp.int32, (LANES, QB), 0)
    blk_cur = (t0 + lax.broadcasted_iota(jnp.int32, (LANES, QB), 1)) // SLC_BLOCK
    causal_b = jj <= blk_cur
    forced = (jj == 0) | (causal_b & (jj > blk_cur - SLC_LOCAL))
    score = jnp.where(causal_b, jnp.where(forced, 1e9, p_slc), NEG)
    jf = jj.astype(F32)
    sel = jnp.zeros((LANES, QB), F32)
    for _ in range(SLC_TOPK):
        mx = jnp.max(score, axis=0, keepdims=True)
        first = jnp.min(jnp.where(score == mx, jf, float(LANES)), axis=0, keepdims=True)
        hit = jf == first
        sel = jnp.where(hit, 1.0, sel)
        score = jnp.where(hit, REMOVED, score)
    unsel = jnp.where(causal_b & (sel > 0.0), 0.0, NEG).astype(BF16)

    first = jnp.maximum(qb - WINDOW // QB, 0)
    start = pl.multiple_of(first * QB, QB)
    shift = pl.multiple_of(start + WINDOW - t0, QB)
    s = (jnp.dot(kw_ref[pl.ds(start, N_WIN), 0:DH], qt, preferred_element_type=F32)
         + tbw_ref[pl.ds(shift, N_WIN), :])
    e = jnp.exp2(s - jnp.max(s, axis=0, keepdims=True))
    inv = 1.0 / jnp.sum(e, axis=0, keepdims=True)
    e = e.astype(BF16)
    o_w = jnp.zeros((DH, GQ), F32)
    for r in range(N_WIN // QB):
        o_w = o_w + jnp.dot(vwt_ref[first + r], e[r * QB:(r + 1) * QB], preferred_element_type=F32)
    o_w = o_w * inv

    qa = jnp.concatenate([qt, jnp.zeros((HSEG - DH, GQ), BF16), jnp.concatenate([unsel] * NSA_HPG, axis=1)],
                         axis=0)
    n_tiles = qb // 2 + 1
    even = (qb % 2) == 0

    chunks = [(c * CHAIN_W, (c + 1) * CHAIN_W) for c in range(GQ // CHAIN_W)]
    qas = [qa[:, a:b] for a, b in chunks]

    def sel_qk(kt):
        start = pl.multiple_of(kt * SEL_TK, SEL_TK)
        return [jnp.dot(ksa_ref[pl.ds(start, SEL_TK), :], qc, preferred_element_type=F32)
                for qc in qas]

    def sel_vt(kt):
        return [vst_ref[kt]] * len(chunks)

    def near_step(kt, logits, var, carries):
        logits = [s + tbs_ref[var, :, a:b] for s, (a, b) in zip(logits, chunks)]
        return _online_softmax_steps(logits, sel_vt(kt), carries)

    carries = tuple(_softmax_carry(DH, CHAIN_W) for _ in chunks)
    n_far = jnp.maximum(n_tiles - 2, 0)
    logits, carries = _pipelined_softmax_loop(n_far, sel_qk, sel_vt, carries, logit_sc, prob_sc)
    logits, carries = lax.cond(
        n_tiles >= 2,
        lambda lg, c: (tuple(sel_qk(n_tiles - 1)), near_step(n_tiles - 2, lg, jnp.where(even, 0, 3), c)),
        lambda lg, c: (lg, c), logits, carries)
    carries = near_step(n_tiles - 1, logits, jnp.where(even, 1, 2), carries)
    o_s = jnp.concatenate([_softmax_result(c, DH) for c in carries], axis=1)

    gates = gate_ref[...]
    o_ref[...] = (gates[0:1] * o_c + gates[1:2] * o_s + gates[2:3] * o_w).astype(o_ref.dtype)


def _nsa(qt, kc, vct, ksa, vst, kw, vwt, gates, tbc_hi, tbc_lo, tbs, tbw, ssum):
    B, G, nqb, _, _ = qt.shape
    T = ksa.shape[1]
    bg = lambda a: pl.BlockSpec((None, None) + a.shape[2:], lambda b, g, i: (b, g) + (0,) * (a.ndim - 2))
    lane_g = lambda w: pl.BlockSpec((None, T, w), lambda b, g, i: (b, 0, g))
    per_g = lambda a: pl.BlockSpec((None,) + a.shape[1:], lambda b, g, i: (g,) + (0,) * (a.ndim - 1))
    per_blk = lambda a: pl.BlockSpec((None, None, None) + a.shape[3:], lambda b, g, i: (b, g, i) + (0,) * (a.ndim - 3))
    return pl.pallas_call(
        _nsa_kernel,
        grid=(B, G, nqb),
        in_specs=[per_blk(qt), bg(kc), bg(vct), lane_g(2 * HSEG), bg(vst), lane_g(HSEG), bg(vwt), per_blk(gates),
                  per_g(tbc_hi), per_g(tbc_lo), per_g(tbs), per_g(tbw),
                  pl.BlockSpec(ssum.shape, lambda b, g, i: (0, 0))],
        out_specs=pl.BlockSpec((None, None, None, DH, GQ), lambda b, g, i: (b, g, i, 0, 0)),
        out_shape=jax.ShapeDtypeStruct((B, G, nqb, DH, GQ), BF16),
        scratch_shapes=[pltpu.VMEM((2, GQ // CHAIN_W, SEL_TK, CHAIN_W), F32),
                        pltpu.VMEM((2, GQ // CHAIN_W, SEL_TK, CHAIN_W), BF16)],
        compiler_params=_cparams(("arbitrary", "arbitrary", "arbitrary")),
        name="nsa_attention",
    )(qt, kc, vct, ksa, vst, kw, vwt, gates, tbc_hi, tbc_lo, tbs, tbw, ssum)


def _merge_kernel(x_ref, oa_ref, ob_ref, oc_ref, ga_ref, wm_ref, bm_ref, wo_ref, wout_ref, gf_ref, wgrp_ref, bgrp_ref,
                  wrt_ref, brt_ref, xo_ref, h2_ref, comb_ref):
    x = x_ref[...]
    h = _rms(x, ga_ref[...]).astype(BF16)
    mixed = None
    for i, o_ref in enumerate((oa_ref, ob_ref, oc_ref)):
        gate = jax.nn.sigmoid(jnp.dot(h, wm_ref[:, i * D_MODEL:(i + 1) * D_MODEL], preferred_element_type=F32)
                              + bm_ref[:, i * D_MODEL:(i + 1) * D_MODEL])
        term = gate * jnp.dot(o_ref[...], wo_ref[i], preferred_element_type=F32)
        mixed = term if mixed is None else mixed + term
    xn = x + jnp.dot(mixed.astype(BF16), wout_ref[...], preferred_element_type=F32)
    xo_ref[...] = xn
    h2 = _rms(xn, gf_ref[...])
    h2_ref[...] = h2.astype(BF16)

    lane = lax.broadcasted_iota(jnp.int32, (x.shape[0], LANES), 1).astype(F32)
    gl = jnp.dot(h2, wgrp_ref[...], preferred_element_type=F32, precision=lax.Precision.HIGHEST) + bgrp_ref[...]
    ge = jnp.exp(gl - jnp.max(gl, axis=-1, keepdims=True))
    gp = ge / jnp.sum(ge, axis=-1, keepdims=True)
    grp_top = jnp.max(gp, axis=-1, keepdims=True)
    g_idx = jnp.min(jnp.where(gp == grp_top, lane, float(LANES)), axis=-1, keepdims=True)
    el = jnp.dot(h2, wrt_ref[...], preferred_element_type=F32, precision=lax.Precision.HIGHEST) + brt_ref[...]
    in_grp = jnp.floor(lane * (1.0 / EPG)) == g_idx
    el = jnp.where(in_grp, el, NEG)
    ee = jnp.exp(el - jnp.max(el, axis=-1, keepdims=True))
    ep = jnp.where(in_grp, ee / jnp.sum(ee, axis=-1, keepdims=True), -1.0)
    p1 = jnp.max(ep, axis=-1, keepdims=True)
    i1 = jnp.min(jnp.where(ep == p1, lane, float(LANES)), axis=-1, keepdims=True)
    ep2 = jnp.where(lane == i1, -1.0, ep)
    p2 = jnp.max(ep2, axis=-1, keepdims=True)
    i2 = jnp.min(jnp.where(ep2 == p2, lane, float(LANES)), axis=-1, keepdims=True)
    den = p1 + p2
    comb_ref[...] = jnp.where(lane == i1, grp_top * p1 / den, 0.0) + jnp.where(lane == i2, grp_top * p2 / den, 0.0)


def _merge(x, oa, ob, oc, ga, wm, bm, wo, wout, gf, wgrp, bgrp, wrt, brt, tm):
    N, D = x.shape
    row = lambda w: pl.BlockSpec((tm, w), lambda i: (i, 0))
    full = lambda a: pl.BlockSpec(a.shape, lambda i: (0,) * a.ndim)
    return pl.pallas_call(
        _merge_kernel,
        grid=(N // tm,),
        in_specs=[row(D), row(512), row(512), row(512), full(ga), full(wm), full(bm), full(wo), full(wout), full(gf),
                  full(wgrp), full(bgrp), full(wrt), full(brt)],
        out_specs=[row(D), row(D), row(LANES)],
        out_shape=[jax.ShapeDtypeStruct((N, D), F32), jax.ShapeDtypeStruct((N, D), BF16),
                   jax.ShapeDtypeStruct((N, LANES), F32)],
        compiler_params=_cparams(("arbitrary",)),
        name="merge_router",
    )(x, oa, ob, oc, ga, wm, bm, wo, wout, gf, wgrp, bgrp, wrt, brt)


def _moe_kernel(x_ref, h_ref, comb_ref, wg_ref, wu_ref, wd_ref, gfin_ref, o_ref, acc_sc, *, final_norm):
    e = pl.program_id(1)

    @pl.when(e == 0)
    def _():
        acc_sc[...] = jnp.zeros_like(acc_sc)

    h = h_ref[...]
    a = jnp.dot(h, wg_ref[...], preferred_element_type=F32)
    u = jnp.dot(h, wu_ref[...], preferred_element_type=F32)
    c = _lane_pick(comb_ref[...], e)
    hid = (jax.nn.silu(a) * u * c).astype(BF16)
    acc_sc[...] += jnp.dot(hid, wd_ref[...], preferred_element_type=F32)

    @pl.when(e == pl.num_programs(1) - 1)
    def _():
        y = x_ref[...] + acc_sc[...]
        o_ref[...] = _rms(y, gfin_ref[...]) if final_norm else y


def _moe(x, h2, comb, wg, wu, wd, gfin, tm, final_norm):
    N, D = x.shape
    row = lambda w: pl.BlockSpec((tm, w), lambda i, e: (i, 0))
    return pl.pallas_call(
        functools.partial(_moe_kernel, final_norm=final_norm),
        grid=(N // tm, N_EXPERTS),
        in_specs=[row(D), row(D), row(LANES),
                  pl.BlockSpec((None, D, EXPERT_DIM), lambda i, e: (e, 0, 0)),
                  pl.BlockSpec((None, D, EXPERT_DIM), lambda i, e: (e, 0, 0)),
                  pl.BlockSpec((None, EXPERT_DIM, D), lambda i, e: (e, 0, 0)),
                  pl.BlockSpec((1, D), lambda i, e: (0, 0))],
        out_specs=row(D),
        out_shape=jax.ShapeDtypeStruct((N, D), F32),
        scratch_shapes=[pltpu.VMEM((tm, D), F32)],
        compiler_params=_cparams(("arbitrary", "arbitrary")),
        name="moe_experts",
    )(x, h2, comb, wg, wu, wd, gfin)


def _pad_cols(w, n):
    return jnp.pad(w, ((0, 0), (0, n - w.shape[1])))


def _pack_w1(w_in_l):
    o = np.cumsum([0, 512, 768, 24, MLA_Q_RANK, MLA_KV_RANK, MLA_ROPE, 1536, FOX_HEADS])
    nq, nkv, ng, cq, ckv, kr, fox, ff = (w_in_l[:, o[i]:o[i + 1]] for i in range(8))
    d_in = w_in_l.shape[0]
    half = MLA_ROPE // 2
    nkv = nkv.reshape(d_in, 3, 2, NSA_G, DH)
    segs = lambda w: _pad_feat(w, HSEG).reshape(d_in, -1)
    z = lambda n: jnp.zeros((d_in, n), w_in_l.dtype)
    x1, x2 = kr[:, :half], kr[:, half:]
    parts = {
        "NQ": nq, "NCMP": nkv[:, 0].reshape(d_in, -1), "KS": segs(nkv[:, 1, 0]), "VS": nkv[:, 1, 1].reshape(d_in, -1),
        "KW": segs(nkv[:, 2, 0]), "VW": nkv[:, 2, 1].reshape(d_in, -1), "NG": _pad_cols(ng, LANES), "CQ": cq, "CKV": ckv,
        "KRA": jnp.concatenate([z(MLA_NOPE), x1, x2, z(HSEG - MLA_QK)], axis=1),
        "KRB": jnp.concatenate([z(MLA_NOPE), x2, x1, z(HSEG - MLA_QK)], axis=1),
        "FQ": fox[:, :512], "FK": segs(fox[:, 512:1024].reshape(d_in, FOX_HEADS, DH)), "FV": fox[:, 1024:],
        "FF": _pad_cols(ff, LANES),
    }
    w1 = jnp.concatenate([parts[name] for name, _ in _SEG_WIDTHS], axis=1)
    return w1.astype(BF16), w_in_l[:, o[8]:]


def _pack_wuq(w_uq_l):
    w = w_uq_l.reshape(MLA_Q_RANK, MLA_HEADS, MLA_QK)
    half = MLA_ROPE // 2
    nope, r1, r2 = w[:, :, :MLA_NOPE], w[:, :, MLA_NOPE:MLA_NOPE + half], w[:, :, MLA_NOPE + half:]
    direct = _pad_feat(jnp.concatenate([nope, r1, r2], axis=-1), HSEG).reshape(MLA_Q_RANK, -1)
    swapped = _pad_feat(jnp.concatenate([jnp.zeros_like(nope), r2, r1], axis=-1), HSEG).reshape(MLA_Q_RANK, -1)
    return jnp.concatenate([direct, swapped], axis=1).astype(BF16)


def _rotary_tables():
    half = MLA_ROPE // 2
    inv_freq = ROPE_THETA ** (-jnp.arange(half, dtype=F32) / half)
    zeros = lambda n: jnp.zeros((n,), F32)
    freq = jnp.concatenate([zeros(MLA_NOPE), inv_freq, inv_freq, zeros(HSEG - MLA_QK)])
    sign = jnp.concatenate([zeros(MLA_NOPE), -jnp.ones((half,), F32), jnp.ones((half,), F32), zeros(HSEG - MLA_QK)])
    return jnp.stack([freq, sign])


def _heads(a, B, T, H):
    return a.reshape(B, T, H, -1).transpose(0, 2, 1, 3)


def _tile_major_t(k, tk):
    B, H, T, d = k.shape
    return k.reshape(B, H, T // tk, tk, d).transpose(0, 1, 2, 4, 3)


def _value_tiles(v, tk):
    vt = _tile_major_t(v, tk)
    extra = jnp.zeros(vt.shape[:3] + (SUM_ROWS, tk), vt.dtype).at[..., 0, :].set(1.0)
    return jnp.concatenate([vt, extra], axis=3)


def _pad_feat(a, n):
    return jnp.pad(a, ((0, 0),) * (a.ndim - 1) + ((0, n - a.shape[-1]),))


def _ssum_matrix(ncp):
    n = np.arange(ncp)[None, :]
    j = np.arange(LANES)[:, None]
    d = n - 4 * j
    m = np.where((d == 0) | (d == 4), 1.0, np.where((d >= 1) & (d <= 3), 2.0, 0.0))
    return jnp.asarray(m, BF16)


def kernel(x, positions, rel_bias, norm_attn, w_in, b_merge, cmp_pe, cmp_k_w1, cmp_k_b1, cmp_k_w2, cmp_v_w1, cmp_v_b1,
           cmp_v_w2, mla_q_norm, mla_kv_norm, mla_w_uq, mla_w_uk, mla_w_uv, fox_b_f, w_o_nsa, w_o_mla, w_o_fox, w_out,
           norm_ffn, moe_w_grp, moe_b_grp, moe_w_router, moe_b_router, moe_w_gate, moe_w_up, moe_w_down, norm_final):
    B, T, D = x.shape
    N = B * T
    n_sub = T // CMP_STRIDE
    nqb = T // QB
    tm_in = min(512, T)
    fl_tq, fl_hb = min(512, T), 2
    tm_merge = 512
    tm_moe = 1024

    tbc, tbs, tbw = _bias_tables(rel_bias)
    tbc = tbc.reshape(2, CMP_TAB, NSA_G, GQ).transpose(0, 2, 1, 3)
    tbc_hi, tbc_lo = tbc[0], tbc[1]
    tbs = tbs.reshape(4, SEL_TK, NSA_G, GQ).transpose(2, 0, 1, 3)
    tbw = tbw.reshape(WIN_TAB, NSA_G, GQ).transpose(1, 0, 2)
    ssum = _ssum_matrix(n_sub)
    rot = _rotary_tables()
    tri = jnp.asarray(np.tril(np.ones((tm_in, tm_in), np.float32)), BF16)
    pos3 = positions.reshape(B, T, 1)

    for l in range(DEPTH):
        w1, w_merge = _pack_w1(w_in[l])
        fb = _pad_cols(fox_b_f[l].reshape(1, -1), LANES)
        wuk = _pad_feat(mla_w_uk[l].reshape(MLA_KV_RANK, MLA_HEADS, MLA_NOPE), HSEG).reshape(MLA_KV_RANK, -1)
        nq, ncmp, ksa, nvs, kw, nvw, ng, mq, mk, mv, fq, fk, fv, fcum = _inproj(
            x, pos3, norm_attn[l].reshape(1, D), w1, _pack_wuq(mla_w_uq[l]), wuk.astype(BF16),
            mla_w_uv[l].astype(BF16), mla_q_norm[l].reshape(1, -1), mla_kv_norm[l].reshape(1, -1), fb, rot, tri, tm_in)

        sub = ncmp.reshape(B, T, 2, NSA_G, DH).transpose(0, 3, 2, 1, 4).reshape(B, NSA_G, 2, n_sub, CMP_STRIDE * DH)
        pe = jnp.broadcast_to(cmp_pe[l].reshape(1, 2, CMP_STRIDE * DH), (2, 2, CMP_STRIDE * DH))
        cw1 = jnp.stack([cmp_k_w1[l], cmp_v_w1[l]]).reshape(2, 2, CMP_STRIDE * DH, CMP_HIDDEN).astype(BF16)
        cb1 = jnp.stack([cmp_k_b1[l], cmp_v_b1[l]]).reshape(2, 1, CMP_HIDDEN)
        cw2 = jnp.stack([cmp_k_w2[l], cmp_v_w2[l]]).astype(BF16)
        cmp = _compress(sub, pe, cw1, cb1, cw2)
        qt_a = nq.reshape(B, nqb, QB, NSA_G, NSA_HPG, DH).transpose(0, 3, 1, 5, 4, 2).reshape(B, NSA_G, nqb, DH, GQ)
        gates_t = ng[..., :3 * NSA_HEADS].reshape(B, nqb, QB, NSA_G, NSA_HPG, 3).transpose(0, 3, 1, 5, 4, 2)
        gates_t = jnp.pad(gates_t.reshape(B, NSA_G, nqb, 3, GQ), ((0, 0), (0, 0), (0, 0), (0, 5), (0, 0)))
        o_a = _nsa(qt_a, cmp[:, :, 0], cmp[:, :, 1].transpose(0, 1, 3, 2), ksa,
                   _value_tiles(_heads(nvs, B, T, NSA_G), SEL_TK), kw, _tile_major_t(_heads(nvw, B, T, NSA_G), QB),
                   gates_t, tbc_hi, tbc_lo, tbs, tbw, ssum)
        out_a = o_a.reshape(B, NSA_G, nqb, DH, NSA_HPG, QB).transpose(0, 2, 5, 1, 4, 3).reshape(N, NSA_HEADS * DH)

        o_b = _flash(mq.reshape(B, T, MLA_HEADS, HSEG).transpose(0, 2, 3, 1), mk,
                     _value_tiles(_heads(mv, B, T, MLA_HEADS), fl_tq), tq=fl_tq, hb=fl_hb)
        out_b = o_b.transpose(0, 3, 1, 2).reshape(N, -1)

        key_bias = (-fcum[..., :FOX_HEADS]).transpose(0, 2, 1).reshape(B, FOX_HEADS, T, 1)
        o_c = _flash(fq.reshape(B, T, FOX_HEADS, DH).transpose(0, 2, 3, 1), fk,
                     _value_tiles(_heads(fv, B, T, FOX_HEADS), fl_tq), key_bias, tq=fl_tq, hb=fl_hb)
        out_c = o_c.transpose(0, 3, 1, 2).reshape(N, -1)

        wo = jnp.stack([w_o_nsa[l], w_o_mla[l], w_o_fox[l]]).astype(BF16)
        pad_b = lambda b, n: jnp.concatenate([b, jnp.full((LANES - n,), NEG, F32)]).reshape(1, LANES)
        x2, h2, comb = _merge(
            x.reshape(N, D), out_a, out_b, out_c, norm_attn[l].reshape(1, D), w_merge.astype(BF16),
            b_merge[l].reshape(1, -1), wo, w_out[l].astype(BF16), norm_ffn[l].reshape(1, D),
            _pad_cols(moe_w_grp[l], LANES), pad_b(moe_b_grp[l], N_GROUPS),
            _pad_cols(moe_w_router[l], LANES), pad_b(moe_b_router[l], N_EXPERTS), tm_merge)

        x = _moe(x2, h2, comb, moe_w_gate[l].astype(BF16), moe_w_up[l].astype(BF16), moe_w_down[l].astype(BF16),
                 norm_final.reshape(1, D), tm_moe, final_norm=(l == DEPTH - 1)).reshape(B, T, D)
    return x
```

```python
import functools
import math

import numpy as np
import jax
import jax.numpy as jnp
from jax import lax
from jax.experimental import pallas as pl
from jax.experimental.pallas import tpu as pltpu

F32 = jnp.float32
BF16 = jnp.bfloat16

D_MODEL = 1024
DEPTH = 2
EPS = 1e-6
NEG = -1e30
REMOVED = -3e38

NSA_HEADS = 8
NSA_G = 2
NSA_HPG = 4
DH = 64
CMP_STRIDE = 16
CMP_HIDDEN = 128
SLC_BLOCK = 64
SLC_TOPK = 16
SLC_LOCAL = 2
WINDOW = 512
QB = 128
GQ = NSA_HPG * QB
N_WIN = WINDOW + QB
WIN_TAB = WINDOW + N_WIN
SEL_TK = 256
CHAIN_W = 256
SUM_ROWS = 16
CMP_BAND = 32
CMP_TAB = 48
MLA_HEADS = 8
MLA_Q_RANK = 256
MLA_KV_RANK = 128
MLA_NOPE = 64
MLA_ROPE = 32
MLA_QK = 96
ROPE_THETA = 10000.0
FOX_HEADS = 8
REL_BUCKETS = 32
REL_MAX_DIST = 128
N_GROUPS = 4
EPG = 8
N_EXPERTS = 32
EXPERT_DIM = 256

LANES = 128
LOG2E = 1.4426950408889634
VMEM_LIMIT = 56 * 1024 * 1024


def _bucket_thresholds():
    n = np.arange(0, 4 * REL_MAX_DIST)
    max_exact = REL_BUCKETS // 2
    lr = np.log(np.maximum(n, 1).astype(np.float32) / np.float32(max_exact)) / np.float32(math.log(REL_MAX_DIST / max_exact))
    large = max_exact + (lr.astype(np.float32) * np.float32(REL_BUCKETS - max_exact)).astype(np.int32)
    bucket = np.where(n < max_exact, n, np.minimum(large, REL_BUCKETS - 1))
    return [int(np.argmax(bucket >= b)) for b in range(1, REL_BUCKETS)]


BUCKET_THR = _bucket_thresholds()
assert BUCKET_THR[-1] <= REL_MAX_DIST


def _cparams(sem):
    return pltpu.CompilerParams(dimension_semantics=sem, vmem_limit_bytes=VMEM_LIMIT)


def _split3(x):
    hi = x.astype(BF16)
    r = x - hi.astype(F32)
    mid = r.astype(BF16)
    lo = (r - mid.astype(F32)).astype(BF16)
    return hi, mid, lo


def _lane_pick(x, idx):
    lane = lax.broadcasted_iota(jnp.int32, x.shape, x.ndim - 1)
    return jnp.sum(jnp.where(lane == idx, x, 0.0), axis=-1, keepdims=True)


def _online_softmax_steps(logits, vts, carries):
    stats = []
    for s, (m, _) in zip(logits, carries):
        m_new = jnp.maximum(m, jnp.max(s, axis=0, keepdims=True))
        stats.append((m_new, jnp.exp2(m - m_new), jnp.exp2(s - m_new).astype(BF16)))
    out = []
    for (m_new, alpha, p), vt, (_, acc) in zip(stats, vts, carries):
        out.append((m_new, alpha * acc + jnp.dot(vt, p, preferred_element_type=F32)))
    return tuple(out)


def _softmax_carry(dv, width):
    return jnp.full((1, width), NEG, F32), jnp.zeros((dv + SUM_ROWS, width), F32)


def _softmax_result(carry, dv):
    _, acc = carry
    return acc[0:dv] * (1.0 / acc[dv:dv + 1])


def _pipelined_softmax_loop(n_steps, qk_fn, vt_fn, carries, logit_sc, prob_sc):
    n_chains = len(carries)

    def value_products(kt, slot):
        return [jnp.dot(vt, prob_sc[slot, i], preferred_element_type=F32) for i, vt in enumerate(vt_fn(kt))]

    def keep(logits, slot):
        for i, s in enumerate(logits):
            logit_sc[slot, i] = s
        return tuple(jnp.max(s, axis=0, keepdims=True) for s in logits)

    def stage(kt, cur, nxt, state):
        carries, tile_max = state
        upcoming = qk_fn(kt + 1)
        pvs = value_products(jnp.maximum(kt - 1, 0), nxt)
        out = []
        for i, (pv, tm, (m, acc)) in enumerate(zip(pvs, tile_max, carries)):
            m_new = jnp.maximum(m, tm)
            prob_sc[cur, i] = jnp.exp2(logit_sc[cur, i] - m_new).astype(BF16)
            out.append((m_new, jnp.exp2(m - m_new) * (acc + pv)))
        return tuple(out), keep(upcoming, nxt)

    prob_sc[1] = jnp.zeros(prob_sc.shape[1:], BF16)
    state = (tuple(carries), keep(qk_fn(0), 0))
    state = lax.fori_loop(0, n_steps // 2, lambda i, c: stage(2 * i + 1, 1, 0, stage(2 * i, 0, 1, c)), state)
    odd = n_steps % 2
    carries, _ = lax.cond(odd == 1, lambda c: stage(n_steps - 1, 0, 1, c), lambda c: c, state)
    pvs = value_products(jnp.maximum(n_steps - 1, 0), 1 - odd)
    logits = tuple(logit_sc[odd, i] for i in range(n_chains))
    return logits, tuple((m, acc + pv) for (m, acc), pv in zip(carries, pvs))


def _bias_table_kernel(tab_ref, tbc_ref, tbs_ref, tbw_ref):
    h = pl.program_id(0)
    far = tab_ref[REL_BUCKETS - 1, h]

    def delta(dist):
        val = jnp.full(dist.shape, tab_ref[0, h], F32)
        for b in range(1, REL_BUCKETS):
            val = jnp.where(dist >= BUCKET_THR[b - 1], tab_ref[b, h], val)
        return (val - far) * LOG2E

    j = lax.broadcasted_iota(jnp.int32, (CMP_TAB, QB), 0)
    q = lax.broadcasted_iota(jnp.int32, (CMP_TAB, QB), 1)
    dist = q - CMP_STRIDE * j + (2 * QB - 2 * CMP_STRIDE + 1)
    band = jnp.where(j < CMP_BAND, jnp.where(dist >= 0, delta(dist), NEG), jnp.where(j == CMP_BAND, NEG, 0.0))
    band_hi = band.astype(BF16)
    tbc_ref[0] = band_hi
    tbc_ref[1] = (band - band_hi.astype(F32)).astype(BF16)
    c = lax.broadcasted_iota(jnp.int32, (SEL_TK, QB), 0)
    q = lax.broadcasted_iota(jnp.int32, (SEL_TK, QB), 1)
    for var, off in enumerate((SEL_TK, 0, QB)):
        dist = q + off - c
        tbs_ref[var] = jnp.where(dist >= 0, delta(dist), NEG)
    tbs_ref[3] = jnp.zeros((SEL_TK, QB), F32)
    c = lax.broadcasted_iota(jnp.int32, (WIN_TAB, QB), 0)
    q = lax.broadcasted_iota(jnp.int32, (WIN_TAB, QB), 1)
    dist = q + WINDOW - c
    tbw_ref[...] = jnp.where((dist >= 0) & (dist < WINDOW), delta(dist), NEG)


def _bias_tables(rel_bias):
    hq = NSA_HEADS * QB
    return pl.pallas_call(
        _bias_table_kernel,
        grid=(NSA_HEADS,),
        in_specs=[pl.BlockSpec(memory_space=pltpu.SMEM)],
        out_specs=[
            pl.BlockSpec((2, CMP_TAB, QB), lambda h: (0, 0, h)),
            pl.BlockSpec((4, SEL_TK, QB), lambda h: (0, 0, h)),
            pl.BlockSpec((WIN_TAB, QB), lambda h: (0, h)),
        ],
        out_shape=[
            jax.ShapeDtypeStruct((2, CMP_TAB, hq), BF16),
            jax.ShapeDtypeStruct((4, SEL_TK, hq), F32),
            jax.ShapeDtypeStruct((WIN_TAB, hq), F32),
        ],
        compiler_params=_cparams(("arbitrary",)),
        name="bias_tables",
    )(rel_bias)


HSEG = 128
_SEG_WIDTHS = (("NQ", 512), ("NCMP", 256), ("KS", NSA_G * HSEG), ("VS", 128), ("KW", NSA_G * HSEG), ("VW", 128),
               ("NG", 128), ("CQ", MLA_Q_RANK), ("CKV", MLA_KV_RANK), ("KRA", HSEG), ("KRB", HSEG), ("FQ", 512),
               ("FK", FOX_HEADS * HSEG), ("FV", 512), ("FF", 128))
SEG = {}
_off = 0
for _name, _w in _SEG_WIDTHS:
    SEG[_name] = (_off, _off + _w)
    _off += _w


def _rms(x, g):
    return x * lax.rsqrt(jnp.mean(x * x, axis=-1, keepdims=True) + EPS) * g


def _inproj_kernel(x_ref, pos_ref, g_ref, w1_ref, wuq_ref, wuk_ref, wuv_ref, qn_ref, kvn_ref, fb_ref, rot_ref,
                   tri_ref, place_ref, nq_ref, ncmp_ref, ksa_ref, vs_ref, kw_ref, vw_ref, ng_ref, mq_ref, mk_ref, mv_ref,
                   fq_ref, fk_ref, fv_ref, carry_ref):
    i = pl.program_id(1)
    tm = x_ref.shape[0]
    h = _rms(x_ref[...], g_ref[...]).astype(BF16)

    def seg(name):
        a, b = SEG[name]
        return jnp.dot(h, w1_ref[:, a:b], preferred_element_type=F32)

    nq = seg("NQ") * (DH ** -0.5 * LOG2E)
    for pair in range(NSA_HEADS // 2):
        pair_t = nq[:, pair * HSEG:(pair + 1) * HSEG].T.astype(BF16)
        for sub in range(2):
            g, hh = divmod(2 * pair + sub, NSA_HPG)
            for jb in range(tm // QB):
                nq_ref[g, jb, :, hh * QB:(hh + 1) * QB] = pair_t[sub * DH:(sub + 1) * DH, jb * QB:(jb + 1) * QB]
    ncmp_ref[...] = seg("NCMP").astype(BF16)
    vs_ref[...] = seg("VS").astype(BF16)
    kw_ref[...] = seg("KW").astype(BF16)
    vw_ref[...] = seg("VW").astype(BF16)
    ng_ref[...] = jax.nn.sigmoid(seg("NG"))
    ks = seg("KS").astype(BF16)
    tok = i * tm + lax.broadcasted_iota(jnp.int32, (tm, LANES), 0)
    blk = (tok // SLC_BLOCK == lax.broadcasted_iota(jnp.int32, (tm, LANES), 1))
    blk = jnp.where(blk, 1.0, 0.0).astype(BF16)
    for g in range(NSA_G):
        ksa_ref[:, 2 * g * HSEG:(2 * g + 1) * HSEG] = ks[:, g * HSEG:(g + 1) * HSEG]
        ksa_ref[:, (2 * g + 1) * HSEG:(2 * g + 2) * HSEG] = blk

    ang = pos_ref[...].astype(F32) * rot_ref[0:1, :]
    cos = jnp.cos(ang)
    sin = jnp.sin(ang) * rot_ref[1:2, :]
    cos_h = jnp.concatenate([cos] * MLA_HEADS, axis=1)
    sin_h = jnp.concatenate([sin] * MLA_HEADS, axis=1)
    width = MLA_HEADS * HSEG
    c_q = _rms(seg("CQ"), qn_ref[...]).astype(BF16)
    q_a = jnp.dot(c_q, wuq_ref[:, 0:width], preferred_element_type=F32)
    q_b = jnp.dot(c_q, wuq_ref[:, width:2 * width], preferred_element_type=F32)
    mq = (q_a * cos_h + q_b * sin_h) * (MLA_QK ** -0.5 * LOG2E)
    for hd in range(MLA_HEADS):
        mq_ref[hd] = mq[:, hd * HSEG:(hd + 1) * HSEG].T.astype(BF16)
    c_kv = _rms(seg("CKV"), kvn_ref[...]).astype(BF16)
    k_rope = seg("KRA") * cos + seg("KRB") * sin
    k_nope = jnp.dot(c_kv, wuk_ref[...], preferred_element_type=F32)
    mk_ref[...] = (k_nope + jnp.concatenate([k_rope] * MLA_HEADS, axis=1)).astype(BF16)
    mv_ref[...] = jnp.dot(c_kv, wuv_ref[...], preferred_element_type=F32).astype(BF16)

    fq = seg("FQ") * (DH ** -0.5 * LOG2E)
    ones_rows = jnp.where(lax.broadcasted_iota(jnp.int32, (HSEG - DH, tm), 0) < 3, 1.0, 0.0).astype(BF16)
    for pair in range(FOX_HEADS // 2):
        pair_t = fq[:, pair * HSEG:(pair + 1) * HSEG].T.astype(BF16)
        for sub in range(2):
            fq_ref[2 * pair + sub, 0:DH, :] = pair_t[sub * DH:(sub + 1) * DH]
            fq_ref[2 * pair + sub, DH:HSEG, :] = ones_rows
    fv_ref[...] = seg("FV").astype(BF16)
    log_f = jax.nn.log_sigmoid(seg("FF") + fb_ref[...])

    @pl.when(i == 0)
    def _():
        carry_ref[...] = jnp.zeros_like(carry_ref)

    tri = tri_ref[...]
    cum = carry_ref[...]
    for part in _split3(log_f):
        cum = cum + jnp.dot(tri, part, preferred_element_type=F32)
    carry_ref[...] = cum[tm - 1:tm, :]
    fk = seg("FK")
    for n, part in enumerate(_split3(cum * LOG2E)):
        fk = fk + jnp.dot(part, place_ref[n], preferred_element_type=F32)
    fk_ref[...] = fk.astype(BF16)


def _inproj(x, pos, g, w1, wuq, wuk, wuv, qn, kvn, fb, rot, tri, place, tm):
    B, T, D = x.shape
    row = lambda w: pl.BlockSpec((None, tm, w), lambda b, i: (b, i, 0))
    full = lambda a: pl.BlockSpec(a.shape, lambda b, i: (0,) * a.ndim)
    heads_t = lambda h: pl.BlockSpec((None, h, HSEG, tm), lambda b, i: (b, 0, 0, i))
    tok = lambda w, dt=BF16: (row(w), jax.ShapeDtypeStruct((B, T, w), dt))
    q_t = lambda h: (heads_t(h), jax.ShapeDtypeStruct((B, h, HSEG, T), BF16))
    outs = [
        (pl.BlockSpec((None, NSA_G, tm // QB, DH, GQ), lambda b, i: (b, 0, i, 0, 0)),
         jax.ShapeDtypeStruct((B, NSA_G, T // QB, DH, GQ), BF16)),
        tok(256), tok(2 * NSA_G * HSEG), tok(128), tok(NSA_G * HSEG), tok(128), tok(128, F32),
        q_t(MLA_HEADS), tok(MLA_HEADS * HSEG), tok(512),
        q_t(FOX_HEADS), tok(FOX_HEADS * HSEG), tok(512),
    ]
    return pl.pallas_call(
        _inproj_kernel,
        grid=(B, T // tm),
        in_specs=[row(D), row(1), full(g), full(w1), full(wuq), full(wuk), full(wuv), full(qn), full(kvn), full(fb),
                  full(rot), full(tri), full(place)],
        out_specs=[spec for spec, _ in outs],
        out_shape=[shape for _, shape in outs],
        scratch_shapes=[pltpu.VMEM((1, LANES), F32)],
        compiler_params=_cparams(("arbitrary", "arbitrary")),
        name="inproj",
    )(x, pos, g, w1, wuq, wuk, wuv, qn, kvn, fb, rot, tri, place)


def _flash_kernel(qt_ref, k_ref, vt_ref, o_ref, logit_sc, prob_sc, *, tq, hb):
    qi = pl.program_id(2)
    dv = vt_ref.shape[2] - SUM_ROWS
    chains = [(j, c) for j in range(hb) for c in range(tq // CHAIN_W)]
    qts = [qt_ref[j, :, c * CHAIN_W:(c + 1) * CHAIN_W] for j, c in chains]

    def qk(kt):
        st = pl.multiple_of(kt * tq, tq)
        return [jnp.dot(k_ref[pl.ds(st, tq), j * HSEG:(j + 1) * HSEG], qt, preferred_element_type=F32)
                for (j, _), qt in zip(chains, qts)]

    init = tuple(_softmax_carry(dv, CHAIN_W) for _ in chains)
    diag, carries = _pipelined_softmax_loop(qi, qk, lambda kt: [vt_ref[j, kt] for j, _ in chains], init,
                                            logit_sc, prob_sc)
    kpos = lax.broadcasted_iota(jnp.int32, (tq, CHAIN_W), 0)
    qpos = lax.broadcasted_iota(jnp.int32, (tq, CHAIN_W), 1)
    logits = [jnp.where(kpos <= qpos + c * CHAIN_W, s, NEG) for (_, c), s in zip(chains, diag)]
    final = _online_softmax_steps(logits, [vt_ref[j, qi] for j, _ in chains], carries)
    for (j, c), carry in zip(chains, final):
        o_ref[j, :, c * CHAIN_W:(c + 1) * CHAIN_W] = _softmax_result(carry, dv).astype(o_ref.dtype)


def _flash(qt, k, vt, *, tq, hb):
    B, H, dk, T = qt.shape
    dv = vt.shape[3] - SUM_ROWS
    return pl.pallas_call(
        functools.partial(_flash_kernel, tq=tq, hb=hb),
        grid=(B, H // hb, T // tq),
        in_specs=[
            pl.BlockSpec((None, hb, dk, tq), lambda b, h, i: (b, h, 0, i)),
            pl.BlockSpec((None, T, hb * HSEG), lambda b, h, i: (b, 0, h)),
            pl.BlockSpec((None, hb, T // tq, dv + SUM_ROWS, tq), lambda b, h, i: (b, h, 0, 0, 0)),
        ],
        out_specs=pl.BlockSpec((None, hb, dv, tq), lambda b, h, i: (b, h, 0, i)),
        out_shape=jax.ShapeDtypeStruct((B, H, dv, T), BF16),
        scratch_shapes=[pltpu.VMEM((2, hb * tq // CHAIN_W, tq, CHAIN_W), F32),
                        pltpu.VMEM((2, hb * tq // CHAIN_W, tq, CHAIN_W), BF16)],
        compiler_params=_cparams(("arbitrary", "arbitrary", "arbitrary")),
        name="flash_attention",
    )(qt, k, vt)


def _compress_kernel(sub_ref, pe_ref, w1_ref, b1_ref, w2_ref, o_ref):
    sub = sub_ref[...].astype(F32)
    n_sub = sub.shape[0]
    a = jnp.dot((sub + pe_ref[0:1, :]).astype(BF16), w1_ref[0], preferred_element_type=F32)
    b = jnp.dot((sub + pe_ref[1:2, :]).astype(BF16), w1_ref[1], preferred_element_type=F32)
    hid = jax.nn.gelu(a + pltpu.roll(b, n_sub - 1, 0) + b1_ref[...])
    o_ref[...] = jnp.dot(hid.astype(BF16), w2_ref[...], preferred_element_type=F32).astype(o_ref.dtype)


def _compress(sub, pe, w1, b1, w2):
    B, G, _, n_sub, _ = sub.shape
    return pl.pallas_call(
        _compress_kernel,
        grid=(B, G, 2),
        in_specs=[
            pl.BlockSpec((None, None, None, n_sub, 1024), lambda b, g, s: (b, g, s, 0, 0)),
            pl.BlockSpec((None, 2, 1024), lambda b, g, s: (s, 0, 0)),
            pl.BlockSpec((None, 2, 1024, CMP_HIDDEN), lambda b, g, s: (s, 0, 0, 0)),
            pl.BlockSpec((None, 1, CMP_HIDDEN), lambda b, g, s: (s, 0, 0)),
            pl.BlockSpec((None, CMP_HIDDEN, DH), lambda b, g, s: (s, 0, 0)),
        ],
        out_specs=pl.BlockSpec((None, None, None, n_sub, DH), lambda b, g, s: (b, g, s, 0, 0)),
        out_shape=jax.ShapeDtypeStruct((B, G, 2, n_sub, DH), BF16),
        compiler_params=_cparams(("arbitrary", "arbitrary", "arbitrary")),
        name="nsa_compress",
    )(sub, pe, w1, b1, w2)


def _nsa_kernel(qt_ref, kc_ref, vct_ref, ksa_ref, vst_ref, kw_ref, vwt_ref, gate_ref, tbc_hi_ref, tbc_lo_ref,
                tbs_ref, tbw_ref, ssum_ref, o_ref, logit_sc, prob_sc):
    qb = pl.program_id(2)
    t0 = qb * QB
    ncp = kc_ref.shape[0]
    qt = qt_ref[...]

    s = jnp.dot(kc_ref[...], qt, preferred_element_type=F32)
    n0 = (QB // CMP_STRIDE) * qb - 16
    rel = lax.broadcasted_iota(jnp.int32, (ncp, CMP_TAB), 0) - n0
    col = lax.broadcasted_iota(jnp.int32, (ncp, CMP_TAB), 1)
    place = jnp.where(jnp.minimum(rel, CMP_BAND) == col, 1.0, 0.0).astype(BF16)
    s = s + (jnp.dot(place, tbc_hi_ref[...], preferred_element_type=F32)
             + jnp.dot(place, tbc_lo_ref[...], preferred_element_type=F32))
    m = jnp.max(s, axis=0, keepdims=True)
    e = jnp.exp2(s - m)
    p_c = e * jnp.where(m > 0.5 * NEG, 1.0 / jnp.sum(e, axis=0, keepdims=True), 0.0)
    o_c = jnp.dot(vct_ref[...], p_c.astype(BF16), preferred_element_type=F32)

    imp = p_c[:, 0:QB]
    for hh in range(1, NSA_HPG):
        imp = imp + p_c[:, hh * QB:(hh + 1) * QB]
    ssum = ssum_ref[...]
    p_slc = sum(jnp.dot(ssum, part, preferred_element_type=F32) for part in _split3(imp))
    jj = lax.broadcasted_iota(jnp.int32, (LANES, QB), 0)
    blk_cur = (t0 + lax.broadcasted_iota(jnp.int32, (LANES, QB), 1)) // SLC_BLOCK
    causal_b = jj <= blk_cur
    forced = (jj == 0) | (causal_b & (jj > blk_cur - SLC_LOCAL))
    score = jnp.where(causal_b & jnp.logical_not(forced), p_slc, NEG)
    jf = jj.astype(F32)
    sel = jnp.where(forced, 1.0, 0.0)
    for _ in range(SLC_TOPK - 1 - SLC_LOCAL):
        mx = jnp.max(score, axis=0, keepdims=True)
        first = jnp.min(jnp.where(score == mx, jf, float(LANES)), axis=0, keepdims=True)
        hit = jf == first
        sel = jnp.where(hit, 1.0, sel)
        score = jnp.where(hit, REMOVED, score)
    unsel = jnp.where(causal_b & (sel > 0.0), 0.0, NEG).astype(BF16)

    first = jnp.maximum(qb - WINDOW // QB, 0)
    start = pl.multiple_of(first * QB, QB)
    shift = pl.multiple_of(start + WINDOW - t0, QB)
    s = (jnp.dot(kw_ref[pl.ds(start, N_WIN), 0:DH], qt, preferred_element_type=F32)
         + tbw_ref[pl.ds(shift, N_WIN), :])
    e = jnp.exp2(s - jnp.max(s, axis=0, keepdims=True))
    inv = 1.0 / jnp.sum(e, axis=0, keepdims=True)
    e = e.astype(BF16)
    o_w = jnp.zeros((DH, GQ), F32)
    for r in range(N_WIN // QB):
        o_w = o_w + jnp.dot(vwt_ref[first + r], e[r * QB:(r + 1) * QB], preferred_element_type=F32)
    o_w = o_w * inv

    qa = jnp.concatenate([qt, jnp.zeros((HSEG - DH, GQ), BF16), jnp.concatenate([unsel] * NSA_HPG, axis=1)],
                         axis=0)
    n_tiles = qb // 2 + 1
    even = (qb % 2) == 0

    chunks = [(c * CHAIN_W, (c + 1) * CHAIN_W) for c in range(GQ // CHAIN_W)]
    qas = [qa[:, a:b] for a, b in chunks]

    def sel_qk(kt):
        start = pl.multiple_of(kt * SEL_TK, SEL_TK)
        return [jnp.dot(ksa_ref[pl.ds(start, SEL_TK), :], qc, preferred_element_type=F32)
                for qc in qas]

    def sel_vt(kt):
        return [vst_ref[kt]] * len(chunks)

    def near_step(kt, logits, var, carries):
        logits = [s + tbs_ref[var, :, a:b] for s, (a, b) in zip(logits, chunks)]
        return _online_softmax_steps(logits, sel_vt(kt), carries)

    carries = tuple(_softmax_carry(DH, CHAIN_W) for _ in chunks)
    n_far = jnp.maximum(n_tiles - 2, 0)
    logits, carries = _pipelined_softmax_loop(n_far, sel_qk, sel_vt, carries, logit_sc, prob_sc)
    logits, carries = lax.cond(
        n_tiles >= 2,
        lambda lg, c: (tuple(sel_qk(n_tiles - 1)), near_step(n_tiles - 2, lg, jnp.where(even, 0, 3), c)),
        lambda lg, c: (lg, c), logits, carries)
    carries = near_step(n_tiles - 1, logits, jnp.where(even, 1, 2), carries)
    o_s = jnp.concatenate([_softmax_result(c, DH) for c in carries], axis=1)

    gates = gate_ref[...]
    o_ref[...] = (gates[0:1] * o_c + gates[1:2] * o_s + gates[2:3] * o_w).astype(o_ref.dtype)


def _nsa(qt, kc, vct, ksa, vst, kw, vwt, gates, tbc_hi, tbc_lo, tbs, tbw, ssum):
    B, G, nqb, _, _ = qt.shape
    T = ksa.shape[1]
    bg = lambda a: pl.BlockSpec((None, None) + a.shape[2:], lambda b, g, i: (b, g) + (0,) * (a.ndim - 2))
    lane_g = lambda w: pl.BlockSpec((None, T, w), lambda b, g, i: (b, 0, g))
    per_g = lambda a: pl.BlockSpec((None,) + a.shape[1:], lambda b, g, i: (g,) + (0,) * (a.ndim - 1))
    per_blk = lambda a: pl.BlockSpec((None, None, None) + a.shape[3:], lambda b, g, i: (b, g, i) + (0,) * (a.ndim - 3))
    return pl.pallas_call(
        _nsa_kernel,
        grid=(B, G, nqb),
        in_specs=[per_blk(qt), bg(kc), bg(vct), lane_g(2 * HSEG), bg(vst), lane_g(HSEG), bg(vwt), per_blk(gates),
                  per_g(tbc_hi), per_g(tbc_lo), per_g(tbs), per_g(tbw),
                  pl.BlockSpec(ssum.shape, lambda b, g, i: (0, 0))],
        out_specs=pl.BlockSpec((None, None, None, DH, GQ), lambda b, g, i: (b, g, i, 0, 0)),
        out_shape=jax.ShapeDtypeStruct((B, G, nqb, DH, GQ), BF16),
        scratch_shapes=[pltpu.VMEM((2, GQ // CHAIN_W, SEL_TK, CHAIN_W), F32),
                        pltpu.VMEM((2, GQ // CHAIN_W, SEL_TK, CHAIN_W), BF16)],
        compiler_params=_cparams(("arbitrary", "arbitrary", "arbitrary")),
        name="nsa_attention",
    )(qt, kc, vct, ksa, vst, kw, vwt, gates, tbc_hi, tbc_lo, tbs, tbw, ssum)


def _merge_kernel(x_ref, oa_ref, ob_ref, oc_ref, ga_ref, wm_ref, bm_ref, wo_ref, wout_ref, gf_ref, wgrp_ref, bgrp_ref,
                  wrt_ref, brt_ref, xo_ref, h2_ref, comb_ref):
    x = x_ref[...]
    h = _rms(x, ga_ref[...]).astype(BF16)
    mixed = None
    for i, o_ref in enumerate((oa_ref, ob_ref, oc_ref)):
        gate = jax.nn.sigmoid(jnp.dot(h, wm_ref[:, i * D_MODEL:(i + 1) * D_MODEL], preferred_element_type=F32)
                              + bm_ref[:, i * D_MODEL:(i + 1) * D_MODEL])
        term = gate * jnp.dot(o_ref[...], wo_ref[i], preferred_element_type=F32)
        mixed = term if mixed is None else mixed + term
    xn = x + jnp.dot(mixed.astype(BF16), wout_ref[...], preferred_element_type=F32)
    xo_ref[...] = xn
    h2 = _rms(xn, gf_ref[...])
    h2_ref[...] = h2.astype(BF16)

    lane = lax.broadcasted_iota(jnp.int32, (x.shape[0], LANES), 1).astype(F32)
    gl = jnp.dot(h2, wgrp_ref[...], preferred_element_type=F32, precision=lax.Precision.HIGHEST) + bgrp_ref[...]
    ge = jnp.exp(gl - jnp.max(gl, axis=-1, keepdims=True))
    gp = ge / jnp.sum(ge, axis=-1, keepdims=True)
    grp_top = jnp.max(gp, axis=-1, keepdims=True)
    g_idx = jnp.min(jnp.where(gp == grp_top, lane, float(LANES)), axis=-1, keepdims=True)
    el = jnp.dot(h2, wrt_ref[...], preferred_element_type=F32, precision=lax.Precision.HIGHEST) + brt_ref[...]
    in_grp = jnp.floor(lane * (1.0 / EPG)) == g_idx
    el = jnp.where(in_grp, el, NEG)
    ee = jnp.exp(el - jnp.max(el, axis=-1, keepdims=True))
    ep = jnp.where(in_grp, ee / jnp.sum(ee, axis=-1, keepdims=True), -1.0)
    p1 = jnp.max(ep, axis=-1, keepdims=True)
    i1 = jnp.min(jnp.where(ep == p1, lane, float(LANES)), axis=-1, keepdims=True)
    ep2 = jnp.where(lane == i1, -1.0, ep)
    p2 = jnp.max(ep2, axis=-1, keepdims=True)
    i2 = jnp.min(jnp.where(ep2 == p2, lane, float(LANES)), axis=-1, keepdims=True)
    den = p1 + p2
    comb_ref[...] = jnp.where(lane == i1, grp_top * p1 / den, 0.0) + jnp.where(lane == i2, grp_top * p2 / den, 0.0)


def _merge(x, oa, ob, oc, ga, wm, bm, wo, wout, gf, wgrp, bgrp, wrt, brt, tm):
    N, D = x.shape
    row = lambda w: pl.BlockSpec((tm, w), lambda i: (i, 0))
    full = lambda a: pl.BlockSpec(a.shape, lambda i: (0,) * a.ndim)
    return pl.pallas_call(
        _merge_kernel,
        grid=(N // tm,),
        in_specs=[row(D), row(512), row(512), row(512), full(ga), full(wm), full(bm), full(wo), full(wout), full(gf),
                  full(wgrp), full(bgrp), full(wrt), full(brt)],
        out_specs=[row(D), row(D), row(LANES)],
        out_shape=[jax.ShapeDtypeStruct((N, D), F32), jax.ShapeDtypeStruct((N, D), BF16),
                   jax.ShapeDtypeStruct((N, LANES), F32)],
        compiler_params=_cparams(("arbitrary",)),
        name="merge_router",
    )(x, oa, ob, oc, ga, wm, bm, wo, wout, gf, wgrp, bgrp, wrt, brt)


def _moe_kernel(x_ref, h_ref, comb_ref, wg_ref, wu_ref, wd_ref, gfin_ref, o_ref, acc_sc, *, final_norm):
    e = pl.program_id(1)

    @pl.when(e == 0)
    def _():
        acc_sc[...] = jnp.zeros_like(acc_sc)

    h = h_ref[...]
    a = jnp.dot(h, wg_ref[...], preferred_element_type=F32)
    u = jnp.dot(h, wu_ref[...], preferred_element_type=F32)
    c = _lane_pick(comb_ref[...], e)
    hid = (jax.nn.silu(a) * u * c).astype(BF16)
    acc_sc[...] += jnp.dot(hid, wd_ref[...], preferred_element_type=F32)

    @pl.when(e == pl.num_programs(1) - 1)
    def _():
        y = x_ref[...] + acc_sc[...]
        o_ref[...] = _rms(y, gfin_ref[...]) if final_norm else y


def _moe(x, h2, comb, wg, wu, wd, gfin, tm, final_norm):
    N, D = x.shape
    row = lambda w: pl.BlockSpec((tm, w), lambda i, e: (i, 0))
    return pl.pallas_call(
        functools.partial(_moe_kernel, final_norm=final_norm),
        grid=(N // tm, N_EXPERTS),
        in_specs=[row(D), row(D), row(LANES),
                  pl.BlockSpec((None, D, EXPERT_DIM), lambda i, e: (e, 0, 0)),
                  pl.BlockSpec((None, D, EXPERT_DIM), lambda i, e: (e, 0, 0)),
                  pl.BlockSpec((None, EXPERT_DIM, D), lambda i, e: (e, 0, 0)),
                  pl.BlockSpec((1, D), lambda i, e: (0, 0))],
        out_specs=row(D),
        out_shape=jax.ShapeDtypeStruct((N, D), F32),
        scratch_shapes=[pltpu.VMEM((tm, D), F32)],
        compiler_params=_cparams(("arbitrary", "arbitrary")),
        name="moe_experts",
    )(x, h2, comb, wg, wu, wd, gfin)


def _pad_cols(w, n):
    return jnp.pad(w, ((0, 0), (0, n - w.shape[1])))


def _pack_w1(w_in_l):
    o = np.cumsum([0, 512, 768, 24, MLA_Q_RANK, MLA_KV_RANK, MLA_ROPE, 1536, FOX_HEADS])
    nq, nkv, ng, cq, ckv, kr, fox, ff = (w_in_l[:, o[i]:o[i + 1]] for i in range(8))
    d_in = w_in_l.shape[0]
    half = MLA_ROPE // 2
    nkv = nkv.reshape(d_in, 3, 2, NSA_G, DH)
    segs = lambda w: _pad_feat(w, HSEG).reshape(d_in, -1)
    z = lambda n: jnp.zeros((d_in, n), w_in_l.dtype)
    x1, x2 = kr[:, :half], kr[:, half:]
    parts = {
        "NQ": nq, "NCMP": nkv[:, 0].reshape(d_in, -1), "KS": segs(nkv[:, 1, 0]), "VS": nkv[:, 1, 1].reshape(d_in, -1),
        "KW": segs(nkv[:, 2, 0]), "VW": nkv[:, 2, 1].reshape(d_in, -1), "NG": _pad_cols(ng, LANES), "CQ": cq, "CKV": ckv,
        "KRA": jnp.concatenate([z(MLA_NOPE), x1, x2, z(HSEG - MLA_QK)], axis=1),
        "KRB": jnp.concatenate([z(MLA_NOPE), x2, x1, z(HSEG - MLA_QK)], axis=1),
        "FQ": fox[:, :512], "FK": segs(fox[:, 512:1024].reshape(d_in, FOX_HEADS, DH)), "FV": fox[:, 1024:],
        "FF": _pad_cols(ff, LANES),
    }
    w1 = jnp.concatenate([parts[name] for name, _ in _SEG_WIDTHS], axis=1)
    return w1.astype(BF16), w_in_l[:, o[8]:]


def _pack_wuq(w_uq_l):
    w = w_uq_l.reshape(MLA_Q_RANK, MLA_HEADS, MLA_QK)
    half = MLA_ROPE // 2
    nope, r1, r2 = w[:, :, :MLA_NOPE], w[:, :, MLA_NOPE:MLA_NOPE + half], w[:, :, MLA_NOPE + half:]
    direct = _pad_feat(jnp.concatenate([nope, r1, r2], axis=-1), HSEG).reshape(MLA_Q_RANK, -1)
    swapped = _pad_feat(jnp.concatenate([jnp.zeros_like(nope), r2, r1], axis=-1), HSEG).reshape(MLA_Q_RANK, -1)
    return jnp.concatenate([direct, swapped], axis=1).astype(BF16)


def _rotary_tables():
    half = MLA_ROPE // 2
    inv_freq = ROPE_THETA ** (-jnp.arange(half, dtype=F32) / half)
    zeros = lambda n: jnp.zeros((n,), F32)
    freq = jnp.concatenate([zeros(MLA_NOPE), inv_freq, inv_freq, zeros(HSEG - MLA_QK)])
    sign = jnp.concatenate([zeros(MLA_NOPE), -jnp.ones((half,), F32), jnp.ones((half,), F32), zeros(HSEG - MLA_QK)])
    return jnp.stack([freq, sign])


def _heads(a, B, T, H):
    return a.reshape(B, T, H, -1).transpose(0, 2, 1, 3)


def _tile_major_t(k, tk):
    B, H, T, d = k.shape
    return k.reshape(B, H, T // tk, tk, d).transpose(0, 1, 2, 4, 3)


def _value_tiles(v, tk):
    vt = _tile_major_t(v, tk)
    extra = jnp.zeros(vt.shape[:3] + (SUM_ROWS, tk), vt.dtype).at[..., 0, :].set(1.0)
    return jnp.concatenate([vt, extra], axis=3)


def _pad_feat(a, n):
    return jnp.pad(a, ((0, 0),) * (a.ndim - 1) + ((0, n - a.shape[-1]),))


def _ssum_matrix(ncp):
    n = np.arange(ncp)[None, :]
    j = np.arange(LANES)[:, None]
    d = n - 4 * j
    m = np.where((d == 0) | (d == 4), 1.0, np.where((d >= 1) & (d <= 3), 2.0, 0.0))
    return jnp.asarray(m, BF16)


def kernel(x, positions, rel_bias, norm_attn, w_in, b_merge, cmp_pe, cmp_k_w1, cmp_k_b1, cmp_k_w2, cmp_v_w1, cmp_v_b1,
           cmp_v_w2, mla_q_norm, mla_kv_norm, mla_w_uq, mla_w_uk, mla_w_uv, fox_b_f, w_o_nsa, w_o_mla, w_o_fox, w_out,
           norm_ffn, moe_w_grp, moe_b_grp, moe_w_router, moe_b_router, moe_w_gate, moe_w_up, moe_w_down, norm_final):
    B, T, D = x.shape
    N = B * T
    n_sub = T // CMP_STRIDE
    nqb = T // QB
    tm_in = min(512, T)
    fl_tq, fl_hb = min(512, T), 2
    tm_merge = 512
    tm_moe = 1024

    tbc, tbs, tbw = _bias_tables(rel_bias)
    tbc = tbc.reshape(2, CMP_TAB, NSA_G, GQ).transpose(0, 2, 1, 3)
    tbc_hi, tbc_lo = tbc[0], tbc[1]
    tbs = tbs.reshape(4, SEL_TK, NSA_G, GQ).transpose(2, 0, 1, 3)
    tbw = tbw.reshape(WIN_TAB, NSA_G, GQ).transpose(1, 0, 2)
    ssum = _ssum_matrix(n_sub)
    rot = _rotary_tables()
    cum_place = np.zeros((3, LANES, FOX_HEADS * HSEG), np.float32)
    for n in range(3):
        cum_place[n, np.arange(FOX_HEADS), np.arange(FOX_HEADS) * HSEG + DH + n] = -1.0
    cum_place = jnp.asarray(cum_place, BF16)
    tri =jnp.asarray(np.tril(np.ones((tm_in, tm_in), np.float32)), BF16)
    pos3 = positions.reshape(B, T, 1)

    for l in range(DEPTH):
        w1, w_merge = _pack_w1(w_in[l])
        fb = _pad_cols(fox_b_f[l].reshape(1, -1), LANES)
        wuk = _pad_feat(mla_w_uk[l].reshape(MLA_KV_RANK, MLA_HEADS, MLA_NOPE), HSEG).reshape(MLA_KV_RANK, -1)
        qt_a, ncmp, ksa, nvs, kw, nvw, ng, mq_t, mk, mv, fq_t, fk, fv = _inproj(
            x, pos3, norm_attn[l].reshape(1, D), w1, _pack_wuq(mla_w_uq[l]), wuk.astype(BF16),
            mla_w_uv[l].astype(BF16), mla_q_norm[l].reshape(1, -1), mla_kv_norm[l].reshape(1, -1), fb, rot, tri,
            cum_place, tm_in)

        sub = ncmp.reshape(B, T, 2, NSA_G, DH).transpose(0, 3, 2, 1, 4).reshape(B, NSA_G, 2, n_sub, CMP_STRIDE * DH)
        pe = jnp.broadcast_to(cmp_pe[l].reshape(1, 2, CMP_STRIDE * DH), (2, 2, CMP_STRIDE * DH))
        cw1 = jnp.stack([cmp_k_w1[l], cmp_v_w1[l]]).reshape(2, 2, CMP_STRIDE * DH, CMP_HIDDEN).astype(BF16)
        cb1 = jnp.stack([cmp_k_b1[l], cmp_v_b1[l]]).reshape(2, 1, CMP_HIDDEN)
        cw2 = jnp.stack([cmp_k_w2[l], cmp_v_w2[l]]).astype(BF16)
        cmp = _compress(sub, pe, cw1, cb1, cw2)
        gates_t = ng[..., :3 * NSA_HEADS].reshape(B, nqb, QB, NSA_G, NSA_HPG, 3).transpose(0, 3, 1, 5, 4, 2)
        gates_t = jnp.pad(gates_t.reshape(B, NSA_G, nqb, 3, GQ), ((0, 0), (0, 0), (0, 0), (0, 5), (0, 0)))
        o_a = _nsa(qt_a, cmp[:, :, 0], cmp[:, :, 1].transpose(0, 1, 3, 2), ksa,
                   _value_tiles(_heads(nvs, B, T, NSA_G), SEL_TK), kw, _tile_major_t(_heads(nvw, B, T, NSA_G), QB),
                   gates_t, tbc_hi, tbc_lo, tbs, tbw, ssum)
        out_a = o_a.reshape(B, NSA_G, nqb, DH, NSA_HPG, QB).transpose(0, 2, 5, 1, 4, 3).reshape(N, NSA_HEADS * DH)

        o_b = _flash(mq_t, mk, _value_tiles(_heads(mv, B, T, MLA_HEADS), fl_tq), tq=fl_tq, hb=fl_hb)
        out_b = o_b.transpose(0, 3, 1, 2).reshape(N, -1)

        o_c = _flash(fq_t, fk, _value_tiles(_heads(fv, B, T, FOX_HEADS), fl_tq), tq=fl_tq, hb=fl_hb)
        out_c = o_c.transpose(0, 3, 1, 2).reshape(N, -1)

        wo = jnp.stack([w_o_nsa[l], w_o_mla[l], w_o_fox[l]]).astype(BF16)
        pad_b = lambda b, n: jnp.concatenate([b, jnp.full((LANES - n,), NEG, F32)]).reshape(1, LANES)
        x2, h2, comb = _merge(
            x.reshape(N, D), out_a, out_b, out_c, norm_attn[l].reshape(1, D), w_merge.astype(BF16),
            b_merge[l].reshape(1, -1), wo, w_out[l].astype(BF16), norm_ffn[l].reshape(1, D),
            _pad_cols(moe_w_grp[l], LANES), pad_b(moe_b_grp[l], N_GROUPS),
            _pad_cols(moe_w_router[l], LANES), pad_b(moe_b_router[l], N_EXPERTS), tm_merge)

        x = _moe(x2, h2, comb, moe_w_gate[l].astype(BF16), moe_w_up[l].astype(BF16), moe_w_down[l].astype(BF16),
                 norm_final.reshape(1, D), tm_moe, final_norm=(l == DEPTH - 1)).reshape(B, T, D)
    return x
```

```python
import functools
import math

import numpy as np
import jax
import jax.numpy as jnp
from jax import lax
from jax.experimental import pallas as pl
from jax.experimental.pallas import tpu as pltpu

F32 = jnp.float32
BF16 = jnp.bfloat16

D_MODEL = 1024
DEPTH = 2
EPS = 1e-6
NEG = -1e30
REMOVED = -3e38

NSA_HEADS = 8
NSA_G = 2
NSA_HPG = 4
DH = 64
CMP_STRIDE = 16
CMP_HIDDEN = 128
SLC_BLOCK = 64
SLC_TOPK = 16
SLC_LOCAL = 2
WINDOW = 512
QB = 128
GQ = NSA_HPG * QB
N_WIN = WINDOW + QB
WIN_TAB = WINDOW + N_WIN
SEL_TK = 256
NSA_PAIR = SEL_TK // QB
CHAIN_W = 256
SUM_ROWS = 16
CMP_BAND = 32
CMP_TAB = 64
MLA_HEADS = 8
MLA_Q_RANK = 256
MLA_KV_RANK = 128
MLA_NOPE = 64
MLA_ROPE = 32
MLA_QK = 96
ROPE_THETA = 10000.0
FOX_HEADS = 8
REL_BUCKETS = 32
REL_MAX_DIST = 128
N_GROUPS = 4
EPG = 8
N_EXPERTS = 32
EXPERT_DIM = 256

LANES = 128
LOG2E = 1.4426950408889634
VMEM_LIMIT = 56 * 1024 * 1024


def _bucket_thresholds():
    n = np.arange(0, 4 * REL_MAX_DIST)
    max_exact = REL_BUCKETS // 2
    lr = np.log(np.maximum(n, 1).astype(np.float32) / np.float32(max_exact)) / np.float32(math.log(REL_MAX_DIST / max_exact))
    large = max_exact + (lr.astype(np.float32) * np.float32(REL_BUCKETS - max_exact)).astype(np.int32)
    bucket = np.where(n < max_exact, n, np.minimum(large, REL_BUCKETS - 1))
    return [int(np.argmax(bucket >= b)) for b in range(1, REL_BUCKETS)]


BUCKET_THR = _bucket_thresholds()
assert BUCKET_THR[-1] <= REL_MAX_DIST


def _cparams(sem):
    return pltpu.CompilerParams(dimension_semantics=sem, vmem_limit_bytes=VMEM_LIMIT)


def _split3(x):
    hi = x.astype(BF16)
    r = x - hi.astype(F32)
    mid = r.astype(BF16)
    lo = (r - mid.astype(F32)).astype(BF16)
    return hi, mid, lo


def _lane_pick(x, idx):
    lane = lax.broadcasted_iota(jnp.int32, x.shape, x.ndim - 1)
    return jnp.sum(jnp.where(lane == idx, x, 0.0), axis=-1, keepdims=True)


def _online_softmax_steps(logits, vts, carries):
    stats = []
    for s, (m, _) in zip(logits, carries):
        m_new = jnp.maximum(m, jnp.max(s, axis=0, keepdims=True))
        stats.append((m_new, jnp.exp2(m - m_new), jnp.exp2(s - m_new).astype(BF16)))
    out = []
    for (m_new, alpha, p), vt, (_, acc) in zip(stats, vts, carries):
        out.append((m_new, alpha * acc + jnp.dot(vt, p, preferred_element_type=F32)))
    return tuple(out)


def _softmax_carry(dv, width):
    return jnp.full((1, width), NEG, F32), jnp.zeros((dv + SUM_ROWS, width), F32)


def _softmax_result(carry, dv):
    _, acc = carry
    return acc[0:dv] * (1.0 / acc[dv:dv + 1])


def _pipelined_softmax_loop(n_steps, qk_fn, vt_fn, carries, logit_sc, prob_sc):
    n_chains = len(carries)

    def value_products(kt, slot):
        return [jnp.dot(vt, prob_sc[slot, i], preferred_element_type=F32) for i, vt in enumerate(vt_fn(kt))]

    def keep(logits, slot):
        for i, s in enumerate(logits):
            logit_sc[slot, i] = s
        return tuple(jnp.max(s, axis=0, keepdims=True) for s in logits)

    def stage(kt, cur, nxt, state):
        carries, tile_max = state
        upcoming = qk_fn(kt + 1)
        pvs = value_products(jnp.maximum(kt - 1, 0), nxt)
        out = []
        for i, (pv, tm, (m, acc)) in enumerate(zip(pvs, tile_max, carries)):
            m_new = jnp.maximum(m, tm)
            prob_sc[cur, i] = jnp.exp2(logit_sc[cur, i] - m_new).astype(BF16)
            out.append((m_new, jnp.exp2(m - m_new) * (acc + pv)))
        return tuple(out), keep(upcoming, nxt)

    prob_sc[1] = jnp.zeros(prob_sc.shape[1:], BF16)
    state = (tuple(carries), keep(qk_fn(0), 0))
    state = lax.fori_loop(0, n_steps // 2, lambda i, c: stage(2 * i + 1, 1, 0, stage(2 * i, 0, 1, c)), state)
    odd = n_steps % 2
    carries, _ = lax.cond(odd == 1, lambda c: stage(n_steps - 1, 0, 1, c), lambda c: c, state)
    pvs = value_products(jnp.maximum(n_steps - 1, 0), 1 - odd)
    logits = tuple(logit_sc[odd, i] for i in range(n_chains))
    return logits, tuple((m, acc + pv) for (m, acc), pv in zip(carries, pvs))


def _bias_table_kernel(tab_ref, tbc_ref, tbs_ref, tbw_ref):
    h = pl.program_id(0)
    far = tab_ref[REL_BUCKETS - 1, h]

    def delta(dist):
        val = jnp.full(dist.shape, tab_ref[0, h], F32)
        for b in range(1, REL_BUCKETS):
            val = jnp.where(dist >= BUCKET_THR[b - 1], tab_ref[b, h], val)
        return (val - far) * LOG2E

    j = lax.broadcasted_iota(jnp.int32, (CMP_TAB, QB), 0)
    q = lax.broadcasted_iota(jnp.int32, (CMP_TAB, QB), 1)
    dist = q - CMP_STRIDE * j + (2 * QB - 2 * CMP_STRIDE + 1)
    band = jnp.where(j < CMP_BAND, jnp.where(dist >= 0, delta(dist), NEG), jnp.where(j == CMP_BAND, NEG, 0.0))
    band_hi = band.astype(BF16)
    tbc_ref[0] = band_hi
    tbc_ref[1] = (band - band_hi.astype(F32)).astype(BF16)
    c = lax.broadcasted_iota(jnp.int32, (SEL_TK, QB), 0)
    q = lax.broadcasted_iota(jnp.int32, (SEL_TK, QB), 1)
    for var, off in enumerate((SEL_TK, 0, QB)):
        dist = q + off - c
        tbs_ref[var] = jnp.where(dist >= 0, delta(dist), NEG)
    tbs_ref[3] = jnp.zeros((SEL_TK, QB), F32)
    c = lax.broadcasted_iota(jnp.int32, (WIN_TAB, QB), 0)
    q = lax.broadcasted_iota(jnp.int32, (WIN_TAB, QB), 1)
    dist = q + WINDOW - c
    tbw_ref[...] = jnp.where((dist >= 0) & (dist < WINDOW), delta(dist), NEG)


def _bias_tables(rel_bias):
    hq = NSA_HEADS * QB
    return pl.pallas_call(
        _bias_table_kernel,
        grid=(NSA_HEADS,),
        in_specs=[pl.BlockSpec(memory_space=pltpu.SMEM)],
        out_specs=[
            pl.BlockSpec((2, CMP_TAB, QB), lambda h: (0, 0, h)),
            pl.BlockSpec((4, SEL_TK, QB), lambda h: (0, 0, h)),
            pl.BlockSpec((WIN_TAB, QB), lambda h: (0, h)),
        ],
        out_shape=[
            jax.ShapeDtypeStruct((2, CMP_TAB, hq), BF16),
            jax.ShapeDtypeStruct((4, SEL_TK, hq), F32),
            jax.ShapeDtypeStruct((WIN_TAB, hq), F32),
        ],
        compiler_params=_cparams(("arbitrary",)),
        name="bias_tables",
    )(rel_bias)


HSEG = 128
_SEG_WIDTHS = (("NQ", 512), ("NCMP", 256), ("KS", NSA_G * HSEG), ("VS", 128), ("KW", NSA_G * HSEG), ("VW", 128),
               ("NG", 128), ("CQ", MLA_Q_RANK), ("CKV", MLA_KV_RANK), ("KRA", HSEG), ("KRB", HSEG), ("FQ", 512),
               ("FK", FOX_HEADS * HSEG), ("FV", 512), ("FF", 128))
SEG = {}
_off = 0
for _name, _w in _SEG_WIDTHS:
    SEG[_name] = (_off, _off + _w)
    _off += _w


def _rms(x, g):
    return x * lax.rsqrt(jnp.mean(x * x, axis=-1, keepdims=True) + EPS) * g


def _inproj_kernel(x_ref, pos_ref, g_ref, w1_ref, wuq_ref, wuk_ref, wuv_ref, qn_ref, kvn_ref, fb_ref, rot_ref,
                   tri_ref, place_ref, nq_ref, ncmp_ref, ksa_ref, vs_ref, kw_ref, vw_ref, ng_ref, mq_ref, mk_ref, mv_ref,
                   fq_ref, fk_ref, fv_ref, carry_ref):
    i = pl.program_id(1)
    tm = x_ref.shape[0]
    h = _rms(x_ref[...], g_ref[...]).astype(BF16)

    def seg(name):
        a, b = SEG[name]
        return jnp.dot(h, w1_ref[:, a:b], preferred_element_type=F32)

    nq = seg("NQ") * (DH ** -0.5 * LOG2E)
    for pair in range(NSA_HEADS // 2):
        pair_t = nq[:, pair * HSEG:(pair + 1) * HSEG].T.astype(BF16)
        for sub in range(2):
            g, hh = divmod(2 * pair + sub, NSA_HPG)
            for jb in range(tm // QB):
                nq_ref[g, jb, :, hh * QB:(hh + 1) * QB] = pair_t[sub * DH:(sub + 1) * DH, jb * QB:(jb + 1) * QB]
    ncmp_ref[...] = seg("NCMP").astype(BF16)
    def sum_rows(width):
        return jnp.where(lax.broadcasted_iota(jnp.int32, (SUM_ROWS, width), 0) == 0, 1.0, 0.0).astype(BF16)

    def put_value_tiles(ref, values, tile, with_sum):
        for pair in range(values.shape[1] // HSEG):
            pair_t = values[:, pair * HSEG:(pair + 1) * HSEG].astype(BF16).T
            for sub in range(2):
                for t in range(tm // tile):
                    ref[2 * pair + sub, t, 0:DH, :] = pair_t[sub * DH:(sub + 1) * DH, t * tile:(t + 1) * tile]
                    if with_sum:
                        ref[2 * pair + sub, t, DH:DH + SUM_ROWS, :] = sum_rows(tile)

    put_value_tiles(vs_ref, seg("VS"), SEL_TK, True)
    kw_ref[...] = seg("KW").astype(BF16)
    put_value_tiles(vw_ref, seg("VW"), QB, False)
    ng_ref[...] = jax.nn.sigmoid(seg("NG"))
    ks = seg("KS").astype(BF16)
    tok = i * tm + lax.broadcasted_iota(jnp.int32, (tm, LANES), 0)
    blk = (tok // SLC_BLOCK == lax.broadcasted_iota(jnp.int32, (tm, LANES), 1))
    blk = jnp.where(blk, 1.0, 0.0).astype(BF16)
    for g in range(NSA_G):
        ksa_ref[:, 2 * g * HSEG:(2 * g + 1) * HSEG] = ks[:, g * HSEG:(g + 1) * HSEG]
        ksa_ref[:, (2 * g + 1) * HSEG:(2 * g + 2) * HSEG] = blk

    ang = pos_ref[...].astype(F32) * rot_ref[0:1, :]
    cos = jnp.cos(ang)
    sin = jnp.sin(ang) * rot_ref[1:2, :]
    cos_h = jnp.concatenate([cos] * MLA_HEADS, axis=1)
    sin_h = jnp.concatenate([sin] * MLA_HEADS, axis=1)
    width = MLA_HEADS * HSEG
    c_q = _rms(seg("CQ"), qn_ref[...]).astype(BF16)
    q_a = jnp.dot(c_q, wuq_ref[:, 0:width], preferred_element_type=F32)
    q_b = jnp.dot(c_q, wuq_ref[:, width:2 * width], preferred_element_type=F32)
    mq = (q_a * cos_h + q_b * sin_h) * (MLA_QK ** -0.5 * LOG2E)
    for hd in range(MLA_HEADS):
        mq_ref[hd] = mq[:, hd * HSEG:(hd + 1) * HSEG].T.astype(BF16)
    c_kv = _rms(seg("CKV"), kvn_ref[...]).astype(BF16)
    k_rope = seg("KRA") * cos + seg("KRB") * sin
    k_nope = jnp.dot(c_kv, wuk_ref[...], preferred_element_type=F32)
    mk_ref[...] = (k_nope + jnp.concatenate([k_rope] * MLA_HEADS, axis=1)).astype(BF16)
    put_value_tiles(mv_ref, jnp.dot(c_kv, wuv_ref[...], preferred_element_type=F32), tm, True)

    fq = seg("FQ") * (DH ** -0.5 * LOG2E)
    ones_rows = jnp.where(lax.broadcasted_iota(jnp.int32, (HSEG - DH, tm), 0) < 3, 1.0, 0.0).astype(BF16)
    for pair in range(FOX_HEADS // 2):
        pair_t = fq[:, pair * HSEG:(pair + 1) * HSEG].T.astype(BF16)
        for sub in range(2):
            fq_ref[2 * pair + sub, 0:DH, :] = pair_t[sub * DH:(sub + 1) * DH]
            fq_ref[2 * pair + sub, DH:HSEG, :] = ones_rows
    put_value_tiles(fv_ref, seg("FV"), tm, True)
    log_f = jax.nn.log_sigmoid(seg("FF") + fb_ref[...])

    @pl.when(i == 0)
    def _():
        carry_ref[...] = jnp.zeros_like(carry_ref)

    tri = tri_ref[...]
    cum = carry_ref[...]
    for part in _split3(log_f):
        cum = cum + jnp.dot(tri, part, preferred_element_type=F32)
    carry_ref[...] = cum[tm - 1:tm, :]
    fk = seg("FK")
    for n, part in enumerate(_split3(cum * LOG2E)):
        fk = fk + jnp.dot(part, place_ref[n], preferred_element_type=F32)
    fk_ref[...] = fk.astype(BF16)


def _inproj(x, pos, g, w1, wuq, wuk, wuv, qn, kvn, fb, rot, tri, place, tm):
    B, T, D = x.shape
    row = lambda w: pl.BlockSpec((None, tm, w), lambda b, i: (b, i, 0))
    full = lambda a: pl.BlockSpec(a.shape, lambda b, i: (0,) * a.ndim)
    heads_t = lambda h: pl.BlockSpec((None, h, HSEG, tm), lambda b, i: (b, 0, 0, i))
    tok = lambda w, dt=BF16: (row(w), jax.ShapeDtypeStruct((B, T, w), dt))
    q_t = lambda h: (heads_t(h), jax.ShapeDtypeStruct((B, h, HSEG, T), BF16))
    v_t = lambda h, tile, extra: (
        pl.BlockSpec((None, h, tm // tile, DH + extra, tile), lambda b, i: (b, 0, i, 0, 0)),
        jax.ShapeDtypeStruct((B, h, T // tile, DH + extra, tile), BF16))
    outs = [
        (pl.BlockSpec((None, NSA_G, tm // QB, DH, GQ), lambda b, i: (b, 0, i, 0, 0)),
         jax.ShapeDtypeStruct((B, NSA_G, T // QB, DH, GQ), BF16)),
        tok(256), tok(2 * NSA_G * HSEG), v_t(NSA_G, SEL_TK, SUM_ROWS), tok(NSA_G * HSEG), v_t(NSA_G, QB, 0),
        tok(128, F32),
        q_t(MLA_HEADS), tok(MLA_HEADS * HSEG), v_t(MLA_HEADS, tm, SUM_ROWS),
        q_t(FOX_HEADS), tok(FOX_HEADS * HSEG), v_t(FOX_HEADS, tm, SUM_ROWS),
    ]
    return pl.pallas_call(
        _inproj_kernel,
        grid=(B, T // tm),
        in_specs=[row(D), row(1), full(g), full(w1), full(wuq), full(wuk), full(wuv), full(qn), full(kvn), full(fb),
                  full(rot), full(tri), full(place)],
        out_specs=[spec for spec, _ in outs],
        out_shape=[shape for _, shape in outs],
        scratch_shapes=[pltpu.VMEM((1, LANES), F32)],
        compiler_params=_cparams(("arbitrary", "arbitrary")),
        name="inproj",
    )(x, pos, g, w1, wuq, wuk, wuv, qn, kvn, fb, rot, tri, place)


def _flash_kernel(qt_ref, k_ref, vt_ref, o_ref, logit_sc, prob_sc, *, tq, hb):
    qi = pl.program_id(2)
    dv = vt_ref.shape[2] - SUM_ROWS
    chains = [(j, c) for j in range(hb) for c in range(tq // CHAIN_W)]
    qts = [qt_ref[j, :, c * CHAIN_W:(c + 1) * CHAIN_W] for j, c in chains]

    def qk(kt):
        st = pl.multiple_of(kt * tq, tq)
        return [jnp.dot(k_ref[pl.ds(st, tq), j * HSEG:(j + 1) * HSEG], qt, preferred_element_type=F32)
                for (j, _), qt in zip(chains, qts)]

    init = tuple(_softmax_carry(dv, CHAIN_W) for _ in chains)
    diag, carries = _pipelined_softmax_loop(qi, qk, lambda kt: [vt_ref[j, kt] for j, _ in chains], init,
                                            logit_sc, prob_sc)
    kpos = lax.broadcasted_iota(jnp.int32, (tq, CHAIN_W), 0)
    qpos = lax.broadcasted_iota(jnp.int32, (tq, CHAIN_W), 1)
    logits = [jnp.where(kpos <= qpos + c * CHAIN_W, s, NEG) for (_, c), s in zip(chains, diag)]
    final = _online_softmax_steps(logits, [vt_ref[j, qi] for j, _ in chains], carries)
    for (j, c), carry in zip(chains, final):
        o_ref[j, :, c * CHAIN_W:(c + 1) * CHAIN_W] = _softmax_result(carry, dv).astype(o_ref.dtype)


def _flash(qt, k, vt, *, tq, hb):
    B, H, dk, T = qt.shape
    dv = vt.shape[3] - SUM_ROWS
    return pl.pallas_call(
        functools.partial(_flash_kernel, tq=tq, hb=hb),
        grid=(B, H // hb, T // tq),
        in_specs=[
            pl.BlockSpec((None, hb, dk, tq), lambda b, h, i: (b, h, 0, i)),
            pl.BlockSpec((None, T, hb * HSEG), lambda b, h, i: (b, 0, h)),
            pl.BlockSpec((None, hb, T // tq, dv + SUM_ROWS, tq), lambda b, h, i: (b, h, 0, 0, 0)),
        ],
        out_specs=pl.BlockSpec((None, hb, dv, tq), lambda b, h, i: (b, h, 0, i)),
        out_shape=jax.ShapeDtypeStruct((B, H, dv, T), BF16),
        scratch_shapes=[pltpu.VMEM((2, hb * tq // CHAIN_W, tq, CHAIN_W), F32),
                        pltpu.VMEM((2, hb * tq // CHAIN_W, tq, CHAIN_W), BF16)],
        compiler_params=_cparams(("arbitrary", "arbitrary", "arbitrary")),
        name="flash_attention",
    )(qt, k, vt)


def _compress_kernel(sub_ref, pe_ref, w1_ref, b1_ref, w2_ref, o_ref):
    sub = sub_ref[...].astype(F32)
    n_sub = sub.shape[0]
    a = jnp.dot((sub + pe_ref[0:1, :]).astype(BF16), w1_ref[0], preferred_element_type=F32)
    b = jnp.dot((sub + pe_ref[1:2, :]).astype(BF16), w1_ref[1], preferred_element_type=F32)
    hid = jax.nn.gelu(a + pltpu.roll(b, n_sub - 1, 0) + b1_ref[...])
    o_ref[...] = jnp.dot(hid.astype(BF16), w2_ref[...], preferred_element_type=F32).astype(o_ref.dtype)


def _compress(sub, pe, w1, b1, w2):
    B, G, _, n_sub, _ = sub.shape
    return pl.pallas_call(
        _compress_kernel,
        grid=(B, G, 2),
        in_specs=[
            pl.BlockSpec((None, None, None, n_sub, 1024), lambda b, g, s: (b, g, s, 0, 0)),
            pl.BlockSpec((None, 2, 1024), lambda b, g, s: (s, 0, 0)),
            pl.BlockSpec((None, 2, 1024, CMP_HIDDEN), lambda b, g, s: (s, 0, 0, 0)),
            pl.BlockSpec((None, 1, CMP_HIDDEN), lambda b, g, s: (s, 0, 0)),
            pl.BlockSpec((None, CMP_HIDDEN, DH), lambda b, g, s: (s, 0, 0)),
        ],
        out_specs=pl.BlockSpec((None, None, None, n_sub, DH), lambda b, g, s: (b, g, s, 0, 0)),
        out_shape=jax.ShapeDtypeStruct((B, G, 2, n_sub, DH), BF16),
        compiler_params=_cparams(("arbitrary", "arbitrary", "arbitrary")),
        name="nsa_compress",
    )(sub, pe, w1, b1, w2)


def _nsa_kernel(qt_ref, kc_ref, vct_ref, ksa_ref, vst_ref, kw_ref, vwt_ref, gate_ref, tbc_hi_ref, tbc_lo_ref,
                tbs_ref, tbw_ref, ssum_ref, o_ref, logit_sc, prob_sc):
    pair = pl.program_id(2)
    ncp = kc_ref.shape[0]
    fronts = _nsa_fronts([NSA_PAIR * pair + i for i in range(NSA_PAIR)], [qt_ref[i] for i in range(NSA_PAIR)],
                         kc_ref, vct_ref, kw_ref, vwt_ref, tbc_hi_ref, tbc_lo_ref, tbw_ref, ssum_ref, ncp)

    n_tiles = pair + 1
    chains = [(i, c * CHAIN_W, (c + 1) * CHAIN_W) for i in range(NSA_PAIR) for c in range(GQ // CHAIN_W)]
    qas = []
    for i, a, b in chains:
        qt, unsel = qt_ref[i], fronts[i][1]
        qa = jnp.concatenate([qt, jnp.zeros((HSEG - DH, GQ), BF16), jnp.concatenate([unsel] * NSA_HPG, axis=1)],
                             axis=0)
        qas.append(qa[:, a:b])

    def sel_qk(kt):
        start = pl.multiple_of(kt * SEL_TK, SEL_TK)
        return [jnp.dot(ksa_ref[pl.ds(start, SEL_TK), :], qc, preferred_element_type=F32)
                for qc in qas]

    def sel_vt(kt):
        return [vst_ref[kt]] * len(chains)

    def near_step(kt, logits, variants, carries):
        logits = [s + tbs_ref[variants[i], :, a:b] for s, (i, a, b) in zip(logits, chains)]
        return _online_softmax_steps(logits, sel_vt(kt), carries)

    carries = tuple(_softmax_carry(DH, CHAIN_W) for _ in chains)
    n_far = jnp.maximum(n_tiles - 2, 0)
    logits, carries = _pipelined_softmax_loop(n_far, sel_qk, sel_vt, carries, logit_sc, prob_sc)
    logits, carries = lax.cond(
        n_tiles >= 2,
        lambda lg, c: (tuple(sel_qk(n_tiles - 1)), near_step(n_tiles - 2, lg, (0, 3), c)),
        lambda lg, c: (lg, c), logits, carries)
    carries = near_step(n_tiles - 1, logits, (1, 2), carries)

    per_block = GQ // CHAIN_W
    for i in range(NSA_PAIR):
        o_c, _, o_w = fronts[i]
        o_s = jnp.concatenate([_softmax_result(c, DH) for c in carries[i * per_block:(i + 1) * per_block]], axis=1)
        gates = gate_ref[i]
        o_ref[i] = (gates[0:1] * o_c + gates[1:2] * o_s + gates[2:3] * o_w).astype(o_ref.dtype)


def _nsa_fronts(qbs, qts, kc_ref, vct_ref, kw_ref, vwt_ref, tbc_hi_ref, tbc_lo_ref, tbw_ref, ssum_ref, ncp):
    blocks = range(len(qbs))

    row = lax.broadcasted_iota(jnp.int32, (ncp, 2 * HSEG), 0)
    lane = lax.broadcasted_iota(jnp.int32, (ncp, 2 * HSEG), 1)
    col = jnp.where((lane >= DH) & (lane < DH + 2 * CMP_TAB), (lane - DH) & (CMP_TAB - 1), CMP_TAB)
    keys = kc_ref[...]
    p_cs, o_cs = [], []
    for i in blocks:
        rel = row - ((QB // CMP_STRIDE) * qbs[i] - 16)
        lhs = jnp.where(jnp.minimum(rel, CMP_BAND) == col, 1.0, keys)
        rhs = jnp.concatenate([qts[i], tbc_hi_ref[...], tbc_lo_ref[...], jnp.zeros((2 * HSEG - DH - 2 * CMP_TAB, GQ), BF16)],
                              axis=0)
        s = jnp.dot(lhs.astype(BF16), rhs, preferred_element_type=F32)
        m = jnp.max(s, axis=0, keepdims=True)
        e = jnp.exp2(s - m)
        p_cs.append(e * jnp.where(m > 0.5 * NEG, 1.0 / jnp.sum(e, axis=0, keepdims=True), 0.0))
    for i in blocks:
        o_cs.append(jnp.dot(vct_ref[...], p_cs[i].astype(BF16), preferred_element_type=F32))

    ssum = ssum_ref[...]
    jj = lax.broadcasted_iota(jnp.int32, (LANES, QB), 0)
    jf = jj.astype(F32)
    causal, scores, sels = [], [], []
    for i in blocks:
        imp = p_cs[i][:, 0:QB]
        for hh in range(1, NSA_HPG):
            imp = imp + p_cs[i][:, hh * QB:(hh + 1) * QB]
        p_slc = sum(jnp.dot(ssum, part, preferred_element_type=F32) for part in _split3(imp))
        blk_cur = (qbs[i] * QB + lax.broadcasted_iota(jnp.int32, (LANES, QB), 1)) // SLC_BLOCK
        causal_b = jj <= blk_cur
        forced = (jj == 0) | (causal_b & (jj > blk_cur - SLC_LOCAL))
        causal.append(causal_b)
        scores.append(jnp.where(causal_b & jnp.logical_not(forced), p_slc, NEG))
        sels.append(jnp.where(forced, 1.0, 0.0))
    for _ in range(SLC_TOPK - 1 - SLC_LOCAL):
        for i in blocks:
            mx = jnp.max(scores[i], axis=0, keepdims=True)
            first = jnp.min(jnp.where(scores[i] == mx, jf, float(LANES)), axis=0, keepdims=True)
            hit = jf == first
            sels[i] = jnp.where(hit, 1.0, sels[i])
            scores[i] = jnp.where(hit, REMOVED, scores[i])
    unsels = [jnp.where(causal[i] & (sels[i] > 0.0), 0.0, NEG).astype(BF16) for i in blocks]

    o_ws = []
    for i in blocks:
        first = jnp.maximum(qbs[i] - WINDOW // QB, 0)
        start = pl.multiple_of(first * QB, QB)
        shift = pl.multiple_of(start + WINDOW - qbs[i] * QB, QB)
        s = (jnp.dot(kw_ref[pl.ds(start, N_WIN), 0:DH], qts[i], preferred_element_type=F32)
             + tbw_ref[pl.ds(shift, N_WIN), :])
        e = jnp.exp2(s - jnp.max(s, axis=0, keepdims=True))
        inv = 1.0 / jnp.sum(e, axis=0, keepdims=True)
        e = e.astype(BF16)
        o_w = jnp.zeros((DH, GQ), F32)
        for r in range(N_WIN // QB):
            o_w = o_w + jnp.dot(vwt_ref[first + r], e[r * QB:(r + 1) * QB], preferred_element_type=F32)
        o_ws.append(o_w * inv)

    return [(o_cs[i], unsels[i], o_ws[i]) for i in blocks]


def _nsa(qt, kc, vct, ksa, vst, kw, vwt, gates, tbc_hi, tbc_lo, tbs, tbw, ssum):
    B, G, nqb, _, _ = qt.shape
    T = ksa.shape[1]
    bg = lambda a: pl.BlockSpec((None, None) + a.shape[2:], lambda b, g, i: (b, g) + (0,) * (a.ndim - 2))
    lane_g = lambda w: pl.BlockSpec((None, T, w), lambda b, g, i: (b, 0, g))
    per_g = lambda a: pl.BlockSpec((None,) + a.shape[1:], lambda b, g, i: (g,) + (0,) * (a.ndim - 1))
    per_blk = lambda a: pl.BlockSpec((None, None, NSA_PAIR) + a.shape[3:],
                                     lambda b, g, i: (b, g, i) + (0,) * (a.ndim - 3))
    n_chains = NSA_PAIR * GQ // CHAIN_W
    return pl.pallas_call(
        _nsa_kernel,
        grid=(B, G, nqb // NSA_PAIR),
        in_specs=[per_blk(qt), bg(kc), bg(vct), lane_g(2 * HSEG), bg(vst), lane_g(HSEG), bg(vwt), per_blk(gates),
                  per_g(tbc_hi), per_g(tbc_lo), per_g(tbs), per_g(tbw),
                  pl.BlockSpec(ssum.shape, lambda b, g, i: (0, 0))],
        out_specs=pl.BlockSpec((None, None, NSA_PAIR, DH, GQ), lambda b, g, i: (b, g, i, 0, 0)),
        out_shape=jax.ShapeDtypeStruct((B, G, nqb, DH, GQ), BF16),
        scratch_shapes=[pltpu.VMEM((2, n_chains, SEL_TK, CHAIN_W), F32),
                        pltpu.VMEM((2, n_chains, SEL_TK, CHAIN_W), BF16)],
        compiler_params=_cparams(("arbitrary", "arbitrary", "arbitrary")),
        name="nsa_attention",
    )(qt, kc, vct, ksa, vst, kw, vwt, gates, tbc_hi, tbc_lo, tbs, tbw, ssum)


def _merge_kernel(x_ref, oa_ref, ob_ref, oc_ref, ga_ref, wm_ref, bm_ref, wo_ref, wout_ref, gf_ref, wgrp_ref, bgrp_ref,
                  wrt_ref, brt_ref, xo_ref, h2_ref, comb_ref):
    x = x_ref[...]
    h = _rms(x, ga_ref[...]).astype(BF16)
    mixed = None
    for i, o_ref in enumerate((oa_ref, ob_ref, oc_ref)):
        gate = jax.nn.sigmoid(jnp.dot(h, wm_ref[:, i * D_MODEL:(i + 1) * D_MODEL], preferred_element_type=F32)
                              + bm_ref[:, i * D_MODEL:(i + 1) * D_MODEL])
        term = gate * jnp.dot(o_ref[...], wo_ref[i], preferred_element_type=F32)
        mixed = term if mixed is None else mixed + term
    xn = x + jnp.dot(mixed.astype(BF16), wout_ref[...], preferred_element_type=F32)
    xo_ref[...] = xn
    h2 = _rms(xn, gf_ref[...])
    h2_ref[...] = h2.astype(BF16)

    lane = lax.broadcasted_iota(jnp.int32, (x.shape[0], LANES), 1).astype(F32)
    gl = jnp.dot(h2, wgrp_ref[...], preferred_element_type=F32, precision=lax.Precision.HIGHEST) + bgrp_ref[...]
    ge = jnp.exp(gl - jnp.max(gl, axis=-1, keepdims=True))
    gp = ge / jnp.sum(ge, axis=-1, keepdims=True)
    grp_top = jnp.max(gp, axis=-1, keepdims=True)
    g_idx = jnp.min(jnp.where(gp == grp_top, lane, float(LANES)), axis=-1, keepdims=True)
    el = jnp.dot(h2, wrt_ref[...], preferred_element_type=F32, precision=lax.Precision.HIGHEST) + brt_ref[...]
    in_grp = jnp.floor(lane * (1.0 / EPG)) == g_idx
    el = jnp.where(in_grp, el, NEG)
    ee = jnp.exp(el - jnp.max(el, axis=-1, keepdims=True))
    ep = jnp.where(in_grp, ee / jnp.sum(ee, axis=-1, keepdims=True), -1.0)
    p1 = jnp.max(ep, axis=-1, keepdims=True)
    i1 = jnp.min(jnp.where(ep == p1, lane, float(LANES)), axis=-1, keepdims=True)
    ep2 = jnp.where(lane == i1, -1.0, ep)
    p2 = jnp.max(ep2, axis=-1, keepdims=True)
    i2 = jnp.min(jnp.where(ep2 == p2, lane, float(LANES)), axis=-1, keepdims=True)
    den = p1 + p2
    comb_ref[...] = jnp.where(lane == i1, grp_top * p1 / den, 0.0) + jnp.where(lane == i2, grp_top * p2 / den, 0.0)


def _merge(x, oa, ob, oc, ga, wm, bm, wo, wout, gf, wgrp, bgrp, wrt, brt, tm):
    N, D = x.shape
    row = lambda w: pl.BlockSpec((tm, w), lambda i: (i, 0))
    full = lambda a: pl.BlockSpec(a.shape, lambda i: (0,) * a.ndim)
    return pl.pallas_call(
        _merge_kernel,
        grid=(N // tm,),
        in_specs=[row(D), row(512), row(512), row(512), full(ga), full(wm), full(bm), full(wo), full(wout), full(gf),
                  full(wgrp), full(bgrp), full(wrt), full(brt)],
        out_specs=[row(D), row(D), row(LANES)],
        out_shape=[jax.ShapeDtypeStruct((N, D), F32), jax.ShapeDtypeStruct((N, D), BF16),
                   jax.ShapeDtypeStruct((N, LANES), F32)],
        compiler_params=_cparams(("arbitrary",)),
        name="merge_router",
    )(x, oa, ob, oc, ga, wm, bm, wo, wout, gf, wgrp, bgrp, wrt, brt)


def _moe_kernel(x_ref, h_ref, comb_ref, wg_ref, wu_ref, wd_ref, gfin_ref, o_ref, acc_sc, *, final_norm):
    e = pl.program_id(1)

    @pl.when(e == 0)
    def _():
        acc_sc[...] = jnp.zeros_like(acc_sc)

    h = h_ref[...]
    a = jnp.dot(h, wg_ref[...], preferred_element_type=F32)
    u = jnp.dot(h, wu_ref[...], preferred_element_type=F32)
    c = _lane_pick(comb_ref[...], e)
    hid = (jax.nn.silu(a) * u * c).astype(BF16)
    acc_sc[...] += jnp.dot(hid, wd_ref[...], preferred_element_type=F32)

    @pl.when(e == pl.num_programs(1) - 1)
    def _():
        y = x_ref[...] + acc_sc[...]
        o_ref[...] = _rms(y, gfin_ref[...]) if final_norm else y


def _moe(x, h2, comb, wg, wu, wd, gfin, tm, final_norm):
    N, D = x.shape
    row = lambda w: pl.BlockSpec((tm, w), lambda i, e: (i, 0))
    return pl.pallas_call(
        functools.partial(_moe_kernel, final_norm=final_norm),
        grid=(N // tm, N_EXPERTS),
        in_specs=[row(D), row(D), row(LANES),
                  pl.BlockSpec((None, D, EXPERT_DIM), lambda i, e: (e, 0, 0)),
                  pl.BlockSpec((None, D, EXPERT_DIM), lambda i, e: (e, 0, 0)),
                  pl.BlockSpec((None, EXPERT_DIM, D), lambda i, e: (e, 0, 0)),
                  pl.BlockSpec((1, D), lambda i, e: (0, 0))],
        out_specs=row(D),
        out_shape=jax.ShapeDtypeStruct((N, D), F32),
        scratch_shapes=[pltpu.VMEM((tm, D), F32)],
        compiler_params=_cparams(("arbitrary", "arbitrary")),
        name="moe_experts",
    )(x, h2, comb, wg, wu, wd, gfin)


def _pad_cols(w, n):
    return jnp.pad(w, ((0, 0), (0, n - w.shape[1])))


def _pack_w1(w_in_l):
    o = np.cumsum([0, 512, 768, 24, MLA_Q_RANK, MLA_KV_RANK, MLA_ROPE, 1536, FOX_HEADS])
    nq, nkv, ng, cq, ckv, kr, fox, ff = (w_in_l[:, o[i]:o[i + 1]] for i in range(8))
    d_in = w_in_l.shape[0]
    half = MLA_ROPE // 2
    nkv = nkv.reshape(d_in, 3, 2, NSA_G, DH)
    segs = lambda w: _pad_feat(w, HSEG).reshape(d_in, -1)
    z = lambda n: jnp.zeros((d_in, n), w_in_l.dtype)
    x1, x2 = kr[:, :half], kr[:, half:]
    parts = {
        "NQ": nq, "NCMP": nkv[:, 0].reshape(d_in, -1), "KS": segs(nkv[:, 1, 0]), "VS": nkv[:, 1, 1].reshape(d_in, -1),
        "KW": segs(nkv[:, 2, 0]), "VW": nkv[:, 2, 1].reshape(d_in, -1), "NG": _pad_cols(ng, LANES), "CQ": cq, "CKV": ckv,
        "KRA": jnp.concatenate([z(MLA_NOPE), x1, x2, z(HSEG - MLA_QK)], axis=1),
        "KRB": jnp.concatenate([z(MLA_NOPE), x2, x1, z(HSEG - MLA_QK)], axis=1),
        "FQ": fox[:, :512], "FK": segs(fox[:, 512:1024].reshape(d_in, FOX_HEADS, DH)), "FV": fox[:, 1024:],
        "FF": _pad_cols(ff, LANES),
    }
    w1 = jnp.concatenate([parts[name] for name, _ in _SEG_WIDTHS], axis=1)
    return w1.astype(BF16), w_in_l[:, o[8]:]


def _pack_wuq(w_uq_l):
    w = w_uq_l.reshape(MLA_Q_RANK, MLA_HEADS, MLA_QK)
    half = MLA_ROPE // 2
    nope, r1, r2 = w[:, :, :MLA_NOPE], w[:, :, MLA_NOPE:MLA_NOPE + half], w[:, :, MLA_NOPE + half:]
    direct = _pad_feat(jnp.concatenate([nope, r1, r2], axis=-1), HSEG).reshape(MLA_Q_RANK, -1)
    swapped = _pad_feat(jnp.concatenate([jnp.zeros_like(nope), r2, r1], axis=-1), HSEG).reshape(MLA_Q_RANK, -1)
    return jnp.concatenate([direct, swapped], axis=1).astype(BF16)


def _rotary_tables():
    half = MLA_ROPE // 2
    inv_freq = ROPE_THETA ** (-jnp.arange(half, dtype=F32) / half)
    zeros = lambda n: jnp.zeros((n,), F32)
    freq = jnp.concatenate([zeros(MLA_NOPE), inv_freq, inv_freq, zeros(HSEG - MLA_QK)])
    sign = jnp.concatenate([zeros(MLA_NOPE), -jnp.ones((half,), F32), jnp.ones((half,), F32), zeros(HSEG - MLA_QK)])
    return jnp.stack([freq, sign])


def _pad_feat(a, n):
    return jnp.pad(a, ((0, 0),) * (a.ndim - 1) + ((0, n - a.shape[-1]),))


def _ssum_matrix(ncp):
    n = np.arange(ncp)[None, :]
    j = np.arange(LANES)[:, None]
    d = n - 4 * j
    m = np.where((d == 0) | (d == 4), 1.0, np.where((d >= 1) & (d <= 3), 2.0, 0.0))
    return jnp.asarray(m, BF16)


def kernel(x, positions, rel_bias, norm_attn, w_in, b_merge, cmp_pe, cmp_k_w1, cmp_k_b1, cmp_k_w2, cmp_v_w1, cmp_v_b1,
           cmp_v_w2, mla_q_norm, mla_kv_norm, mla_w_uq, mla_w_uk, mla_w_uv, fox_b_f, w_o_nsa, w_o_mla, w_o_fox, w_out,
           norm_ffn, moe_w_grp, moe_b_grp, moe_w_router, moe_b_router, moe_w_gate, moe_w_up, moe_w_down, norm_final):
    B, T, D = x.shape
    N = B * T
    n_sub = T // CMP_STRIDE
    nqb = T // QB
    tm_in = min(512, T)
    fl_tq, fl_hb = min(512, T), 2
    tm_merge = 512
    tm_moe = 1024

    tbc, tbs, tbw = _bias_tables(rel_bias)
    tbc = tbc.reshape(2, CMP_TAB, NSA_G, GQ).transpose(0, 2, 1, 3)
    tbc_hi, tbc_lo = tbc[0], tbc[1]
    tbs = tbs.reshape(4, SEL_TK, NSA_G, GQ).transpose(2, 0, 1, 3)
    tbw = tbw.reshape(WIN_TAB, NSA_G, GQ).transpose(1, 0, 2)
    ssum = _ssum_matrix(n_sub)
    rot = _rotary_tables()
    cum_place = np.zeros((3, LANES, FOX_HEADS * HSEG), np.float32)
    for n in range(3):
        cum_place[n, np.arange(FOX_HEADS), np.arange(FOX_HEADS) * HSEG + DH + n] = -1.0
    cum_place = jnp.asarray(cum_place, BF16)
    tri =jnp.asarray(np.tril(np.ones((tm_in, tm_in), np.float32)), BF16)
    pos3 = positions.reshape(B, T, 1)

    for l in range(DEPTH):
        w1, w_merge = _pack_w1(w_in[l])
        fb = _pad_cols(fox_b_f[l].reshape(1, -1), LANES)
        wuk = _pad_feat(mla_w_uk[l].reshape(MLA_KV_RANK, MLA_HEADS, MLA_NOPE), HSEG).reshape(MLA_KV_RANK, -1)
        qt_a, ncmp, ksa, vs_t, kw, vw_t, ng, mq_t, mk, mv_t, fq_t, fk, fv_t = _inproj(
            x, pos3, norm_attn[l].reshape(1, D), w1, _pack_wuq(mla_w_uq[l]), wuk.astype(BF16),
            mla_w_uv[l].astype(BF16), mla_q_norm[l].reshape(1, -1), mla_kv_norm[l].reshape(1, -1), fb, rot, tri,
            cum_place, tm_in)

        sub = ncmp.reshape(B, T, 2, NSA_G, DH).transpose(0, 3, 2, 1, 4).reshape(B, NSA_G, 2, n_sub, CMP_STRIDE * DH)
        pe = jnp.broadcast_to(cmp_pe[l].reshape(1, 2, CMP_STRIDE * DH), (2, 2, CMP_STRIDE * DH))
        cw1 = jnp.stack([cmp_k_w1[l], cmp_v_w1[l]]).reshape(2, 2, CMP_STRIDE * DH, CMP_HIDDEN).astype(BF16)
        cb1 = jnp.stack([cmp_k_b1[l], cmp_v_b1[l]]).reshape(2, 1, CMP_HIDDEN)
        cw2 = jnp.stack([cmp_k_w2[l], cmp_v_w2[l]]).astype(BF16)
        cmp = _compress(sub, pe, cw1, cb1, cw2)
        gates_t = ng[..., :3 * NSA_HEADS].reshape(B, nqb, QB, NSA_G, NSA_HPG, 3).transpose(0, 3, 1, 5, 4, 2)
        gates_t = jnp.pad(gates_t.reshape(B, NSA_G, nqb, 3, GQ), ((0, 0), (0, 0), (0, 0), (0, 5), (0, 0)))
        o_a = _nsa(qt_a, _pad_feat(cmp[:, :, 0], 2 * HSEG), cmp[:, :, 1].transpose(0, 1, 3, 2), ksa,
                   vs_t, kw, vw_t,
                   gates_t, tbc_hi, tbc_lo, tbs, tbw, ssum)
        out_a = o_a.reshape(B, NSA_G, nqb, DH, NSA_HPG, QB).transpose(0, 2, 5, 1, 4, 3).reshape(N, NSA_HEADS * DH)

        o_b = _flash(mq_t, mk, mv_t, tq=fl_tq, hb=fl_hb)
        out_b = o_b.transpose(0, 3, 1, 2).reshape(N, -1)

        o_c = _flash(fq_t, fk, fv_t, tq=fl_tq, hb=fl_hb)
        out_c = o_c.transpose(0, 3, 1, 2).reshape(N, -1)

        wo = jnp.stack([w_o_nsa[l], w_o_mla[l], w_o_fox[l]]).astype(BF16)
        pad_b = lambda b, n: jnp.concatenate([b, jnp.full((LANES - n,), NEG, F32)]).reshape(1, LANES)
        x2, h2, comb = _merge(
            x.reshape(N, D), out_a, out_b, out_c, norm_attn[l].reshape(1, D), w_merge.astype(BF16),
            b_merge[l].reshape(1, -1), wo, w_out[l].astype(BF16), norm_ffn[l].reshape(1, D),
            _pad_cols(moe_w_grp[l], LANES), pad_b(moe_b_grp[l], N_GROUPS),
            _pad_cols(moe_w_router[l], LANES), pad_b(moe_b_router[l], N_EXPERTS), tm_merge)

        x = _moe(x2, h2, comb, moe_w_gate[l].astype(BF16), moe_w_up[l].astype(BF16), moe_w_down[l].astype(BF16),
                 norm_final.reshape(1, D), tm_moe, final_norm=(l == DEPTH - 1)).reshape(B, T, D)
    return x
```

```python
import functools
import math

import numpy as np
import jax
import jax.numpy as jnp
from jax import lax
from jax.experimental import pallas as pl
from jax.experimental.pallas import tpu as pltpu

F32 = jnp.float32
BF16 = jnp.bfloat16

D_MODEL = 1024
DEPTH = 2
EPS = 1e-6
NEG = -1e30
REMOVED = -3e38

NSA_HEADS = 8
NSA_G = 2
NSA_HPG = 4
DH = 64
CMP_STRIDE = 16
CMP_HIDDEN = 128
SLC_BLOCK = 64
SLC_TOPK = 16
SLC_LOCAL = 2
WINDOW = 512
QB = 128
GQ = NSA_HPG * QB
N_WIN = WINDOW + QB
WIN_TAB = WINDOW + N_WIN
SEL_TK = 256
NSA_PAIR = SEL_TK // QB
CHAIN_W = 256
SUM_ROWS = 16
CMP_BAND = 32
CMP_TAB = 64
MLA_HEADS = 8
MLA_Q_RANK = 256
MLA_KV_RANK = 128
MLA_NOPE = 64
MLA_ROPE = 32
MLA_QK = 96
ROPE_THETA = 10000.0
FOX_HEADS = 8
REL_BUCKETS = 32
REL_MAX_DIST = 128
N_GROUPS = 4
EPG = 8
N_EXPERTS = 32
EXPERT_DIM = 256

LANES = 128
LOG2E = 1.4426950408889634
VMEM_LIMIT = 56 * 1024 * 1024


def _bucket_thresholds():
    n = np.arange(0, 4 * REL_MAX_DIST)
    max_exact = REL_BUCKETS // 2
    lr = np.log(np.maximum(n, 1).astype(np.float32) / np.float32(max_exact)) / np.float32(math.log(REL_MAX_DIST / max_exact))
    large = max_exact + (lr.astype(np.float32) * np.float32(REL_BUCKETS - max_exact)).astype(np.int32)
    bucket = np.where(n < max_exact, n, np.minimum(large, REL_BUCKETS - 1))
    return [int(np.argmax(bucket >= b)) for b in range(1, REL_BUCKETS)]


BUCKET_THR = _bucket_thresholds()
assert BUCKET_THR[-1] <= REL_MAX_DIST


def _cparams(sem):
    return pltpu.CompilerParams(dimension_semantics=sem, vmem_limit_bytes=VMEM_LIMIT)


def _split3(x):
    hi = x.astype(BF16)
    r = x - hi.astype(F32)
    mid = r.astype(BF16)
    lo = (r - mid.astype(F32)).astype(BF16)
    return hi, mid, lo


def _lane_pick(x, idx):
    lane = lax.broadcasted_iota(jnp.int32, x.shape, x.ndim - 1)
    return jnp.sum(jnp.where(lane == idx, x, 0.0), axis=-1, keepdims=True)


def _online_softmax_steps(logits, vts, carries):
    stats = []
    for s, (m, _) in zip(logits, carries):
        m_new = jnp.maximum(m, jnp.max(s, axis=0, keepdims=True))
        stats.append((m_new, jnp.exp2(m - m_new), jnp.exp2(s - m_new).astype(BF16)))
    out = []
    for (m_new, alpha, p), vt, (_, acc) in zip(stats, vts, carries):
        out.append((m_new, alpha * acc + jnp.dot(vt, p, preferred_element_type=F32)))
    return tuple(out)


def _softmax_carry(dv, width):
    return jnp.full((1, width), NEG, F32), jnp.zeros((dv + SUM_ROWS, width), F32)


def _softmax_result(carry, dv):
    _, acc = carry
    return acc[0:dv] * (1.0 / acc[dv:dv + 1])


def _pipelined_softmax_loop(n_steps, qk_fn, vt_fn, carries, logit_sc, prob_sc):
    n_chains = len(carries)

    def value_products(kt, slot):
        return [jnp.dot(vt, prob_sc[slot, i], preferred_element_type=F32) for i, vt in enumerate(vt_fn(kt))]

    def keep(logits, slot):
        for i, s in enumerate(logits):
            logit_sc[slot, i] = s
        return tuple(jnp.max(s, axis=0, keepdims=True) for s in logits)

    def stage(kt, cur, nxt, state):
        carries, tile_max = state
        upcoming = qk_fn(kt + 1)
        pvs = value_products(jnp.maximum(kt - 1, 0), nxt)
        out = []
        for i, (pv, tm, (m, acc)) in enumerate(zip(pvs, tile_max, carries)):
            m_new = jnp.maximum(m, tm)
            prob_sc[cur, i] = jnp.exp2(logit_sc[cur, i] - m_new).astype(BF16)
            out.append((m_new, jnp.exp2(m - m_new) * (acc + pv)))
        return tuple(out), keep(upcoming, nxt)

    prob_sc[1] = jnp.zeros(prob_sc.shape[1:], BF16)
    state = (tuple(carries), keep(qk_fn(0), 0))
    state = lax.fori_loop(0, n_steps // 2, lambda i, c: stage(2 * i + 1, 1, 0, stage(2 * i, 0, 1, c)), state)
    odd = n_steps % 2
    carries, _ = lax.cond(odd == 1, lambda c: stage(n_steps - 1, 0, 1, c), lambda c: c, state)
    pvs = value_products(jnp.maximum(n_steps - 1, 0), 1 - odd)
    logits = tuple(logit_sc[odd, i] for i in range(n_chains))
    return logits, tuple((m, acc + pv) for (m, acc), pv in zip(carries, pvs))


def _bias_table_kernel(tab_ref, tbc_ref, tbs_ref, tbw_ref):
    h = pl.program_id(0)
    far = tab_ref[REL_BUCKETS - 1, h]

    def delta(dist):
        val = jnp.full(dist.shape, tab_ref[0, h], F32)
        for b in range(1, REL_BUCKETS):
            val = jnp.where(dist >= BUCKET_THR[b - 1], tab_ref[b, h], val)
        return (val - far) * LOG2E

    j = lax.broadcasted_iota(jnp.int32, (CMP_TAB, QB), 0)
    q = lax.broadcasted_iota(jnp.int32, (CMP_TAB, QB), 1)
    dist = q - CMP_STRIDE * j + (2 * QB - 2 * CMP_STRIDE + 1)
    band = jnp.where(j < CMP_BAND, jnp.where(dist >= 0, delta(dist), NEG), jnp.where(j == CMP_BAND, NEG, 0.0))
    band_hi = band.astype(BF16)
    tbc_ref[0] = band_hi
    tbc_ref[1] = (band - band_hi.astype(F32)).astype(BF16)
    c = lax.broadcasted_iota(jnp.int32, (SEL_TK, QB), 0)
    q = lax.broadcasted_iota(jnp.int32, (SEL_TK, QB), 1)
    for var, off in enumerate((SEL_TK, 0, QB)):
        dist = q + off - c
        tbs_ref[var] = jnp.where(dist >= 0, delta(dist), NEG)
    tbs_ref[3] = jnp.zeros((SEL_TK, QB), F32)
    c = lax.broadcasted_iota(jnp.int32, (WIN_TAB, QB), 0)
    q = lax.broadcasted_iota(jnp.int32, (WIN_TAB, QB), 1)
    dist = q + WINDOW - c
    tbw_ref[...] = jnp.where((dist >= 0) & (dist < WINDOW), delta(dist), NEG)


def _bias_tables(rel_bias):
    hq = NSA_HEADS * QB
    return pl.pallas_call(
        _bias_table_kernel,
        grid=(NSA_HEADS,),
        in_specs=[pl.BlockSpec(memory_space=pltpu.SMEM)],
        out_specs=[
            pl.BlockSpec((2, CMP_TAB, QB), lambda h: (0, 0, h)),
            pl.BlockSpec((4, SEL_TK, QB), lambda h: (0, 0, h)),
            pl.BlockSpec((WIN_TAB, QB), lambda h: (0, h)),
        ],
        out_shape=[
            jax.ShapeDtypeStruct((2, CMP_TAB, hq), BF16),
            jax.ShapeDtypeStruct((4, SEL_TK, hq), F32),
            jax.ShapeDtypeStruct((WIN_TAB, hq), F32),
        ],
        compiler_params=_cparams(("arbitrary",)),
        name="bias_tables",
    )(rel_bias)


HSEG = 128
_SEG_WIDTHS = (("NQ", 512), ("NCMP", 256), ("KS", NSA_G * HSEG), ("VS", 128), ("KW", NSA_G * HSEG), ("VW", 128),
               ("NG", 128), ("CQ", MLA_Q_RANK), ("CKV", MLA_KV_RANK), ("KRA", HSEG), ("KRB", HSEG), ("FQ", 512),
               ("FK", FOX_HEADS * HSEG), ("FV", 512), ("FF", 128))
SEG = {}
_off = 0
for _name, _w in _SEG_WIDTHS:
    SEG[_name] = (_off, _off + _w)
    _off += _w


def _rms(x, g):
    return x * lax.rsqrt(jnp.mean(x * x, axis=-1, keepdims=True) + EPS) * g


def _inproj_kernel(x_ref, pos_ref, g_ref, w1_ref, wuq_ref, wuk_ref, wuv_ref, qn_ref, kvn_ref, fb_ref, rot_ref,
                   tri_ref, place_ref, nq_ref, ncmp_ref, ksa_ref, vs_ref, kw_ref, vw_ref, ng_ref, mq_ref, mk_ref, mv_ref,
                   fq_ref, fk_ref, fv_ref, carry_ref):
    i = pl.program_id(1)
    tm = x_ref.shape[0]
    h = _rms(x_ref[...], g_ref[...]).astype(BF16)

    def seg(name):
        a, b = SEG[name]
        return jnp.dot(h, w1_ref[:, a:b], preferred_element_type=F32)

    nq = seg("NQ") * (DH ** -0.5 * LOG2E)
    for pair in range(NSA_HEADS // 2):
        pair_t = nq[:, pair * HSEG:(pair + 1) * HSEG].T.astype(BF16)
        for sub in range(2):
            g, hh = divmod(2 * pair + sub, NSA_HPG)
            for jb in range(tm // QB):
                nq_ref[g, jb, :, hh * QB:(hh + 1) * QB] = pair_t[sub * DH:(sub + 1) * DH, jb * QB:(jb + 1) * QB]
    ncmp_ref[...] = seg("NCMP").astype(BF16)
    def sum_rows(width):
        return jnp.where(lax.broadcasted_iota(jnp.int32, (SUM_ROWS, width), 0) == 0, 1.0, 0.0).astype(BF16)

    def put_value_tiles(ref, values, tile, with_sum):
        for pair in range(values.shape[1] // HSEG):
            pair_t = values[:, pair * HSEG:(pair + 1) * HSEG].astype(BF16).T
            for sub in range(2):
                for t in range(tm // tile):
                    ref[2 * pair + sub, t, 0:DH, :] = pair_t[sub * DH:(sub + 1) * DH, t * tile:(t + 1) * tile]
                    if with_sum:
                        ref[2 * pair + sub, t, DH:DH + SUM_ROWS, :] = sum_rows(tile)

    put_value_tiles(vs_ref, seg("VS"), SEL_TK, True)
    kw_ref[...] = seg("KW").astype(BF16)
    put_value_tiles(vw_ref, seg("VW"), QB, False)
    ng_ref[...] = jax.nn.sigmoid(seg("NG"))
    ks = seg("KS").astype(BF16)
    tok = i * tm + lax.broadcasted_iota(jnp.int32, (tm, LANES), 0)
    blk = (tok // SLC_BLOCK == lax.broadcasted_iota(jnp.int32, (tm, LANES), 1))
    blk = jnp.where(blk, 1.0, 0.0).astype(BF16)
    for g in range(NSA_G):
        ksa_ref[:, 2 * g * HSEG:(2 * g + 1) * HSEG] = ks[:, g * HSEG:(g + 1) * HSEG]
        ksa_ref[:, (2 * g + 1) * HSEG:(2 * g + 2) * HSEG] = blk

    ang = pos_ref[...].astype(F32) * rot_ref[0:1, :]
    cos = jnp.cos(ang)
    sin = jnp.sin(ang) * rot_ref[1:2, :]
    cos_h = jnp.concatenate([cos] * MLA_HEADS, axis=1)
    sin_h = jnp.concatenate([sin] * MLA_HEADS, axis=1)
    width = MLA_HEADS * HSEG
    c_q = _rms(seg("CQ"), qn_ref[...]).astype(BF16)
    q_a = jnp.dot(c_q, wuq_ref[:, 0:width], preferred_element_type=F32)
    q_b = jnp.dot(c_q, wuq_ref[:, width:2 * width], preferred_element_type=F32)
    mq = (q_a * cos_h + q_b * sin_h) * (MLA_QK ** -0.5 * LOG2E)
    for hd in range(MLA_HEADS):
        mq_ref[hd] = mq[:, hd * HSEG:(hd + 1) * HSEG].T.astype(BF16)
    c_kv = _rms(seg("CKV"), kvn_ref[...]).astype(BF16)
    k_rope = seg("KRA") * cos + seg("KRB") * sin
    k_nope = jnp.dot(c_kv, wuk_ref[...], preferred_element_type=F32)
    mk_ref[...] = (k_nope + jnp.concatenate([k_rope] * MLA_HEADS, axis=1)).astype(BF16)
    put_value_tiles(mv_ref, jnp.dot(c_kv, wuv_ref[...], preferred_element_type=F32), tm, True)

    fq = seg("FQ") * (DH ** -0.5 * LOG2E)
    ones_rows = jnp.where(lax.broadcasted_iota(jnp.int32, (HSEG - DH, tm), 0) < 3, 1.0, 0.0).astype(BF16)
    for pair in range(FOX_HEADS // 2):
        pair_t = fq[:, pair * HSEG:(pair + 1) * HSEG].T.astype(BF16)
        for sub in range(2):
            fq_ref[2 * pair + sub, 0:DH, :] = pair_t[sub * DH:(sub + 1) * DH]
            fq_ref[2 * pair + sub, DH:HSEG, :] = ones_rows
    put_value_tiles(fv_ref, seg("FV"), tm, True)
    log_f = jax.nn.log_sigmoid(seg("FF") + fb_ref[...])

    @pl.when(i == 0)
    def _():
        carry_ref[...] = jnp.zeros_like(carry_ref)

    tri = tri_ref[...]
    cum = carry_ref[...]
    for part in _split3(log_f):
        cum = cum + jnp.dot(tri, part, preferred_element_type=F32)
    carry_ref[...] = cum[tm - 1:tm, :]
    fk = seg("FK")
    for n, part in enumerate(_split3(cum * LOG2E)):
        fk = fk + jnp.dot(part, place_ref[n], preferred_element_type=F32)
    fk_ref[...] = fk.astype(BF16)


def _inproj(x, pos, g, w1, wuq, wuk, wuv, qn, kvn, fb, rot, tri, place, tm):
    B, T, D = x.shape
    row = lambda w: pl.BlockSpec((None, tm, w), lambda b, i: (b, i, 0))
    full = lambda a: pl.BlockSpec(a.shape, lambda b, i: (0,) * a.ndim)
    heads_t = lambda h: pl.BlockSpec((None, h, HSEG, tm), lambda b, i: (b, 0, 0, i))
    tok = lambda w, dt=BF16: (row(w), jax.ShapeDtypeStruct((B, T, w), dt))
    q_t = lambda h: (heads_t(h), jax.ShapeDtypeStruct((B, h, HSEG, T), BF16))
    v_t = lambda h, tile, extra: (
        pl.BlockSpec((None, h, tm // tile, DH + extra, tile), lambda b, i: (b, 0, i, 0, 0)),
        jax.ShapeDtypeStruct((B, h, T // tile, DH + extra, tile), BF16))
    outs = [
        (pl.BlockSpec((None, NSA_G, tm // QB, DH, GQ), lambda b, i: (b, 0, i, 0, 0)),
         jax.ShapeDtypeStruct((B, NSA_G, T // QB, DH, GQ), BF16)),
        tok(256), tok(2 * NSA_G * HSEG), v_t(NSA_G, SEL_TK, SUM_ROWS), tok(NSA_G * HSEG), v_t(NSA_G, QB, 0),
        tok(128, F32),
        q_t(MLA_HEADS), tok(MLA_HEADS * HSEG), v_t(MLA_HEADS, tm, SUM_ROWS),
        q_t(FOX_HEADS), tok(FOX_HEADS * HSEG), v_t(FOX_HEADS, tm, SUM_ROWS),
    ]
    return pl.pallas_call(
        _inproj_kernel,
        grid=(B, T // tm),
        in_specs=[row(D), row(1), full(g), full(w1), full(wuq), full(wuk), full(wuv), full(qn), full(kvn), full(fb),
                  full(rot), full(tri), full(place)],
        out_specs=[spec for spec, _ in outs],
        out_shape=[shape for _, shape in outs],
        scratch_shapes=[pltpu.VMEM((1, LANES), F32)],
        compiler_params=_cparams(("arbitrary", "arbitrary")),
        name="inproj",
    )(x, pos, g, w1, wuq, wuk, wuv, qn, kvn, fb, rot, tri, place)


def _flash_kernel(qt_ref, k_ref, vt_ref, o_ref, logit_sc, prob_sc, *, tq, hb):
    qi = pl.program_id(2)
    dv = vt_ref.shape[2] - SUM_ROWS
    chains = [(j, c) for j in range(hb) for c in range(tq // CHAIN_W)]
    qts = [qt_ref[j, :, c * CHAIN_W:(c + 1) * CHAIN_W] for j, c in chains]

    def qk(kt):
        st = pl.multiple_of(kt * tq, tq)
        return [jnp.dot(k_ref[pl.ds(st, tq), j * HSEG:(j + 1) * HSEG], qt, preferred_element_type=F32)
                for (j, _), qt in zip(chains, qts)]

    init = tuple(_softmax_carry(dv, CHAIN_W) for _ in chains)
    diag, carries = _pipelined_softmax_loop(qi, qk, lambda kt: [vt_ref[j, kt] for j, _ in chains], init,
                                            logit_sc, prob_sc)
    kpos = lax.broadcasted_iota(jnp.int32, (tq, CHAIN_W), 0)
    qpos = lax.broadcasted_iota(jnp.int32, (tq, CHAIN_W), 1)
    logits = [jnp.where(kpos <= qpos + c * CHAIN_W, s, NEG) for (_, c), s in zip(chains, diag)]
    final = _online_softmax_steps(logits, [vt_ref[j, qi] for j, _ in chains], carries)
    results = {jc: _softmax_result(carry, dv) for jc, carry in zip(chains, final)}
    for c in range(tq // CHAIN_W):
        stacked = jnp.concatenate([results[(j, c)] for j in range(hb)], axis=0)
        o_ref[c * CHAIN_W:(c + 1) * CHAIN_W, :] = stacked.T.astype(o_ref.dtype)


def _flash(qt, k, vt, *, tq, hb):
    B, H, dk, T = qt.shape
    dv = vt.shape[3] - SUM_ROWS
    return pl.pallas_call(
        functools.partial(_flash_kernel, tq=tq, hb=hb),
        grid=(B, H // hb, T // tq),
        in_specs=[
            pl.BlockSpec((None, hb, dk, tq), lambda b, h, i: (b, h, 0, i)),
            pl.BlockSpec((None, T, hb * HSEG), lambda b, h, i: (b, 0, h)),
            pl.BlockSpec((None, hb, T // tq, dv + SUM_ROWS, tq), lambda b, h, i: (b, h, 0, 0, 0)),
        ],
        out_specs=pl.BlockSpec((None, tq, hb * dv), lambda b, h, i: (b, i, h)),
        out_shape=jax.ShapeDtypeStruct((B, T, H * dv), BF16),
        scratch_shapes=[pltpu.VMEM((2, hb * tq // CHAIN_W, tq, CHAIN_W), F32),
                        pltpu.VMEM((2, hb * tq // CHAIN_W, tq, CHAIN_W), BF16)],
        compiler_params=_cparams(("arbitrary", "arbitrary", "arbitrary")),
        name="flash_attention",
    )(qt, k, vt)


def _compress_kernel(sub_ref, pe_ref, w1_ref, b1_ref, w2_ref, o_ref):
    sub = sub_ref[...].astype(F32)
    n_sub = sub.shape[0]
    a = jnp.dot((sub + pe_ref[0:1, :]).astype(BF16), w1_ref[0], preferred_element_type=F32)
    b = jnp.dot((sub + pe_ref[1:2, :]).astype(BF16), w1_ref[1], preferred_element_type=F32)
    hid = jax.nn.gelu(a + pltpu.roll(b, n_sub - 1, 0) + b1_ref[...])
    o_ref[...] = jnp.dot(hid.astype(BF16), w2_ref[...], preferred_element_type=F32).astype(o_ref.dtype)


def _compress(sub, pe, w1, b1, w2):
    B, G, _, n_sub, _ = sub.shape
    return pl.pallas_call(
        _compress_kernel,
        grid=(B, G, 2),
        in_specs=[
            pl.BlockSpec((None, None, None, n_sub, 1024), lambda b, g, s: (b, g, s, 0, 0)),
            pl.BlockSpec((None, 2, 1024), lambda b, g, s: (s, 0, 0)),
            pl.BlockSpec((None, 2, 1024, CMP_HIDDEN), lambda b, g, s: (s, 0, 0, 0)),
            pl.BlockSpec((None, 1, CMP_HIDDEN), lambda b, g, s: (s, 0, 0)),
            pl.BlockSpec((None, CMP_HIDDEN, DH), lambda b, g, s: (s, 0, 0)),
        ],
        out_specs=pl.BlockSpec((None, None, None, n_sub, DH), lambda b, g, s: (b, g, s, 0, 0)),
        out_shape=jax.ShapeDtypeStruct((B, G, 2, n_sub, DH), BF16),
        compiler_params=_cparams(("arbitrary", "arbitrary", "arbitrary")),
        name="nsa_compress",
    )(sub, pe, w1, b1, w2)


def _nsa_kernel(qt_ref, kc_ref, vct_ref, ksa_ref, vst_ref, kw_ref, vwt_ref, gate_ref, tbc_hi_ref, tbc_lo_ref,
                tbs_ref, tbw_ref, ssum_ref, o_ref, logit_sc, prob_sc):
    pair = pl.program_id(2)
    ncp = kc_ref.shape[0]
    fronts = _nsa_fronts([NSA_PAIR * pair + i for i in range(NSA_PAIR)], [qt_ref[i] for i in range(NSA_PAIR)],
                         kc_ref, vct_ref, kw_ref, vwt_ref, tbc_hi_ref, tbc_lo_ref, tbw_ref, ssum_ref, ncp)

    n_tiles = pair + 1
    chains = [(i, c * CHAIN_W, (c + 1) * CHAIN_W) for i in range(NSA_PAIR) for c in range(GQ // CHAIN_W)]
    qas = []
    for i, a, b in chains:
        qt, unsel = qt_ref[i], fronts[i][1]
        qa = jnp.concatenate([qt, jnp.zeros((HSEG - DH, GQ), BF16), jnp.concatenate([unsel] * NSA_HPG, axis=1)],
                             axis=0)
        qas.append(qa[:, a:b])

    def sel_qk(kt):
        start = pl.multiple_of(kt * SEL_TK, SEL_TK)
        return [jnp.dot(ksa_ref[pl.ds(start, SEL_TK), :], qc, preferred_element_type=F32)
                for qc in qas]

    def sel_vt(kt):
        return [vst_ref[kt]] * len(chains)

    def near_step(kt, logits, variants, carries):
        logits = [s + tbs_ref[variants[i], :, a:b] for s, (i, a, b) in zip(logits, chains)]
        return _online_softmax_steps(logits, sel_vt(kt), carries)

    carries = tuple(_softmax_carry(DH, CHAIN_W) for _ in chains)
    n_far = jnp.maximum(n_tiles - 2, 0)
    logits, carries = _pipelined_softmax_loop(n_far, sel_qk, sel_vt, carries, logit_sc, prob_sc)
    logits, carries = lax.cond(
        n_tiles >= 2,
        lambda lg, c: (tuple(sel_qk(n_tiles - 1)), near_step(n_tiles - 2, lg, (0, 3), c)),
        lambda lg, c: (lg, c), logits, carries)
    carries = near_step(n_tiles - 1, logits, (1, 2), carries)

    per_block = GQ // CHAIN_W
    for i in range(NSA_PAIR):
        o_c, _, o_w = fronts[i]
        o_s = jnp.concatenate([_softmax_result(c, DH) for c in carries[i * per_block:(i + 1) * per_block]], axis=1)
        gates = gate_ref[i]
        out = gates[0:1] * o_c + gates[1:2] * o_s + gates[2:3] * o_w
        for pr in range(NSA_HPG // 2):
            stacked = jnp.concatenate([out[:, (2 * pr + s) * QB:(2 * pr + s + 1) * QB] for s in range(2)], axis=0)
            o_ref[i * QB:(i + 1) * QB, pr * 2 * DH:(pr + 1) * 2 * DH] = stacked.T.astype(o_ref.dtype)


def _nsa_fronts(qbs, qts, kc_ref, vct_ref, kw_ref, vwt_ref, tbc_hi_ref, tbc_lo_ref, tbw_ref, ssum_ref, ncp):
    blocks = range(len(qbs))

    row = lax.broadcasted_iota(jnp.int32, (ncp, 2 * HSEG), 0)
    lane = lax.broadcasted_iota(jnp.int32, (ncp, 2 * HSEG), 1)
    col = jnp.where((lane >= DH) & (lane < DH + 2 * CMP_TAB), (lane - DH) & (CMP_TAB - 1), CMP_TAB)
    keys = kc_ref[...]
    p_cs, o_cs = [], []
    for i in blocks:
        rel = row - ((QB // CMP_STRIDE) * qbs[i] - 16)
        lhs = jnp.where(jnp.minimum(rel, CMP_BAND) == col, 1.0, keys)
        rhs = jnp.concatenate([qts[i], tbc_hi_ref[...], tbc_lo_ref[...], jnp.zeros((2 * HSEG - DH - 2 * CMP_TAB, GQ), BF16)],
                              axis=0)
        s = jnp.dot(lhs.astype(BF16), rhs, preferred_element_type=F32)
        m = jnp.max(s, axis=0, keepdims=True)
        e = jnp.exp2(s - m)
        p_cs.append(e * jnp.where(m > 0.5 * NEG, 1.0 / jnp.sum(e, axis=0, keepdims=True), 0.0))
    for i in blocks:
        o_cs.append(jnp.dot(vct_ref[...], p_cs[i].astype(BF16), preferred_element_type=F32))

    ssum = ssum_ref[...]
    jj = lax.broadcasted_iota(jnp.int32, (LANES, QB), 0)
    jf = jj.astype(F32)
    causal, scores, sels = [], [], []
    for i in blocks:
        imp = p_cs[i][:, 0:QB]
        for hh in range(1, NSA_HPG):
            imp = imp + p_cs[i][:, hh * QB:(hh + 1) * QB]
        p_slc = sum(jnp.dot(ssum, part, preferred_element_type=F32) for part in _split3(imp))
        blk_cur = (qbs[i] * QB + lax.broadcasted_iota(jnp.int32, (LANES, QB), 1)) // SLC_BLOCK
        causal_b = jj <= blk_cur
        forced = (jj == 0) | (causal_b & (jj > blk_cur - SLC_LOCAL))
        causal.append(causal_b)
        scores.append(jnp.where(causal_b & jnp.logical_not(forced), p_slc, NEG))
        sels.append(jnp.where(forced, 1.0, 0.0))
    for _ in range(SLC_TOPK - 1 - SLC_LOCAL):
        for i in blocks:
            mx = jnp.max(scores[i], axis=0, keepdims=True)
            first = jnp.min(jnp.where(scores[i] == mx, jf, float(LANES)), axis=0, keepdims=True)
            hit = jf == first
            sels[i] = jnp.where(hit, 1.0, sels[i])
            scores[i] = jnp.where(hit, REMOVED, scores[i])
    unsels = [jnp.where(causal[i] & (sels[i] > 0.0), 0.0, NEG).astype(BF16) for i in blocks]

    o_ws = []
    for i in blocks:
        first = jnp.maximum(qbs[i] - WINDOW // QB, 0)
        start = pl.multiple_of(first * QB, QB)
        shift = pl.multiple_of(start + WINDOW - qbs[i] * QB, QB)
        s = (jnp.dot(kw_ref[pl.ds(start, N_WIN), 0:DH], qts[i], preferred_element_type=F32)
             + tbw_ref[pl.ds(shift, N_WIN), :])
        e = jnp.exp2(s - jnp.max(s, axis=0, keepdims=True))
        inv = 1.0 / jnp.sum(e, axis=0, keepdims=True)
        e = e.astype(BF16)
        o_w = jnp.zeros((DH, GQ), F32)
        for r in range(N_WIN // QB):
            o_w = o_w + jnp.dot(vwt_ref[first + r], e[r * QB:(r + 1) * QB], preferred_element_type=F32)
        o_ws.append(o_w * inv)

    return [(o_cs[i], unsels[i], o_ws[i]) for i in blocks]


def _nsa(qt, kc, vct, ksa, vst, kw, vwt, gates, tbc_hi, tbc_lo, tbs, tbw, ssum):
    B, G, nqb, _, _ = qt.shape
    T = ksa.shape[1]
    bg = lambda a: pl.BlockSpec((None, None) + a.shape[2:], lambda b, g, i: (b, g) + (0,) * (a.ndim - 2))
    lane_g = lambda w: pl.BlockSpec((None, T, w), lambda b, g, i: (b, 0, g))
    per_g = lambda a: pl.BlockSpec((None,) + a.shape[1:], lambda b, g, i: (g,) + (0,) * (a.ndim - 1))
    per_blk = lambda a: pl.BlockSpec((None, None, NSA_PAIR) + a.shape[3:],
                                     lambda b, g, i: (b, g, i) + (0,) * (a.ndim - 3))
    n_chains = NSA_PAIR * GQ // CHAIN_W
    return pl.pallas_call(
        _nsa_kernel,
        grid=(B, G, nqb // NSA_PAIR),
        in_specs=[per_blk(qt), bg(kc), bg(vct), lane_g(2 * HSEG), bg(vst), lane_g(HSEG), bg(vwt), per_blk(gates),
                  per_g(tbc_hi), per_g(tbc_lo), per_g(tbs), per_g(tbw),
                  pl.BlockSpec(ssum.shape, lambda b, g, i: (0, 0))],
        out_specs=pl.BlockSpec((None, NSA_PAIR * QB, NSA_HPG * DH), lambda b, g, i: (b, i, g)),
        out_shape=jax.ShapeDtypeStruct((B, nqb * QB, G * NSA_HPG * DH), BF16),
        scratch_shapes=[pltpu.VMEM((2, n_chains, SEL_TK, CHAIN_W), F32),
                        pltpu.VMEM((2, n_chains, SEL_TK, CHAIN_W), BF16)],
        compiler_params=_cparams(("arbitrary", "arbitrary", "arbitrary")),
        name="nsa_attention",
    )(qt, kc, vct, ksa, vst, kw, vwt, gates, tbc_hi, tbc_lo, tbs, tbw, ssum)


def _merge_kernel(x_ref, oa_ref, ob_ref, oc_ref, ga_ref, wm_ref, bm_ref, wo_ref, wout_ref, gf_ref, wr_ref, br_ref,
                  xo_ref, h2_ref, comb_ref):
    x = x_ref[...]
    h = _rms(x, ga_ref[...]).astype(BF16)
    mixed = None
    for i, o_ref in enumerate((oa_ref, ob_ref, oc_ref)):
        gate = jax.nn.sigmoid(jnp.dot(h, wm_ref[:, i * D_MODEL:(i + 1) * D_MODEL], preferred_element_type=F32)
                              + bm_ref[:, i * D_MODEL:(i + 1) * D_MODEL])
        term = gate * jnp.dot(o_ref[...], wo_ref[i], preferred_element_type=F32)
        mixed = term if mixed is None else mixed + term
    xn = x + jnp.dot(mixed.astype(BF16), wout_ref[...], preferred_element_type=F32)
    xo_ref[...] = xn
    h2 = _rms(xn, gf_ref[...])
    h2_ref[...] = h2.astype(BF16)

    lane = lax.broadcasted_iota(jnp.int32, (x.shape[0], LANES), 1).astype(F32)
    logits = jnp.dot(h2, wr_ref[...], preferred_element_type=F32, precision=lax.Precision.HIGHEST) + br_ref[...]
    gl = jnp.where(lane >= float(N_EXPERTS), logits, NEG)
    ge = jnp.exp(gl - jnp.max(gl, axis=-1, keepdims=True))
    gp = ge / jnp.sum(ge, axis=-1, keepdims=True)
    grp_top = jnp.max(gp, axis=-1, keepdims=True)
    g_idx = jnp.min(jnp.where(gp == grp_top, lane, float(LANES)), axis=-1, keepdims=True) - float(N_EXPERTS)
    in_grp = jnp.floor(lane * (1.0 / EPG)) == g_idx
    el = jnp.where(in_grp, logits, NEG)
    ee = jnp.exp(el - jnp.max(el, axis=-1, keepdims=True))
    ep = jnp.where(in_grp, ee / jnp.sum(ee, axis=-1, keepdims=True), -1.0)
    p1 = jnp.max(ep, axis=-1, keepdims=True)
    i1 = jnp.min(jnp.where(ep == p1, lane, float(LANES)), axis=-1, keepdims=True)
    ep2 = jnp.where(lane == i1, -1.0, ep)
    p2 = jnp.max(ep2, axis=-1, keepdims=True)
    i2 = jnp.min(jnp.where(ep2 == p2, lane, float(LANES)), axis=-1, keepdims=True)
    den = p1 + p2
    comb_ref[...] = jnp.where(lane == i1, grp_top * p1 / den, 0.0) + jnp.where(lane == i2, grp_top * p2 / den, 0.0)


def _merge(x, oa, ob, oc, ga, wm, bm, wo, wout, gf, wr, br, tm):
    N, D = x.shape
    row = lambda w: pl.BlockSpec((tm, w), lambda i: (i, 0))
    full = lambda a: pl.BlockSpec(a.shape, lambda i: (0,) * a.ndim)
    return pl.pallas_call(
        _merge_kernel,
        grid=(N // tm,),
        in_specs=[row(D), row(512), row(512), row(512), full(ga), full(wm), full(bm), full(wo), full(wout), full(gf),
                  full(wr), full(br)],
        out_specs=[row(D), row(D), row(LANES)],
        out_shape=[jax.ShapeDtypeStruct((N, D), F32), jax.ShapeDtypeStruct((N, D), BF16),
                   jax.ShapeDtypeStruct((N, LANES), F32)],
        compiler_params=_cparams(("arbitrary",)),
        name="merge_router",
    )(x, oa, ob, oc, ga, wm, bm, wo, wout, gf, wr, br)


def _moe_kernel(x_ref, h_ref, comb_ref, wg_ref, wu_ref, wd_ref, gfin_ref, o_ref, acc_sc, *, final_norm):
    e = pl.program_id(1)

    @pl.when(e == 0)
    def _():
        acc_sc[...] = jnp.zeros_like(acc_sc)

    h = h_ref[...]
    a = jnp.dot(h, wg_ref[...], preferred_element_type=F32)
    u = jnp.dot(h, wu_ref[...], preferred_element_type=F32)
    c = _lane_pick(comb_ref[...], e)
    hid = (jax.nn.silu(a) * u * c).astype(BF16)
    acc_sc[...] += jnp.dot(hid, wd_ref[...], preferred_element_type=F32)

    @pl.when(e == pl.num_programs(1) - 1)
    def _():
        y = x_ref[...] + acc_sc[...]
        o_ref[...] = _rms(y, gfin_ref[...]) if final_norm else y


def _moe(x, h2, comb, wg, wu, wd, gfin, tm, final_norm):
    N, D = x.shape
    row = lambda w: pl.BlockSpec((tm, w), lambda i, e: (i, 0))
    return pl.pallas_call(
        functools.partial(_moe_kernel, final_norm=final_norm),
        grid=(N // tm, N_EXPERTS),
        in_specs=[row(D), row(D), row(LANES),
                  pl.BlockSpec((None, D, EXPERT_DIM), lambda i, e: (e, 0, 0)),
                  pl.BlockSpec((None, D, EXPERT_DIM), lambda i, e: (e, 0, 0)),
                  pl.BlockSpec((None, EXPERT_DIM, D), lambda i, e: (e, 0, 0)),
                  pl.BlockSpec((1, D), lambda i, e: (0, 0))],
        out_specs=row(D),
        out_shape=jax.ShapeDtypeStruct((N, D), F32),
        scratch_shapes=[pltpu.VMEM((tm, D), F32)],
        compiler_params=_cparams(("arbitrary", "arbitrary")),
        name="moe_experts",
    )(x, h2, comb, wg, wu, wd, gfin)


def _pad_cols(w, n):
    return jnp.pad(w, ((0, 0), (0, n - w.shape[1])))


def _pack_w1(w_in_l):
    o = np.cumsum([0, 512, 768, 24, MLA_Q_RANK, MLA_KV_RANK, MLA_ROPE, 1536, FOX_HEADS])
    nq, nkv, ng, cq, ckv, kr, fox, ff = (w_in_l[:, o[i]:o[i + 1]] for i in range(8))
    d_in = w_in_l.shape[0]
    half = MLA_ROPE // 2
    nkv = nkv.reshape(d_in, 3, 2, NSA_G, DH)
    segs = lambda w: _pad_feat(w, HSEG).reshape(d_in, -1)
    z = lambda n: jnp.zeros((d_in, n), w_in_l.dtype)
    x1, x2 = kr[:, :half], kr[:, half:]
    parts = {
        "NQ": nq, "NCMP": nkv[:, 0].reshape(d_in, -1), "KS": segs(nkv[:, 1, 0]), "VS": nkv[:, 1, 1].reshape(d_in, -1),
        "KW": segs(nkv[:, 2, 0]), "VW": nkv[:, 2, 1].reshape(d_in, -1), "NG": _pad_cols(ng, LANES), "CQ": cq, "CKV": ckv,
        "KRA": jnp.concatenate([z(MLA_NOPE), x1, x2, z(HSEG - MLA_QK)], axis=1),
        "KRB": jnp.concatenate([z(MLA_NOPE), x2, x1, z(HSEG - MLA_QK)], axis=1),
        "FQ": fox[:, :512], "FK": segs(fox[:, 512:1024].reshape(d_in, FOX_HEADS, DH)), "FV": fox[:, 1024:],
        "FF": _pad_cols(ff, LANES),
    }
    w1 = jnp.concatenate([parts[name] for name, _ in _SEG_WIDTHS], axis=1)
    return w1.astype(BF16), w_in_l[:, o[8]:]


def _pack_wuq(w_uq_l):
    w = w_uq_l.reshape(MLA_Q_RANK, MLA_HEADS, MLA_QK)
    half = MLA_ROPE // 2
    nope, r1, r2 = w[:, :, :MLA_NOPE], w[:, :, MLA_NOPE:MLA_NOPE + half], w[:, :, MLA_NOPE + half:]
    direct = _pad_feat(jnp.concatenate([nope, r1, r2], axis=-1), HSEG).reshape(MLA_Q_RANK, -1)
    swapped = _pad_feat(jnp.concatenate([jnp.zeros_like(nope), r2, r1], axis=-1), HSEG).reshape(MLA_Q_RANK, -1)
    return jnp.concatenate([direct, swapped], axis=1).astype(BF16)


def _rotary_tables():
    half = MLA_ROPE // 2
    inv_freq = ROPE_THETA ** (-jnp.arange(half, dtype=F32) / half)
    zeros = lambda n: jnp.zeros((n,), F32)
    freq = jnp.concatenate([zeros(MLA_NOPE), inv_freq, inv_freq, zeros(HSEG - MLA_QK)])
    sign = jnp.concatenate([zeros(MLA_NOPE), -jnp.ones((half,), F32), jnp.ones((half,), F32), zeros(HSEG - MLA_QK)])
    return jnp.stack([freq, sign])


def _pad_feat(a, n):
    return jnp.pad(a, ((0, 0),) * (a.ndim - 1) + ((0, n - a.shape[-1]),))


def _ssum_matrix(ncp):
    n = np.arange(ncp)[None, :]
    j = np.arange(LANES)[:, None]
    d = n - 4 * j
    m = np.where((d == 0) | (d == 4), 1.0, np.where((d >= 1) & (d <= 3), 2.0, 0.0))
    return jnp.asarray(m, BF16)


def kernel(x, positions, rel_bias, norm_attn, w_in, b_merge, cmp_pe, cmp_k_w1, cmp_k_b1, cmp_k_w2, cmp_v_w1, cmp_v_b1,
           cmp_v_w2, mla_q_norm, mla_kv_norm, mla_w_uq, mla_w_uk, mla_w_uv, fox_b_f, w_o_nsa, w_o_mla, w_o_fox, w_out,
           norm_ffn, moe_w_grp, moe_b_grp, moe_w_router, moe_b_router, moe_w_gate, moe_w_up, moe_w_down, norm_final):
    B, T, D = x.shape
    N = B * T
    n_sub = T // CMP_STRIDE
    nqb = T // QB
    tm_in = min(512, T)
    fl_tq, fl_hb = min(512, T), 2
    tm_merge = 512
    tm_moe = 1024

    tbc, tbs, tbw = _bias_tables(rel_bias)
    tbc = tbc.reshape(2, CMP_TAB, NSA_G, GQ).transpose(0, 2, 1, 3)
    tbc_hi, tbc_lo = tbc[0], tbc[1]
    tbs = tbs.reshape(4, SEL_TK, NSA_G, GQ).transpose(2, 0, 1, 3)
    tbw = tbw.reshape(WIN_TAB, NSA_G, GQ).transpose(1, 0, 2)
    ssum = _ssum_matrix(n_sub)
    rot = _rotary_tables()
    cum_place = np.zeros((3, LANES, FOX_HEADS * HSEG), np.float32)
    for n in range(3):
        cum_place[n, np.arange(FOX_HEADS), np.arange(FOX_HEADS) * HSEG + DH + n] = -1.0
    cum_place = jnp.asarray(cum_place, BF16)
    tri =jnp.asarray(np.tril(np.ones((tm_in, tm_in), np.float32)), BF16)
    pos3 = positions.reshape(B, T, 1)

    for l in range(DEPTH):
        w1, w_merge = _pack_w1(w_in[l])
        fb = _pad_cols(fox_b_f[l].reshape(1, -1), LANES)
        wuk = _pad_feat(mla_w_uk[l].reshape(MLA_KV_RANK, MLA_HEADS, MLA_NOPE), HSEG).reshape(MLA_KV_RANK, -1)
        qt_a, ncmp, ksa, vs_t, kw, vw_t, ng, mq_t, mk, mv_t, fq_t, fk, fv_t = _inproj(
            x, pos3, norm_attn[l].reshape(1, D), w1, _pack_wuq(mla_w_uq[l]), wuk.astype(BF16),
            mla_w_uv[l].astype(BF16), mla_q_norm[l].reshape(1, -1), mla_kv_norm[l].reshape(1, -1), fb, rot, tri,
            cum_place, tm_in)

        sub = ncmp.reshape(B, T, 2, NSA_G, DH).transpose(0, 3, 2, 1, 4).reshape(B, NSA_G, 2, n_sub, CMP_STRIDE * DH)
        pe = jnp.broadcast_to(cmp_pe[l].reshape(1, 2, CMP_STRIDE * DH), (2, 2, CMP_STRIDE * DH))
        cw1 = jnp.stack([cmp_k_w1[l], cmp_v_w1[l]]).reshape(2, 2, CMP_STRIDE * DH, CMP_HIDDEN).astype(BF16)
        cb1 = jnp.stack([cmp_k_b1[l], cmp_v_b1[l]]).reshape(2, 1, CMP_HIDDEN)
        cw2 = jnp.stack([cmp_k_w2[l], cmp_v_w2[l]]).astype(BF16)
        cmp = _compress(sub, pe, cw1, cb1, cw2)
        gates_t = ng[..., :3 * NSA_HEADS].reshape(B, nqb, QB, NSA_G, NSA_HPG, 3).transpose(0, 3, 1, 5, 4, 2)
        gates_t = jnp.pad(gates_t.reshape(B, NSA_G, nqb, 3, GQ), ((0, 0), (0, 0), (0, 0), (0, 5), (0, 0)))
        o_a = _nsa(qt_a, _pad_feat(cmp[:, :, 0], 2 * HSEG), cmp[:, :, 1].transpose(0, 1, 3, 2), ksa,
                   vs_t, kw, vw_t,
                   gates_t, tbc_hi, tbc_lo, tbs, tbw, ssum)
        out_a = o_a.reshape(N, NSA_HEADS * DH)

        o_b = _flash(mq_t, mk, mv_t, tq=fl_tq, hb=fl_hb)
        out_b = o_b.reshape(N, -1)

        o_c = _flash(fq_t, fk, fv_t, tq=fl_tq, hb=fl_hb)
        out_c = o_c.reshape(N, -1)

        wo = jnp.stack([w_o_nsa[l], w_o_mla[l], w_o_fox[l]]).astype(BF16)
        w_route = _pad_cols(jnp.concatenate([moe_w_router[l], moe_w_grp[l]], axis=1), LANES)
        b_route = jnp.concatenate([moe_b_router[l], moe_b_grp[l],
                                   jnp.full((LANES - N_EXPERTS - N_GROUPS,), NEG, F32)]).reshape(1, LANES)
        x2, h2, comb = _merge(
            x.reshape(N, D), out_a, out_b, out_c, norm_attn[l].reshape(1, D), w_merge.astype(BF16),
            b_merge[l].reshape(1, -1), wo, w_out[l].astype(BF16), norm_ffn[l].reshape(1, D), w_route, b_route, tm_merge)

        x = _moe(x2, h2, comb, moe_w_gate[l].astype(BF16), moe_w_up[l].astype(BF16), moe_w_down[l].astype(BF16),
                 norm_final.reshape(1, D), tm_moe, final_norm=(l == DEPTH - 1)).reshape(B, T, D)
    return x
```

```python
import functools
import math

import numpy as np
import jax
import jax.numpy as jnp
from jax import lax
from jax.experimental import pallas as pl
from jax.experimental.pallas import tpu as pltpu

F32 = jnp.float32
BF16 = jnp.bfloat16

D_MODEL = 1024
DEPTH = 2
EPS = 1e-6
NEG = -1e30
REMOVED = -3e38

NSA_HEADS = 8
NSA_G = 2
NSA_HPG = 4
DH = 64
CMP_STRIDE = 16
CMP_HIDDEN = 128
SLC_BLOCK = 64
SLC_TOPK = 16
SLC_LOCAL = 2
WINDOW = 512
QB = 128
GQ = NSA_HPG * QB
N_WIN = WINDOW + QB
WIN_TAB = WINDOW + N_WIN
SEL_TK = 256
NSA_PAIR = SEL_TK // QB
CHAIN_W = 256
SUM_ROWS = 16
CMP_BAND = 32
CMP_TAB = 64
MLA_HEADS = 8
MLA_Q_RANK = 256
MLA_KV_RANK = 128
MLA_NOPE = 64
MLA_ROPE = 32
MLA_QK = 96
ROPE_THETA = 10000.0
FOX_HEADS = 8
REL_BUCKETS = 32
REL_MAX_DIST = 128
N_GROUPS = 4
EPG = 8
N_EXPERTS = 32
EXPERT_DIM = 256

LANES = 128
LOG2E = 1.4426950408889634
VMEM_LIMIT = 56 * 1024 * 1024


def _bucket_thresholds():
    n = np.arange(0, 4 * REL_MAX_DIST)
    max_exact = REL_BUCKETS // 2
    lr = np.log(np.maximum(n, 1).astype(np.float32) / np.float32(max_exact)) / np.float32(math.log(REL_MAX_DIST / max_exact))
    large = max_exact + (lr.astype(np.float32) * np.float32(REL_BUCKETS - max_exact)).astype(np.int32)
    bucket = np.where(n < max_exact, n, np.minimum(large, REL_BUCKETS - 1))
    return [int(np.argmax(bucket >= b)) for b in range(1, REL_BUCKETS)]


BUCKET_THR = _bucket_thresholds()
assert BUCKET_THR[-1] <= REL_MAX_DIST


def _cparams(sem):
    return pltpu.CompilerParams(dimension_semantics=sem, vmem_limit_bytes=VMEM_LIMIT)


def _split3(x):
    hi = x.astype(BF16)
    r = x - hi.astype(F32)
    mid = r.astype(BF16)
    lo = (r - mid.astype(F32)).astype(BF16)
    return hi, mid, lo


def _lane_pick(x, idx):
    lane = lax.broadcasted_iota(jnp.int32, x.shape, x.ndim - 1)
    return jnp.sum(jnp.where(lane == idx, x, 0.0), axis=-1, keepdims=True)


def _online_softmax_steps(logits, vts, carries):
    stats = []
    for s, (m, _) in zip(logits, carries):
        m_new = jnp.maximum(m, jnp.max(s, axis=0, keepdims=True))
        stats.append((m_new, jnp.exp2(m - m_new), jnp.exp2(s - m_new).astype(BF16)))
    out = []
    for (m_new, alpha, p), vt, (_, acc) in zip(stats, vts, carries):
        out.append((m_new, alpha * acc + jnp.dot(vt, p, preferred_element_type=F32)))
    return tuple(out)


def _softmax_carry(dv, width):
    return jnp.full((1, width), NEG, F32), jnp.zeros((dv + SUM_ROWS, width), F32)


def _softmax_result(carry, dv):
    _, acc = carry
    return acc[0:dv] * (1.0 / acc[dv:dv + 1])


def _pipelined_softmax_loop(n_steps, qk_fn, vt_fn, carries, logit_sc, prob_sc):
    n_chains = len(carries)

    def value_products(kt, slot):
        return [jnp.dot(vt, prob_sc[slot, i], preferred_element_type=F32) for i, vt in enumerate(vt_fn(kt))]

    def keep(logits, slot):
        for i, s in enumerate(logits):
            logit_sc[slot, i] = s
        return tuple(jnp.max(s, axis=0, keepdims=True) for s in logits)

    def stage(kt, cur, nxt, state):
        carries, tile_max = state
        upcoming = qk_fn(kt + 1)
        pvs = value_products(jnp.maximum(kt - 1, 0), nxt)
        out = []
        for i, (pv, tm, (m, acc)) in enumerate(zip(pvs, tile_max, carries)):
            m_new = jnp.maximum(m, tm)
            prob_sc[cur, i] = jnp.exp2(logit_sc[cur, i] - m_new).astype(BF16)
            out.append((m_new, jnp.exp2(m - m_new) * (acc + pv)))
        return tuple(out), keep(upcoming, nxt)

    prob_sc[1] = jnp.zeros(prob_sc.shape[1:], BF16)
    state = (tuple(carries), keep(qk_fn(0), 0))
    state = lax.fori_loop(0, n_steps // 2, lambda i, c: stage(2 * i + 1, 1, 0, stage(2 * i, 0, 1, c)), state)
    odd = n_steps % 2
    carries, _ = lax.cond(odd == 1, lambda c: stage(n_steps - 1, 0, 1, c), lambda c: c, state)
    pvs = value_products(jnp.maximum(n_steps - 1, 0), 1 - odd)
    logits = tuple(logit_sc[odd, i] for i in range(n_chains))
    return logits, tuple((m, acc + pv) for (m, acc), pv in zip(carries, pvs))


def _bias_table_kernel(tab_ref, tbc_ref, tbs_ref, tbw_ref):
    h = pl.program_id(0)
    far = tab_ref[REL_BUCKETS - 1, h]

    def delta(dist):
        val = jnp.full(dist.shape, tab_ref[0, h], F32)
        for b in range(1, REL_BUCKETS):
            val = jnp.where(dist >= BUCKET_THR[b - 1], tab_ref[b, h], val)
        return (val - far) * LOG2E

    j = lax.broadcasted_iota(jnp.int32, (CMP_TAB, QB), 0)
    q = lax.broadcasted_iota(jnp.int32, (CMP_TAB, QB), 1)
    dist = q - CMP_STRIDE * j + (2 * QB - 2 * CMP_STRIDE + 1)
    band = jnp.where(j < CMP_BAND, jnp.where(dist >= 0, delta(dist), NEG), jnp.where(j == CMP_BAND, NEG, 0.0))
    band_hi = band.astype(BF16)
    tbc_ref[0] = band_hi
    tbc_ref[1] = (band - band_hi.astype(F32)).astype(BF16)
    c = lax.broadcasted_iota(jnp.int32, (SEL_TK, QB), 0)
    q = lax.broadcasted_iota(jnp.int32, (SEL_TK, QB), 1)
    for var, off in enumerate((SEL_TK, 0, QB)):
        dist = q + off - c
        tbs_ref[var] = jnp.where(dist >= 0, delta(dist), NEG)
    tbs_ref[3] = jnp.zeros((SEL_TK, QB), F32)
    c = lax.broadcasted_iota(jnp.int32, (WIN_TAB, QB), 0)
    q = lax.broadcasted_iota(jnp.int32, (WIN_TAB, QB), 1)
    dist = q + WINDOW - c
    tbw_ref[...] = jnp.where((dist >= 0) & (dist < WINDOW), delta(dist), NEG)


def _bias_tables(rel_bias):
    hq = NSA_HEADS * QB
    return pl.pallas_call(
        _bias_table_kernel,
        grid=(NSA_HEADS,),
        in_specs=[pl.BlockSpec(memory_space=pltpu.SMEM)],
        out_specs=[
            pl.BlockSpec((2, CMP_TAB, QB), lambda h: (0, 0, h)),
            pl.BlockSpec((4, SEL_TK, QB), lambda h: (0, 0, h)),
            pl.BlockSpec((WIN_TAB, QB), lambda h: (0, h)),
        ],
        out_shape=[
            jax.ShapeDtypeStruct((2, CMP_TAB, hq), BF16),
            jax.ShapeDtypeStruct((4, SEL_TK, hq), F32),
            jax.ShapeDtypeStruct((WIN_TAB, hq), F32),
        ],
        compiler_params=_cparams(("arbitrary",)),
        name="bias_tables",
    )(rel_bias)


HSEG = 128
_SEG_WIDTHS = (("NQ", 512), ("NCMP", 256), ("KS", NSA_G * HSEG), ("VS", 128), ("KW", NSA_G * HSEG), ("VW", 128),
               ("NG", 128), ("CQ", MLA_Q_RANK), ("CKV", MLA_KV_RANK), ("KRA", HSEG), ("KRB", HSEG), ("FQ", 512),
               ("FK", FOX_HEADS * HSEG), ("FV", 512), ("FF", 128))
SEG = {}
_off = 0
for _name, _w in _SEG_WIDTHS:
    SEG[_name] = (_off, _off + _w)
    _off += _w


def _rms(x, g):
    return x * lax.rsqrt(jnp.mean(x * x, axis=-1, keepdims=True) + EPS) * g


def _inproj_kernel(x_ref, pos_ref, g_ref, w1_ref, wuq_ref, wuk_ref, wuv_ref, qn_ref, kvn_ref, fb_ref, rot_ref,
                   tri_ref, place_ref, nq_ref, ncmp_ref, ksa_ref, vs_ref, kw_ref, vw_ref, ng_ref, mq_ref, mk_ref, mv_ref,
                   fq_ref, fk_ref, fv_ref, carry_ref):
    i = pl.program_id(1)
    tm = x_ref.shape[0]
    h = _rms(x_ref[...], g_ref[...]).astype(BF16)

    def seg(name):
        a, b = SEG[name]
        return jnp.dot(h, w1_ref[:, a:b], preferred_element_type=F32)

    nq = seg("NQ") * (DH ** -0.5 * LOG2E)
    for pair in range(NSA_HEADS // 2):
        pair_t = nq[:, pair * HSEG:(pair + 1) * HSEG].T.astype(BF16)
        for sub in range(2):
            g, hh = divmod(2 * pair + sub, NSA_HPG)
            for jb in range(tm // QB):
                nq_ref[g, jb, :, hh * QB:(hh + 1) * QB] = pair_t[sub * DH:(sub + 1) * DH, jb * QB:(jb + 1) * QB]
    ncmp_ref[...] = seg("NCMP").astype(BF16)
    def sum_rows(width):
        return jnp.where(lax.broadcasted_iota(jnp.int32, (SUM_ROWS, width), 0) == 0, 1.0, 0.0).astype(BF16)

    def put_value_tiles(ref, values, tile, with_sum):
        for pair in range(values.shape[1] // HSEG):
            pair_t = values[:, pair * HSEG:(pair + 1) * HSEG].astype(BF16).T
            for sub in range(2):
                for t in range(tm // tile):
                    ref[2 * pair + sub, t, 0:DH, :] = pair_t[sub * DH:(sub + 1) * DH, t * tile:(t + 1) * tile]
                    if with_sum:
                        ref[2 * pair + sub, t, DH:DH + SUM_ROWS, :] = sum_rows(tile)

    put_value_tiles(vs_ref, seg("VS"), SEL_TK, True)
    kw_ref[...] = seg("KW").astype(BF16)
    put_value_tiles(vw_ref, seg("VW"), QB, False)
    ng_ref[...] = jax.nn.sigmoid(seg("NG"))
    ks = seg("KS").astype(BF16)
    tok = i * tm + lax.broadcasted_iota(jnp.int32, (tm, LANES), 0)
    blk = (tok // SLC_BLOCK == lax.broadcasted_iota(jnp.int32, (tm, LANES), 1))
    blk = jnp.where(blk, 1.0, 0.0).astype(BF16)
    for g in range(NSA_G):
        ksa_ref[:, 2 * g * HSEG:(2 * g + 1) * HSEG] = ks[:, g * HSEG:(g + 1) * HSEG]
        ksa_ref[:, (2 * g + 1) * HSEG:(2 * g + 2) * HSEG] = blk

    ang = pos_ref[...].astype(F32) * rot_ref[0:1, :]
    cos = jnp.cos(ang)
    sin = jnp.sin(ang) * rot_ref[1:2, :]
    cos_h = jnp.concatenate([cos] * MLA_HEADS, axis=1)
    sin_h = jnp.concatenate([sin] * MLA_HEADS, axis=1)
    width = MLA_HEADS * HSEG
    c_q = _rms(seg("CQ"), qn_ref[...]).astype(BF16)
    q_a = jnp.dot(c_q, wuq_ref[:, 0:width], preferred_element_type=F32)
    q_b = jnp.dot(c_q, wuq_ref[:, width:2 * width], preferred_element_type=F32)
    mq = (q_a * cos_h + q_b * sin_h) * (MLA_QK ** -0.5 * LOG2E)
    for hd in range(MLA_HEADS):
        mq_ref[hd] = mq[:, hd * HSEG:(hd + 1) * HSEG].T.astype(BF16)
    c_kv = _rms(seg("CKV"), kvn_ref[...]).astype(BF16)
    k_rope = seg("KRA") * cos + seg("KRB") * sin
    k_nope = jnp.dot(c_kv, wuk_ref[...], preferred_element_type=F32)
    mk_ref[...] = (k_nope + jnp.concatenate([k_rope] * MLA_HEADS, axis=1)).astype(BF16)
    put_value_tiles(mv_ref, jnp.dot(c_kv, wuv_ref[...], preferred_element_type=F32), tm, True)

    fq = seg("FQ") * (DH ** -0.5 * LOG2E)
    ones_rows = jnp.where(lax.broadcasted_iota(jnp.int32, (HSEG - DH, tm), 0) < 3, 1.0, 0.0).astype(BF16)
    for pair in range(FOX_HEADS // 2):
        pair_t = fq[:, pair * HSEG:(pair + 1) * HSEG].T.astype(BF16)
        for sub in range(2):
            fq_ref[2 * pair + sub, 0:DH, :] = pair_t[sub * DH:(sub + 1) * DH]
            fq_ref[2 * pair + sub, DH:HSEG, :] = ones_rows
    put_value_tiles(fv_ref, seg("FV"), tm, True)
    log_f = jax.nn.log_sigmoid(seg("FF") + fb_ref[...])

    @pl.when(i == 0)
    def _():
        carry_ref[...] = jnp.zeros_like(carry_ref)

    tri = tri_ref[...]
    cum = carry_ref[...]
    for part in _split3(log_f):
        cum = cum + jnp.dot(tri, part, preferred_element_type=F32)
    carry_ref[...] = cum[tm - 1:tm, :]
    fk = seg("FK")
    for n, part in enumerate(_split3(cum * LOG2E)):
        fk = fk + jnp.dot(part, place_ref[n], preferred_element_type=F32)
    fk_ref[...] = fk.astype(BF16)


def _inproj(x, pos, g, w1, wuq, wuk, wuv, qn, kvn, fb, rot, tri, place, tm):
    B, T, D = x.shape
    row = lambda w: pl.BlockSpec((None, tm, w), lambda b, i: (b, i, 0))
    full = lambda a: pl.BlockSpec(a.shape, lambda b, i: (0,) * a.ndim)
    heads_t = lambda h: pl.BlockSpec((None, h, HSEG, tm), lambda b, i: (b, 0, 0, i))
    tok = lambda w, dt=BF16: (row(w), jax.ShapeDtypeStruct((B, T, w), dt))
    q_t = lambda h: (heads_t(h), jax.ShapeDtypeStruct((B, h, HSEG, T), BF16))
    v_t = lambda h, tile, extra: (
        pl.BlockSpec((None, h, tm // tile, DH + extra, tile), lambda b, i: (b, 0, i, 0, 0)),
        jax.ShapeDtypeStruct((B, h, T // tile, DH + extra, tile), BF16))
    outs = [
        (pl.BlockSpec((None, NSA_G, tm // QB, DH, GQ), lambda b, i: (b, 0, i, 0, 0)),
         jax.ShapeDtypeStruct((B, NSA_G, T // QB, DH, GQ), BF16)),
        tok(256), tok(2 * NSA_G * HSEG), v_t(NSA_G, SEL_TK, SUM_ROWS), tok(NSA_G * HSEG), v_t(NSA_G, QB, 0),
        tok(128, F32),
        q_t(MLA_HEADS), tok(MLA_HEADS * HSEG), v_t(MLA_HEADS, tm, SUM_ROWS),
        q_t(FOX_HEADS), tok(FOX_HEADS * HSEG), v_t(FOX_HEADS, tm, SUM_ROWS),
    ]
    return pl.pallas_call(
        _inproj_kernel,
        grid=(B, T // tm),
        in_specs=[row(D), row(1), full(g), full(w1), full(wuq), full(wuk), full(wuv), full(qn), full(kvn), full(fb),
                  full(rot), full(tri), full(place)],
        out_specs=[spec for spec, _ in outs],
        out_shape=[shape for _, shape in outs],
        scratch_shapes=[pltpu.VMEM((1, LANES), F32)],
        compiler_params=_cparams(("arbitrary", "arbitrary")),
        name="inproj",
    )(x, pos, g, w1, wuq, wuk, wuv, qn, kvn, fb, rot, tri, place)


def _flash_kernel(qt_ref, k_ref, vt_ref, o_ref, logit_sc, prob_sc, *, tq, hb):
    qi = pl.program_id(2)
    dv = vt_ref.shape[2] - SUM_ROWS
    chains = [(j, c) for j in range(hb) for c in range(tq // CHAIN_W)]
    qts = [qt_ref[j, :, c * CHAIN_W:(c + 1) * CHAIN_W] for j, c in chains]

    def qk(kt):
        st = pl.multiple_of(kt * tq, tq)
        return [jnp.dot(k_ref[pl.ds(st, tq), j * HSEG:(j + 1) * HSEG], qt, preferred_element_type=F32)
                for (j, _), qt in zip(chains, qts)]

    init = tuple(_softmax_carry(dv, CHAIN_W) for _ in chains)
    diag, carries = _pipelined_softmax_loop(qi, qk, lambda kt: [vt_ref[j, kt] for j, _ in chains], init,
                                            logit_sc, prob_sc)
    kpos = lax.broadcasted_iota(jnp.int32, (tq, CHAIN_W), 0)
    qpos = lax.broadcasted_iota(jnp.int32, (tq, CHAIN_W), 1)
    logits = [jnp.where(kpos <= qpos + c * CHAIN_W, s, NEG) for (_, c), s in zip(chains, diag)]
    final = _online_softmax_steps(logits, [vt_ref[j, qi] for j, _ in chains], carries)
    results = {jc: _softmax_result(carry, dv) for jc, carry in zip(chains, final)}
    for c in range(tq // CHAIN_W):
        stacked = jnp.concatenate([results[(j, c)] for j in range(hb)], axis=0)
        o_ref[c * CHAIN_W:(c + 1) * CHAIN_W, :] = stacked.T.astype(o_ref.dtype)


def _flash(qt, k, vt, *, tq, hb):
    B, H, dk, T = qt.shape
    dv = vt.shape[3] - SUM_ROWS
    return pl.pallas_call(
        functools.partial(_flash_kernel, tq=tq, hb=hb),
        grid=(B, H // hb, T // tq),
        in_specs=[
            pl.BlockSpec((None, hb, dk, tq), lambda b, h, i: (b, h, 0, i)),
            pl.BlockSpec((None, T, hb * HSEG), lambda b, h, i: (b, 0, h)),
            pl.BlockSpec((None, hb, T // tq, dv + SUM_ROWS, tq), lambda b, h, i: (b, h, 0, 0, 0)),
        ],
        out_specs=pl.BlockSpec((None, tq, hb * dv), lambda b, h, i: (b, i, h)),
        out_shape=jax.ShapeDtypeStruct((B, T, H * dv), BF16),
        scratch_shapes=[pltpu.VMEM((2, hb * tq // CHAIN_W, tq, CHAIN_W), F32),
                        pltpu.VMEM((2, hb * tq // CHAIN_W, tq, CHAIN_W), BF16)],
        compiler_params=_cparams(("arbitrary", "arbitrary", "arbitrary")),
        name="flash_attention",
    )(qt, k, vt)


def _compress_kernel(sub_ref, pe_ref, w1_ref, b1_ref, w2_ref, o_ref):
    sub = sub_ref[...].astype(F32)
    n_sub = sub.shape[0]
    a = jnp.dot((sub + pe_ref[0:1, :]).astype(BF16), w1_ref[0], preferred_element_type=F32)
    b = jnp.dot((sub + pe_ref[1:2, :]).astype(BF16), w1_ref[1], preferred_element_type=F32)
    hid = jax.nn.gelu(a + pltpu.roll(b, n_sub - 1, 0) + b1_ref[...])
    o_ref[...] = jnp.dot(hid.astype(BF16), w2_ref[...], preferred_element_type=F32).astype(o_ref.dtype)


def _compress(sub, pe, w1, b1, w2):
    B, G, _, n_sub, _ = sub.shape
    return pl.pallas_call(
        _compress_kernel,
        grid=(B, G, 2),
        in_specs=[
            pl.BlockSpec((None, None, None, n_sub, 1024), lambda b, g, s: (b, g, s, 0, 0)),
            pl.BlockSpec((None, 2, 1024), lambda b, g, s: (s, 0, 0)),
            pl.BlockSpec((None, 2, 1024, CMP_HIDDEN), lambda b, g, s: (s, 0, 0, 0)),
            pl.BlockSpec((None, 1, CMP_HIDDEN), lambda b, g, s: (s, 0, 0)),
            pl.BlockSpec((None, CMP_HIDDEN, DH), lambda b, g, s: (s, 0, 0)),
        ],
        out_specs=pl.BlockSpec((None, None, None, n_sub, DH), lambda b, g, s: (b, g, s, 0, 0)),
        out_shape=jax.ShapeDtypeStruct((B, G, 2, n_sub, DH), BF16),
        compiler_params=_cparams(("arbitrary", "arbitrary", "arbitrary")),
        name="nsa_compress",
    )(sub, pe, w1, b1, w2)


def _nsa_kernel(qt_ref, kc_ref, vct_ref, ksa_ref, vst_ref, kw_ref, vwt_ref, gate_ref, tbc_hi_ref, tbc_lo_ref,
                tbs_ref, tbw_ref, ssum_ref, o_ref, logit_sc, prob_sc):
    pair = pl.program_id(2)
    ncp = kc_ref.shape[0]
    fronts = _nsa_fronts([NSA_PAIR * pair + i for i in range(NSA_PAIR)], [qt_ref[i] for i in range(NSA_PAIR)],
                         kc_ref, vct_ref, kw_ref, vwt_ref, tbc_hi_ref, tbc_lo_ref, tbw_ref, ssum_ref, ncp)

    n_tiles = pair + 1
    chains = [(i, c * CHAIN_W, (c + 1) * CHAIN_W) for i in range(NSA_PAIR) for c in range(GQ // CHAIN_W)]
    qas = []
    for i, a, b in chains:
        qt, unsel = qt_ref[i], fronts[i][1]
        qa = jnp.concatenate([qt, jnp.zeros((HSEG - DH, GQ), BF16), jnp.concatenate([unsel] * NSA_HPG, axis=1)],
                             axis=0)
        qas.append(qa[:, a:b])

    def sel_qk(kt):
        start = pl.multiple_of(kt * SEL_TK, SEL_TK)
        return [jnp.dot(ksa_ref[pl.ds(start, SEL_TK), :], qc, preferred_element_type=F32)
                for qc in qas]

    def sel_vt(kt):
        return [vst_ref[kt]] * len(chains)

    def near_step(kt, logits, variants, carries):
        logits = [s + tbs_ref[variants[i], :, a:b] for s, (i, a, b) in zip(logits, chains)]
        return _online_softmax_steps(logits, sel_vt(kt), carries)

    carries = tuple(_softmax_carry(DH, CHAIN_W) for _ in chains)
    n_far = jnp.maximum(n_tiles - 2, 0)
    logits, carries = _pipelined_softmax_loop(n_far, sel_qk, sel_vt, carries, logit_sc, prob_sc)
    logits, carries = lax.cond(
        n_tiles >= 2,
        lambda lg, c: (tuple(sel_qk(n_tiles - 1)), near_step(n_tiles - 2, lg, (0, 3), c)),
        lambda lg, c: (lg, c), logits, carries)
    carries = near_step(n_tiles - 1, logits, (1, 2), carries)

    per_block = GQ // CHAIN_W
    for i in range(NSA_PAIR):
        o_c, _, o_w = fronts[i]
        o_s = jnp.concatenate([_softmax_result(c, DH) for c in carries[i * per_block:(i + 1) * per_block]], axis=1)
        gates = gate_ref[i]
        out = gates[0:1] * o_c + gates[1:2] * o_s + gates[2:3] * o_w
        for pr in range(NSA_HPG // 2):
            stacked = jnp.concatenate([out[:, (2 * pr + s) * QB:(2 * pr + s + 1) * QB] for s in range(2)], axis=0)
            o_ref[i * QB:(i + 1) * QB, pr * 2 * DH:(pr + 1) * 2 * DH] = stacked.T.astype(o_ref.dtype)


def _nsa_fronts(qbs, qts, kc_ref, vct_ref, kw_ref, vwt_ref, tbc_hi_ref, tbc_lo_ref, tbw_ref, ssum_ref, ncp):
    blocks = range(len(qbs))

    row = lax.broadcasted_iota(jnp.int32, (ncp, 2 * HSEG), 0)
    lane = lax.broadcasted_iota(jnp.int32, (ncp, 2 * HSEG), 1)
    col = jnp.where((lane >= DH) & (lane < DH + 2 * CMP_TAB), (lane - DH) & (CMP_TAB - 1), CMP_TAB)
    keys = kc_ref[...]
    p_cs, o_cs = [], []
    for i in blocks:
        rel = row - ((QB // CMP_STRIDE) * qbs[i] - 16)
        lhs = jnp.where(jnp.minimum(rel, CMP_BAND) == col, 1.0, keys)
        rhs = jnp.concatenate([qts[i], tbc_hi_ref[...], tbc_lo_ref[...], jnp.zeros((2 * HSEG - DH - 2 * CMP_TAB, GQ), BF16)],
                              axis=0)
        s = jnp.dot(lhs.astype(BF16), rhs, preferred_element_type=F32)
        m = jnp.max(s, axis=0, keepdims=True)
        e = jnp.exp2(s - m)
        p_cs.append(e * jnp.where(m > 0.5 * NEG, 1.0 / jnp.sum(e, axis=0, keepdims=True), 0.0))
    for i in blocks:
        o_cs.append(jnp.dot(vct_ref[...], p_cs[i].astype(BF16), preferred_element_type=F32))

    ssum = ssum_ref[...]
    jj = lax.broadcasted_iota(jnp.int32, (LANES, QB), 0)
    jf = jj.astype(F32)
    causal, scores, sels = [], [], []
    for i in blocks:
        imp = p_cs[i][:, 0:QB]
        for hh in range(1, NSA_HPG):
            imp = imp + p_cs[i][:, hh * QB:(hh + 1) * QB]
        p_slc = sum(jnp.dot(ssum, part, preferred_element_type=F32) for part in _split3(imp))
        blk_cur = (qbs[i] * QB + lax.broadcasted_iota(jnp.int32, (LANES, QB), 1)) // SLC_BLOCK
        causal_b = jj <= blk_cur
        forced = (jj == 0) | (causal_b & (jj > blk_cur - SLC_LOCAL))
        causal.append(causal_b)
        scores.append(jnp.where(causal_b & jnp.logical_not(forced), p_slc, NEG))
        sels.append(jnp.where(forced, 1.0, 0.0))
    for _ in range(SLC_TOPK - 1 - SLC_LOCAL):
        for i in blocks:
            mx = jnp.max(scores[i], axis=0, keepdims=True)
            first = jnp.min(jnp.where(scores[i] == mx, jf, float(LANES)), axis=0, keepdims=True)
            hit = jf == first
            sels[i] = jnp.where(hit, 1.0, sels[i])
            scores[i] = jnp.where(hit, REMOVED, scores[i])
    unsels = [jnp.where(causal[i] & (sels[i] > 0.0), 0.0, NEG).astype(BF16) for i in blocks]

    o_ws = []
    for i in blocks:
        first = jnp.maximum(qbs[i] - WINDOW // QB, 0)
        start = pl.multiple_of(first * QB, QB)
        shift = pl.multiple_of(start + WINDOW - qbs[i] * QB, QB)
        s = (jnp.dot(kw_ref[pl.ds(start, N_WIN), 0:DH], qts[i], preferred_element_type=F32)
             + tbw_ref[pl.ds(shift, N_WIN), :])
        e = jnp.exp2(s - jnp.max(s, axis=0, keepdims=True))
        inv = 1.0 / jnp.sum(e, axis=0, keepdims=True)
        e = e.astype(BF16)
        o_w = jnp.zeros((DH, GQ), F32)
        for r in range(N_WIN // QB):
            o_w = o_w + jnp.dot(vwt_ref[first + r], e[r * QB:(r + 1) * QB], preferred_element_type=F32)
        o_ws.append(o_w * inv)

    return [(o_cs[i], unsels[i], o_ws[i]) for i in blocks]


def _nsa(qt, kc, vct, ksa, vst, kw, vwt, gates, tbc_hi, tbc_lo, tbs, tbw, ssum):
    B, G, nqb, _, _ = qt.shape
    T = ksa.shape[1]
    bg = lambda a: pl.BlockSpec((None, None) + a.shape[2:], lambda b, g, i: (b, g) + (0,) * (a.ndim - 2))
    lane_g = lambda w: pl.BlockSpec((None, T, w), lambda b, g, i: (b, 0, g))
    per_g = lambda a: pl.BlockSpec((None,) + a.shape[1:], lambda b, g, i: (g,) + (0,) * (a.ndim - 1))
    per_blk = lambda a: pl.BlockSpec((None, None, NSA_PAIR) + a.shape[3:],
                                     lambda b, g, i: (b, g, i) + (0,) * (a.ndim - 3))
    n_chains = NSA_PAIR * GQ // CHAIN_W
    return pl.pallas_call(
        _nsa_kernel,
        grid=(B, G, nqb // NSA_PAIR),
        in_specs=[per_blk(qt), bg(kc), bg(vct), lane_g(2 * HSEG), bg(vst), lane_g(HSEG), bg(vwt), per_blk(gates),
                  per_g(tbc_hi), per_g(tbc_lo), per_g(tbs), per_g(tbw),
                  pl.BlockSpec(ssum.shape, lambda b, g, i: (0, 0))],
        out_specs=pl.BlockSpec((None, NSA_PAIR * QB, NSA_HPG * DH), lambda b, g, i: (b, i, g)),
        out_shape=jax.ShapeDtypeStruct((B, nqb * QB, G * NSA_HPG * DH), BF16),
        scratch_shapes=[pltpu.VMEM((2, n_chains, SEL_TK, CHAIN_W), F32),
                        pltpu.VMEM((2, n_chains, SEL_TK, CHAIN_W), BF16)],
        compiler_params=_cparams(("arbitrary", "arbitrary", "arbitrary")),
        name="nsa_attention",
    )(qt, kc, vct, ksa, vst, kw, vwt, gates, tbc_hi, tbc_lo, tbs, tbw, ssum)


def _merge_kernel(x_ref, oa_ref, ob_ref, oc_ref, ga_ref, wm_ref, bm_ref, wo_ref, wout_ref, gf_ref, wgrp_ref, bgrp_ref,
                  wrt_ref, brt_ref, xo_ref, h2_ref, comb_ref):
    x = x_ref[...]
    h = _rms(x, ga_ref[...]).astype(BF16)
    mixed = None
    for i, o_ref in enumerate((oa_ref, ob_ref, oc_ref)):
        gate = jax.nn.sigmoid(jnp.dot(h, wm_ref[:, i * D_MODEL:(i + 1) * D_MODEL], preferred_element_type=F32)
                              + bm_ref[:, i * D_MODEL:(i + 1) * D_MODEL])
        term = gate * jnp.dot(o_ref[...], wo_ref[i], preferred_element_type=F32)
        mixed = term if mixed is None else mixed + term
    xn = x + jnp.dot(mixed.astype(BF16), wout_ref[...], preferred_element_type=F32)
    xo_ref[...] = xn
    h2 = _rms(xn, gf_ref[...])
    h2_ref[...] = h2.astype(BF16)

    lane = lax.broadcasted_iota(jnp.int32, (x.shape[0], LANES), 1).astype(F32)
    gl = jnp.dot(h2, wgrp_ref[...], preferred_element_type=F32, precision=lax.Precision.HIGHEST) + bgrp_ref[...]
    ge = jnp.exp(gl - jnp.max(gl, axis=-1, keepdims=True))
    gp = ge / jnp.sum(ge, axis=-1, keepdims=True)
    grp_top = jnp.max(gp, axis=-1, keepdims=True)
    g_idx = jnp.min(jnp.where(gp == grp_top, lane, float(LANES)), axis=-1, keepdims=True)
    el = jnp.dot(h2, wrt_ref[...], preferred_element_type=F32, precision=lax.Precision.HIGHEST) + brt_ref[...]
    in_grp = jnp.floor(lane * (1.0 / EPG)) == g_idx
    el = jnp.where(in_grp, el, NEG)
    ee = jnp.exp(el - jnp.max(el, axis=-1, keepdims=True))
    ep = jnp.where(in_grp, ee / jnp.sum(ee, axis=-1, keepdims=True), -1.0)
    p1 = jnp.max(ep, axis=-1, keepdims=True)
    i1 = jnp.min(jnp.where(ep == p1, lane, float(LANES)), axis=-1, keepdims=True)
    ep2 = jnp.where(lane == i1, -1.0, ep)
    p2 = jnp.max(ep2, axis=-1, keepdims=True)
    i2 = jnp.min(jnp.where(ep2 == p2, lane, float(LANES)), axis=-1, keepdims=True)
    den = p1 + p2
    comb_ref[...] = jnp.where(lane == i1, grp_top * p1 / den, 0.0) + jnp.where(lane == i2, grp_top * p2 / den, 0.0)


def _merge(x, oa, ob, oc, ga, wm, bm, wo, wout, gf, wgrp, bgrp, wrt, brt, tm):
    N, D = x.shape
    row = lambda w: pl.BlockSpec((tm, w), lambda i: (i, 0))
    full = lambda a: pl.BlockSpec(a.shape, lambda i: (0,) * a.ndim)
    return pl.pallas_call(
        _merge_kernel,
        grid=(N // tm,),
        in_specs=[row(D), row(512), row(512), row(512), full(ga), full(wm), full(bm), full(wo), full(wout), full(gf),
                  full(wgrp), full(bgrp), full(wrt), full(brt)],
        out_specs=[row(D), row(D), row(LANES)],
        out_shape=[jax.ShapeDtypeStruct((N, D), F32), jax.ShapeDtypeStruct((N, D), BF16),
                   jax.ShapeDtypeStruct((N, LANES), F32)],
        compiler_params=_cparams(("arbitrary",)),
        name="merge_router",
    )(x, oa, ob, oc, ga, wm, bm, wo, wout, gf, wgrp, bgrp, wrt, brt)


def _moe_kernel(x_ref, h_ref, comb_ref, wg_ref, wu_ref, wd_ref, gfin_ref, o_ref, acc_sc, *, final_norm):
    e = pl.program_id(1)

    @pl.when(e == 0)
    def _():
        acc_sc[...] = jnp.zeros_like(acc_sc)

    h = h_ref[...]
    a = jnp.dot(h, wg_ref[...], preferred_element_type=F32)
    u = jnp.dot(h, wu_ref[...], preferred_element_type=F32)
    c = _lane_pick(comb_ref[...], e)
    hid = (jax.nn.silu(a) * u * c).astype(BF16)
    acc_sc[...] += jnp.dot(hid, wd_ref[...], preferred_element_type=F32)

    @pl.when(e == pl.num_programs(1) - 1)
    def _():
        y = x_ref[...] + acc_sc[...]
        o_ref[...] = _rms(y, gfin_ref[...]) if final_norm else y


def _moe(x, h2, comb, wg, wu, wd, gfin, tm, final_norm):
    N, D = x.shape
    row = lambda w: pl.BlockSpec((tm, w), lambda i, e: (i, 0))
    return pl.pallas_call(
        functools.partial(_moe_kernel, final_norm=final_norm),
        grid=(N // tm, N_EXPERTS),
        in_specs=[row(D), row(D), row(LANES),
                  pl.BlockSpec((None, D, EXPERT_DIM), lambda i, e: (e, 0, 0)),
                  pl.BlockSpec((None, D, EXPERT_DIM), lambda i, e: (e, 0, 0)),
                  pl.BlockSpec((None, EXPERT_DIM, D), lambda i, e: (e, 0, 0)),
                  pl.BlockSpec((1, D), lambda i, e: (0, 0))],
        out_specs=row(D),
        out_shape=jax.ShapeDtypeStruct((N, D), F32),
        scratch_shapes=[pltpu.VMEM((tm, D), F32)],
        compiler_params=_cparams(("arbitrary", "arbitrary")),
        name="moe_experts",
    )(x, h2, comb, wg, wu, wd, gfin)


def _pad_cols(w, n):
    return jnp.pad(w, ((0, 0), (0, n - w.shape[1])))


def _pack_w1(w_in_l):
    o = np.cumsum([0, 512, 768, 24, MLA_Q_RANK, MLA_KV_RANK, MLA_ROPE, 1536, FOX_HEADS])
    nq, nkv, ng, cq, ckv, kr, fox, ff = (w_in_l[:, o[i]:o[i + 1]] for i in range(8))
    d_in = w_in_l.shape[0]
    half = MLA_ROPE // 2
    nkv = nkv.reshape(d_in, 3, 2, NSA_G, DH)
    segs = lambda w: _pad_feat(w, HSEG).reshape(d_in, -1)
    z = lambda n: jnp.zeros((d_in, n), w_in_l.dtype)
    x1, x2 = kr[:, :half], kr[:, half:]
    parts = {
        "NQ": nq, "NCMP": nkv[:, 0].reshape(d_in, -1), "KS": segs(nkv[:, 1, 0]), "VS": nkv[:, 1, 1].reshape(d_in, -1),
        "KW": segs(nkv[:, 2, 0]), "VW": nkv[:, 2, 1].reshape(d_in, -1), "NG": _pad_cols(ng, LANES), "CQ": cq, "CKV": ckv,
        "KRA": jnp.concatenate([z(MLA_NOPE), x1, x2, z(HSEG - MLA_QK)], axis=1),
        "KRB": jnp.concatenate([z(MLA_NOPE), x2, x1, z(HSEG - MLA_QK)], axis=1),
        "FQ": fox[:, :512], "FK": segs(fox[:, 512:1024].reshape(d_in, FOX_HEADS, DH)), "FV": fox[:, 1024:],
        "FF": _pad_cols(ff, LANES),
    }
    w1 = jnp.concatenate([parts[name] for name, _ in _SEG_WIDTHS], axis=1)
    return w1.astype(BF16), w_in_l[:, o[8]:]


def _pack_wuq(w_uq_l):
    w = w_uq_l.reshape(MLA_Q_RANK, MLA_HEADS, MLA_QK)
    half = MLA_ROPE // 2
    nope, r1, r2 = w[:, :, :MLA_NOPE], w[:, :, MLA_NOPE:MLA_NOPE + half], w[:, :, MLA_NOPE + half:]
    direct = _pad_feat(jnp.concatenate([nope, r1, r2], axis=-1), HSEG).reshape(MLA_Q_RANK, -1)
    swapped = _pad_feat(jnp.concatenate([jnp.zeros_like(nope), r2, r1], axis=-1), HSEG).reshape(MLA_Q_RANK, -1)
    return jnp.concatenate([direct, swapped], axis=1).astype(BF16)


def _rotary_tables():
    half = MLA_ROPE // 2
    inv_freq = ROPE_THETA ** (-jnp.arange(half, dtype=F32) / half)
    zeros = lambda n: jnp.zeros((n,), F32)
    freq = jnp.concatenate([zeros(MLA_NOPE), inv_freq, inv_freq, zeros(HSEG - MLA_QK)])
    sign = jnp.concatenate([zeros(MLA_NOPE), -jnp.ones((half,), F32), jnp.ones((half,), F32), zeros(HSEG - MLA_QK)])
    return jnp.stack([freq, sign])


def _pad_feat(a, n):
    return jnp.pad(a, ((0, 0),) * (a.ndim - 1) + ((0, n - a.shape[-1]),))


def _ssum_matrix(ncp):
    n = np.arange(ncp)[None, :]
    j = np.arange(LANES)[:, None]
    d = n - 4 * j
    m = np.where((d == 0) | (d == 4), 1.0, np.where((d >= 1) & (d <= 3), 2.0, 0.0))
    return jnp.asarray(m, BF16)


def kernel(x, positions, rel_bias, norm_attn, w_in, b_merge, cmp_pe, cmp_k_w1, cmp_k_b1, cmp_k_w2, cmp_v_w1, cmp_v_b1,
           cmp_v_w2, mla_q_norm, mla_kv_norm, mla_w_uq, mla_w_uk, mla_w_uv, fox_b_f, w_o_nsa, w_o_mla, w_o_fox, w_out,
           norm_ffn, moe_w_grp, moe_b_grp, moe_w_router, moe_b_router, moe_w_gate, moe_w_up, moe_w_down, norm_final):
    B, T, D = x.shape
    N = B * T
    n_sub = T // CMP_STRIDE
    nqb = T // QB
    tm_in = min(512, T)
    fl_tq, fl_hb = min(512, T), 2
    tm_merge = 512
    tm_moe = 1024

    tbc, tbs, tbw = _bias_tables(rel_bias)
    tbc = tbc.reshape(2, CMP_TAB, NSA_G, GQ).transpose(0, 2, 1, 3)
    tbc_hi, tbc_lo = tbc[0], tbc[1]
    tbs = tbs.reshape(4, SEL_TK, NSA_G, GQ).transpose(2, 0, 1, 3)
    tbw = tbw.reshape(WIN_TAB, NSA_G, GQ).transpose(1, 0, 2)
    ssum = _ssum_matrix(n_sub)
    rot = _rotary_tables()
    cum_place = np.zeros((3, LANES, FOX_HEADS * HSEG), np.float32)
    for n in range(3):
        cum_place[n, np.arange(FOX_HEADS), np.arange(FOX_HEADS) * HSEG + DH + n] = -1.0
    cum_place = jnp.asarray(cum_place, BF16)
    tri =jnp.asarray(np.tril(np.ones((tm_in, tm_in), np.float32)), BF16)
    pos3 = positions.reshape(B, T, 1)

    for l in range(DEPTH):
        w1, w_merge = _pack_w1(w_in[l])
        fb = _pad_cols(fox_b_f[l].reshape(1, -1), LANES)
        wuk = _pad_feat(mla_w_uk[l].reshape(MLA_KV_RANK, MLA_HEADS, MLA_NOPE), HSEG).reshape(MLA_KV_RANK, -1)
        qt_a, ncmp, ksa, vs_t, kw, vw_t, ng, mq_t, mk, mv_t, fq_t, fk, fv_t = _inproj(
            x, pos3, norm_attn[l].reshape(1, D), w1, _pack_wuq(mla_w_uq[l]), wuk.astype(BF16),
            mla_w_uv[l].astype(BF16), mla_q_norm[l].reshape(1, -1), mla_kv_norm[l].reshape(1, -1), fb, rot, tri,
            cum_place, tm_in)

        sub = ncmp.reshape(B, T, 2, NSA_G, DH).transpose(0, 3, 2, 1, 4).reshape(B, NSA_G, 2, n_sub, CMP_STRIDE * DH)
        pe = jnp.broadcast_to(cmp_pe[l].reshape(1, 2, CMP_STRIDE * DH), (2, 2, CMP_STRIDE * DH))
        cw1 = jnp.stack([cmp_k_w1[l], cmp_v_w1[l]]).reshape(2, 2, CMP_STRIDE * DH, CMP_HIDDEN).astype(BF16)
        cb1 = jnp.stack([cmp_k_b1[l], cmp_v_b1[l]]).reshape(2, 1, CMP_HIDDEN)
        cw2 = jnp.stack([cmp_k_w2[l], cmp_v_w2[l]]).astype(BF16)
        cmp = _compress(sub, pe, cw1, cb1, cw2)
        gates_t = ng[..., :3 * NSA_HEADS].reshape(B, nqb, QB, NSA_G, NSA_HPG, 3).transpose(0, 3, 1, 5, 4, 2)
        gates_t = jnp.pad(gates_t.reshape(B, NSA_G, nqb, 3, GQ), ((0, 0), (0, 0), (0, 0), (0, 5), (0, 0)))
        o_a = _nsa(qt_a, _pad_feat(cmp[:, :, 0], 2 * HSEG), cmp[:, :, 1].transpose(0, 1, 3, 2), ksa,
                   vs_t, kw, vw_t,
                   gates_t, tbc_hi, tbc_lo, tbs, tbw, ssum)
        out_a = o_a.reshape(N, NSA_HEADS * DH)

        o_b = _flash(mq_t, mk, mv_t, tq=fl_tq, hb=fl_hb)
        out_b = o_b.reshape(N, -1)

        o_c = _flash(fq_t, fk, fv_t, tq=fl_tq, hb=fl_hb)
        out_c = o_c.reshape(N, -1)

        wo = jnp.stack([w_o_nsa[l], w_o_mla[l], w_o_fox[l]]).astype(BF16)
        pad_b = lambda b, n: jnp.concatenate([b, jnp.full((LANES - n,), NEG, F32)]).reshape(1, LANES)
        x2, h2, comb = _merge(
            x.reshape(N, D), out_a, out_b, out_c, norm_attn[l].reshape(1, D), w_merge.astype(BF16),
            b_merge[l].reshape(1, -1), wo, w_out[l].astype(BF16), norm_ffn[l].reshape(1, D),
            _pad_cols(moe_w_grp[l], LANES), pad_b(moe_b_grp[l], N_GROUPS),
            _pad_cols(moe_w_router[l], LANES), pad_b(moe_b_router[l], N_EXPERTS), tm_merge)

        x = _moe(x2, h2, comb, moe_w_gate[l].astype(BF16), moe_w_up[l].astype(BF16), moe_w_down[l].astype(BF16),
                 norm_final.reshape(1, D), tm_moe, final_norm=(l == DEPTH - 1)).reshape(B, T, D)
    return x
```

```python
import functools
import math

import numpy as np
import jax
import jax.numpy as jnp
from jax import lax
from jax.experimental import pallas as pl
from jax.experimental.pallas import tpu as pltpu

F32 = jnp.float32
BF16 = jnp.bfloat16

D_MODEL = 1024
DEPTH = 2
EPS = 1e-6
NEG = -1e30
REMOVED = -3e38

NSA_HEADS = 8
NSA_G = 2
NSA_HPG = 4
DH = 64
CMP_STRIDE = 16
CMP_HIDDEN = 128
SLC_BLOCK = 64
SLC_TOPK = 16
SLC_LOCAL = 2
WINDOW = 512
QB = 128
GQ = NSA_HPG * QB
N_WIN = WINDOW + QB
WIN_TAB = WINDOW + N_WIN
SEL_TK = 256
NSA_PAIR = SEL_TK // QB
CHAIN_W = 256
SUM_ROWS = 16
CMP_BAND = 32
CMP_TAB = 64
MLA_HEADS = 8
MLA_Q_RANK = 256
MLA_KV_RANK = 128
MLA_NOPE = 64
MLA_ROPE = 32
MLA_QK = 96
ROPE_THETA = 10000.0
FOX_HEADS = 8
REL_BUCKETS = 32
REL_MAX_DIST = 128
N_GROUPS = 4
EPG = 8
N_EXPERTS = 32
EXPERT_DIM = 256

LANES = 128
LOG2E = 1.4426950408889634
VMEM_LIMIT = 56 * 1024 * 1024


def _bucket_thresholds():
    n = np.arange(0, 4 * REL_MAX_DIST)
    max_exact = REL_BUCKETS // 2
    lr = np.log(np.maximum(n, 1).astype(np.float32) / np.float32(max_exact)) / np.float32(math.log(REL_MAX_DIST / max_exact))
    large = max_exact + (lr.astype(np.float32) * np.float32(REL_BUCKETS - max_exact)).astype(np.int32)
    bucket = np.where(n < max_exact, n, np.minimum(large, REL_BUCKETS - 1))
    return [int(np.argmax(bucket >= b)) for b in range(1, REL_BUCKETS)]


BUCKET_THR = _bucket_thresholds()
assert BUCKET_THR[-1] <= REL_MAX_DIST


def _cparams(sem):
    return pltpu.CompilerParams(dimension_semantics=sem, vmem_limit_bytes=VMEM_LIMIT)


def _split3(x):
    hi = x.astype(BF16)
    r = x - hi.astype(F32)
    mid = r.astype(BF16)
    lo = (r - mid.astype(F32)).astype(BF16)
    return hi, mid, lo


def _lane_pick(x, idx):
    lane = lax.broadcasted_iota(jnp.int32, x.shape, x.ndim - 1)
    return jnp.sum(jnp.where(lane == idx, x, 0.0), axis=-1, keepdims=True)


def _online_softmax_steps(logits, vts, carries):
    stats = []
    for s, (m, _) in zip(logits, carries):
        m_new = jnp.maximum(m, jnp.max(s, axis=0, keepdims=True))
        stats.append((m_new, jnp.exp2(m - m_new), jnp.exp2(s - m_new).astype(BF16)))
    out = []
    for (m_new, alpha, p), vt, (_, acc) in zip(stats, vts, carries):
        out.append((m_new, alpha * acc + jnp.dot(vt, p, preferred_element_type=F32)))
    return tuple(out)


def _softmax_carry(dv, width):
    return jnp.full((1, width), NEG, F32), jnp.zeros((dv + SUM_ROWS, width), F32)


def _softmax_result(carry, dv):
    _, acc = carry
    return acc[0:dv] * (1.0 / acc[dv:dv + 1])


def _pipelined_softmax_loop(n_steps, qk_fn, vt_fn, carries, logit_sc, prob_sc):
    n_chains = len(carries)

    def value_products(kt, slot):
        return [jnp.dot(vt, prob_sc[slot, i], preferred_element_type=F32) for i, vt in enumerate(vt_fn(kt))]

    def keep(logits, slot):
        for i, s in enumerate(logits):
            logit_sc[slot, i] = s
        return tuple(jnp.max(s, axis=0, keepdims=True) for s in logits)

    def stage(kt, cur, nxt, state):
        carries, tile_max = state
        upcoming = qk_fn(kt + 1)
        pvs = value_products(jnp.maximum(kt - 1, 0), nxt)
        out = []
        for i, (pv, tm, (m, acc)) in enumerate(zip(pvs, tile_max, carries)):
            m_new = jnp.maximum(m, tm)
            prob_sc[cur, i] = jnp.exp2(logit_sc[cur, i] - m_new).astype(BF16)
            out.append((m_new, jnp.exp2(m - m_new) * (acc + pv)))
        return tuple(out), keep(upcoming, nxt)

    prob_sc[1] = jnp.zeros(prob_sc.shape[1:], BF16)
    state = (tuple(carries), keep(qk_fn(0), 0))
    state = lax.fori_loop(0, n_steps // 2, lambda i, c: stage(2 * i + 1, 1, 0, stage(2 * i, 0, 1, c)), state)
    odd = n_steps % 2
    carries, _ = lax.cond(odd == 1, lambda c: stage(n_steps - 1, 0, 1, c), lambda c: c, state)
    pvs = value_products(jnp.maximum(n_steps - 1, 0), 1 - odd)
    logits = tuple(logit_sc[odd, i] for i in range(n_chains))
    return logits, tuple((m, acc + pv) for (m, acc), pv in zip(carries, pvs))


def _bias_table_kernel(tab_ref, tbc_ref, tbs_ref, tbw_ref):
    h = pl.program_id(0)
    far = tab_ref[REL_BUCKETS - 1, h]

    def delta(dist):
        val = jnp.full(dist.shape, tab_ref[0, h], F32)
        for b in range(1, REL_BUCKETS):
            val = jnp.where(dist >= BUCKET_THR[b - 1], tab_ref[b, h], val)
        return (val - far) * LOG2E

    j = lax.broadcasted_iota(jnp.int32, (CMP_TAB, QB), 0)
    q = lax.broadcasted_iota(jnp.int32, (CMP_TAB, QB), 1)
    dist = q - CMP_STRIDE * j + (2 * QB - 2 * CMP_STRIDE + 1)
    band = jnp.where(j < CMP_BAND, jnp.where(dist >= 0, delta(dist), NEG), jnp.where(j == CMP_BAND, NEG, 0.0))
    band_hi = band.astype(BF16)
    tbc_ref[0] = band_hi
    tbc_ref[1] = (band - band_hi.astype(F32)).astype(BF16)
    c = lax.broadcasted_iota(jnp.int32, (SEL_TK, QB), 0)
    q = lax.broadcasted_iota(jnp.int32, (SEL_TK, QB), 1)
    for var, off in enumerate((SEL_TK, 0, QB)):
        dist = q + off - c
        tbs_ref[var] = jnp.where(dist >= 0, delta(dist), NEG)
    tbs_ref[3] = jnp.zeros((SEL_TK, QB), F32)
    c = lax.broadcasted_iota(jnp.int32, (WIN_TAB, QB), 0)
    q = lax.broadcasted_iota(jnp.int32, (WIN_TAB, QB), 1)
    dist = q + WINDOW - c
    tbw_ref[...] = jnp.where((dist >= 0) & (dist < WINDOW), delta(dist), NEG)


def _bias_tables(rel_bias):
    hq = NSA_HEADS * QB
    return pl.pallas_call(
        _bias_table_kernel,
        grid=(NSA_HEADS,),
        in_specs=[pl.BlockSpec(memory_space=pltpu.SMEM)],
        out_specs=[
            pl.BlockSpec((2, CMP_TAB, QB), lambda h: (0, 0, h)),
            pl.BlockSpec((4, SEL_TK, QB), lambda h: (0, 0, h)),
            pl.BlockSpec((WIN_TAB, QB), lambda h: (0, h)),
        ],
        out_shape=[
            jax.ShapeDtypeStruct((2, CMP_TAB, hq), BF16),
            jax.ShapeDtypeStruct((4, SEL_TK, hq), F32),
            jax.ShapeDtypeStruct((WIN_TAB, hq), F32),
        ],
        compiler_params=_cparams(("arbitrary",)),
        name="bias_tables",
    )(rel_bias)


HSEG = 128
_SEG_WIDTHS = (("NQ", 512), ("NCMP", 256), ("KS", NSA_G * HSEG), ("VS", 128), ("KW", NSA_G * HSEG), ("VW", 128),
               ("NG", 128), ("CQ", MLA_Q_RANK), ("CKV", MLA_KV_RANK), ("KRA", HSEG), ("KRB", HSEG), ("FQ", 512),
               ("FK", FOX_HEADS * HSEG), ("FV", 512), ("FF", 128))
SEG = {}
_off = 0
for _name, _w in _SEG_WIDTHS:
    SEG[_name] = (_off, _off + _w)
    _off += _w


def _rms(x, g):
    return x * lax.rsqrt(jnp.mean(x * x, axis=-1, keepdims=True) + EPS) * g


def _inproj_kernel(x_ref, pos_ref, g_ref, w1_ref, wuq_ref, wuk_ref, wuv_ref, qn_ref, kvn_ref, fb_ref, rot_ref,
                   tri_ref, place_ref, nq_ref, ncmp_ref, ksa_ref, vs_ref, kw_ref, vw_ref, ng_ref, mq_ref, mk_ref, mv_ref,
                   fq_ref, fk_ref, fv_ref, carry_ref):
    i = pl.program_id(1)
    tm = x_ref.shape[0]
    h = _rms(x_ref[...], g_ref[...]).astype(BF16)

    def seg(name):
        a, b = SEG[name]
        return jnp.dot(h, w1_ref[:, a:b], preferred_element_type=F32)

    nq = seg("NQ") * (DH ** -0.5 * LOG2E)
    for pair in range(NSA_HEADS // 2):
        pair_t = nq[:, pair * HSEG:(pair + 1) * HSEG].T.astype(BF16)
        for sub in range(2):
            g, hh = divmod(2 * pair + sub, NSA_HPG)
            for jb in range(tm // QB):
                nq_ref[g, jb, :, hh * QB:(hh + 1) * QB] = pair_t[sub * DH:(sub + 1) * DH, jb * QB:(jb + 1) * QB]
    ncmp_ref[...] = seg("NCMP").astype(BF16)
    def sum_rows(width):
        return jnp.where(lax.broadcasted_iota(jnp.int32, (SUM_ROWS, width), 0) == 0, 1.0, 0.0).astype(BF16)

    def put_value_tiles(ref, values, tile, with_sum):
        for pair in range(values.shape[1] // HSEG):
            pair_t = values[:, pair * HSEG:(pair + 1) * HSEG].astype(BF16).T
            for sub in range(2):
                for t in range(tm // tile):
                    ref[2 * pair + sub, t, 0:DH, :] = pair_t[sub * DH:(sub + 1) * DH, t * tile:(t + 1) * tile]
                    if with_sum:
                        ref[2 * pair + sub, t, DH:DH + SUM_ROWS, :] = sum_rows(tile)

    put_value_tiles(vs_ref, seg("VS"), SEL_TK, True)
    kw_ref[...] = seg("KW").astype(BF16)
    put_value_tiles(vw_ref, seg("VW"), QB, False)
    ng_ref[...] = jax.nn.sigmoid(seg("NG"))
    ks = seg("KS").astype(BF16)
    tok = i * tm + lax.broadcasted_iota(jnp.int32, (tm, LANES), 0)
    blk = (tok // SLC_BLOCK == lax.broadcasted_iota(jnp.int32, (tm, LANES), 1))
    blk = jnp.where(blk, 1.0, 0.0).astype(BF16)
    for g in range(NSA_G):
        ksa_ref[:, 2 * g * HSEG:(2 * g + 1) * HSEG] = ks[:, g * HSEG:(g + 1) * HSEG]
        ksa_ref[:, (2 * g + 1) * HSEG:(2 * g + 2) * HSEG] = blk

    ang = pos_ref[...].astype(F32) * rot_ref[0:1, :]
    cos = jnp.cos(ang)
    sin = jnp.sin(ang) * rot_ref[1:2, :]
    cos_h = jnp.concatenate([cos] * MLA_HEADS, axis=1)
    sin_h = jnp.concatenate([sin] * MLA_HEADS, axis=1)
    width = MLA_HEADS * HSEG
    c_q = _rms(seg("CQ"), qn_ref[...]).astype(BF16)
    q_a = jnp.dot(c_q, wuq_ref[:, 0:width], preferred_element_type=F32)
    q_b = jnp.dot(c_q, wuq_ref[:, width:2 * width], preferred_element_type=F32)
    mq = (q_a * cos_h + q_b * sin_h) * (MLA_QK ** -0.5 * LOG2E)
    for hd in range(MLA_HEADS):
        mq_ref[hd] = mq[:, hd * HSEG:(hd + 1) * HSEG].T.astype(BF16)
    c_kv = _rms(seg("CKV"), kvn_ref[...]).astype(BF16)
    k_rope = seg("KRA") * cos + seg("KRB") * sin
    k_nope = jnp.dot(c_kv, wuk_ref[...], preferred_element_type=F32)
    mk_ref[...] = (k_nope + jnp.concatenate([k_rope] * MLA_HEADS, axis=1)).astype(BF16)
    put_value_tiles(mv_ref, jnp.dot(c_kv, wuv_ref[...], preferred_element_type=F32), tm, True)

    fq = seg("FQ") * (DH ** -0.5 * LOG2E)
    ones_rows = jnp.where(lax.broadcasted_iota(jnp.int32, (HSEG - DH, tm), 0) < 3, 1.0, 0.0).astype(BF16)
    for pair in range(FOX_HEADS // 2):
        pair_t = fq[:, pair * HSEG:(pair + 1) * HSEG].T.astype(BF16)
        for sub in range(2):
            fq_ref[2 * pair + sub, 0:DH, :] = pair_t[sub * DH:(sub + 1) * DH]
            fq_ref[2 * pair + sub, DH:HSEG, :] = ones_rows
    put_value_tiles(fv_ref, seg("FV"), tm, True)
    log_f = jax.nn.log_sigmoid(seg("FF") + fb_ref[...])

    @pl.when(i == 0)
    def _():
        carry_ref[...] = jnp.zeros_like(carry_ref)

    tri = tri_ref[...]
    cum = carry_ref[...]
    for part in _split3(log_f):
        cum = cum + jnp.dot(tri, part, preferred_element_type=F32)
    carry_ref[...] = cum[tm - 1:tm, :]
    fk = seg("FK")
    for n, part in enumerate(_split3(cum * LOG2E)):
        fk = fk + jnp.dot(part, place_ref[n], preferred_element_type=F32)
    fk_ref[...] = fk.astype(BF16)


def _inproj(x, pos, g, w1, wuq, wuk, wuv, qn, kvn, fb, rot, tri, place, tm):
    B, T, D = x.shape
    row = lambda w: pl.BlockSpec((None, tm, w), lambda b, i: (b, i, 0))
    full = lambda a: pl.BlockSpec(a.shape, lambda b, i: (0,) * a.ndim)
    heads_t = lambda h: pl.BlockSpec((None, h, HSEG, tm), lambda b, i: (b, 0, 0, i))
    tok = lambda w, dt=BF16: (row(w), jax.ShapeDtypeStruct((B, T, w), dt))
    q_t = lambda h: (heads_t(h), jax.ShapeDtypeStruct((B, h, HSEG, T), BF16))
    v_t = lambda h, tile, extra: (
        pl.BlockSpec((None, h, tm // tile, DH + extra, tile), lambda b, i: (b, 0, i, 0, 0)),
        jax.ShapeDtypeStruct((B, h, T // tile, DH + extra, tile), BF16))
    outs = [
        (pl.BlockSpec((None, NSA_G, tm // QB, DH, GQ), lambda b, i: (b, 0, i, 0, 0)),
         jax.ShapeDtypeStruct((B, NSA_G, T // QB, DH, GQ), BF16)),
        tok(256), tok(2 * NSA_G * HSEG), v_t(NSA_G, SEL_TK, SUM_ROWS), tok(NSA_G * HSEG), v_t(NSA_G, QB, 0),
        tok(128, F32),
        q_t(MLA_HEADS), tok(MLA_HEADS * HSEG), v_t(MLA_HEADS, tm, SUM_ROWS),
        q_t(FOX_HEADS), tok(FOX_HEADS * HSEG), v_t(FOX_HEADS, tm, SUM_ROWS),
    ]
    return pl.pallas_call(
        _inproj_kernel,
        grid=(B, T // tm),
        in_specs=[row(D), row(1), full(g), full(w1), full(wuq), full(wuk), full(wuv), full(qn), full(kvn), full(fb),
                  full(rot), full(tri), full(place)],
        out_specs=[spec for spec, _ in outs],
        out_shape=[shape for _, shape in outs],
        scratch_shapes=[pltpu.VMEM((1, LANES), F32)],
        compiler_params=_cparams(("arbitrary", "arbitrary")),
        name="inproj",
    )(x, pos, g, w1, wuq, wuk, wuv, qn, kvn, fb, rot, tri, place)


def _flash_kernel(qt_ref, k_ref, vt_ref, o_ref, logit_sc, prob_sc, *, tq, hb):
    qi = pl.program_id(2)
    dv = vt_ref.shape[2] - SUM_ROWS
    chains = [(j, c) for j in range(hb) for c in range(tq // CHAIN_W)]
    qts = [qt_ref[j, :, c * CHAIN_W:(c + 1) * CHAIN_W] for j, c in chains]

    def qk(kt):
        st = pl.multiple_of(kt * tq, tq)
        return [jnp.dot(k_ref[pl.ds(st, tq), j * HSEG:(j + 1) * HSEG], qt, preferred_element_type=F32)
                for (j, _), qt in zip(chains, qts)]

    init = tuple(_softmax_carry(dv, CHAIN_W) for _ in chains)
    diag, carries = _pipelined_softmax_loop(qi, qk, lambda kt: [vt_ref[j, kt] for j, _ in chains], init,
                                            logit_sc, prob_sc)
    kpos = lax.broadcasted_iota(jnp.int32, (tq, CHAIN_W), 0)
    qpos = lax.broadcasted_iota(jnp.int32, (tq, CHAIN_W), 1)
    logits = [jnp.where(kpos <= qpos + c * CHAIN_W, s, NEG) for (_, c), s in zip(chains, diag)]
    final = _online_softmax_steps(logits, [vt_ref[j, qi] for j, _ in chains], carries)
    results = {jc: _softmax_result(carry, dv) for jc, carry in zip(chains, final)}
    for c in range(tq // CHAIN_W):
        stacked = jnp.concatenate([results[(j, c)] for j in range(hb)], axis=0)
        o_ref[c * CHAIN_W:(c + 1) * CHAIN_W, :] = stacked.T.astype(o_ref.dtype)


def _flash(qt, k, vt, *, tq, hb):
    B, H, dk, T = qt.shape
    dv = vt.shape[3] - SUM_ROWS
    return pl.pallas_call(
        functools.partial(_flash_kernel, tq=tq, hb=hb),
        grid=(B, H // hb, T // tq),
        in_specs=[
            pl.BlockSpec((None, hb, dk, tq), lambda b, h, i: (b, h, 0, i)),
            pl.BlockSpec((None, T, hb * HSEG), lambda b, h, i: (b, 0, h)),
            pl.BlockSpec((None, hb, T // tq, dv + SUM_ROWS, tq), lambda b, h, i: (b, h, 0, 0, 0)),
        ],
        out_specs=pl.BlockSpec((None, tq, hb * dv), lambda b, h, i: (b, i, h)),
        out_shape=jax.ShapeDtypeStruct((B, T, H * dv), BF16),
        scratch_shapes=[pltpu.VMEM((2, hb * tq // CHAIN_W, tq, CHAIN_W), F32),
                        pltpu.VMEM((2, hb * tq // CHAIN_W, tq, CHAIN_W), BF16)],
        compiler_params=_cparams(("arbitrary", "arbitrary", "arbitrary")),
        name="flash_attention",
    )(qt, k, vt)


def _compress_kernel(sub_ref, pe_ref, w1_ref, b1_ref, w2_ref, o_ref):
    sub = sub_ref[...].astype(F32)
    n_sub = sub.shape[0]
    a = jnp.dot((sub + pe_ref[0:1, :]).astype(BF16), w1_ref[0], preferred_element_type=F32)
    b = jnp.dot((sub + pe_ref[1:2, :]).astype(BF16), w1_ref[1], preferred_element_type=F32)
    hid = jax.nn.gelu(a + pltpu.roll(b, n_sub - 1, 0) + b1_ref[...])
    o_ref[...] = jnp.dot(hid.astype(BF16), w2_ref[...], preferred_element_type=F32).astype(o_ref.dtype)


def _compress(sub, pe, w1, b1, w2):
    B, G, _, n_sub, _ = sub.shape
    return pl.pallas_call(
        _compress_kernel,
        grid=(B, G, 2),
        in_specs=[
            pl.BlockSpec((None, None, None, n_sub, 1024), lambda b, g, s: (b, g, s, 0, 0)),
            pl.BlockSpec((None, 2, 1024), lambda b, g, s: (s, 0, 0)),
            pl.BlockSpec((None, 2, 1024, CMP_HIDDEN), lambda b, g, s: (s, 0, 0, 0)),
            pl.BlockSpec((None, 1, CMP_HIDDEN), lambda b, g, s: (s, 0, 0)),
            pl.BlockSpec((None, CMP_HIDDEN, DH), lambda b, g, s: (s, 0, 0)),
        ],
        out_specs=pl.BlockSpec((None, None, None, n_sub, DH), lambda b, g, s: (b, g, s, 0, 0)),
        out_shape=jax.ShapeDtypeStruct((B, G, 2, n_sub, DH), BF16),
        compiler_params=_cparams(("arbitrary", "arbitrary", "arbitrary")),
        name="nsa_compress",
    )(sub, pe, w1, b1, w2)


def _nsa_kernel(qt_ref, kc_ref, vct_ref, ksa_ref, vst_ref, kw_ref, vwt_ref, gate_ref, tbc_hi_ref, tbc_lo_ref,
                tbs_ref, tbw_ref, ssum_ref, o_ref, logit_sc, prob_sc):
    pair = pl.program_id(2)
    ncp = kc_ref.shape[0]
    fronts = _nsa_fronts([NSA_PAIR * pair + i for i in range(NSA_PAIR)], [qt_ref[i] for i in range(NSA_PAIR)],
                         kc_ref, vct_ref, kw_ref, vwt_ref, tbc_hi_ref, tbc_lo_ref, tbw_ref, ssum_ref, ncp)

    n_tiles = pair + 1
    chains = [(i, c * CHAIN_W, (c + 1) * CHAIN_W) for i in range(NSA_PAIR) for c in range(GQ // CHAIN_W)]
    qas = []
    for i, a, b in chains:
        qt, unsel = qt_ref[i], fronts[i][1]
        qa = jnp.concatenate([qt, jnp.zeros((HSEG - DH, GQ), BF16), jnp.concatenate([unsel] * NSA_HPG, axis=1)],
                             axis=0)
        qas.append(qa[:, a:b])

    def sel_qk(kt):
        start = pl.multiple_of(kt * SEL_TK, SEL_TK)
        return [jnp.dot(ksa_ref[pl.ds(start, SEL_TK), :], qc, preferred_element_type=F32)
                for qc in qas]

    def sel_vt(kt):
        return [vst_ref[kt]] * len(chains)

    def near_step(kt, logits, variants, carries):
        logits = [s + tbs_ref[variants[i], :, a:b] for s, (i, a, b) in zip(logits, chains)]
        return _online_softmax_steps(logits, sel_vt(kt), carries)

    carries = tuple(_softmax_carry(DH, CHAIN_W) for _ in chains)
    n_far = jnp.maximum(n_tiles - 2, 0)
    logits, carries = _pipelined_softmax_loop(n_far, sel_qk, sel_vt, carries, logit_sc, prob_sc)
    logits, carries = lax.cond(
        n_tiles >= 2,
        lambda lg, c: (tuple(sel_qk(n_tiles - 1)), near_step(n_tiles - 2, lg, (0, 3), c)),
        lambda lg, c: (lg, c), logits, carries)
    carries = near_step(n_tiles - 1, logits, (1, 2), carries)

    per_block = GQ // CHAIN_W
    for i in range(NSA_PAIR):
        o_c, _, o_w = fronts[i]
        o_s = jnp.concatenate([_softmax_result(c, DH) for c in carries[i * per_block:(i + 1) * per_block]], axis=1)
        gates = gate_ref[i]
        out = gates[0:1] * o_c + gates[1:2] * o_s + gates[2:3] * o_w
        for pr in range(NSA_HPG // 2):
            stacked = jnp.concatenate([out[:, (2 * pr + s) * QB:(2 * pr + s + 1) * QB] for s in range(2)], axis=0)
            o_ref[i * QB:(i + 1) * QB, pr * 2 * DH:(pr + 1) * 2 * DH] = stacked.T.astype(o_ref.dtype)


def _nsa_fronts(qbs, qts, kc_ref, vct_ref, kw_ref, vwt_ref, tbc_hi_ref, tbc_lo_ref, tbw_ref, ssum_ref, ncp):
    blocks = range(len(qbs))

    row = lax.broadcasted_iota(jnp.int32, (ncp, 2 * HSEG), 0)
    lane = lax.broadcasted_iota(jnp.int32, (ncp, 2 * HSEG), 1)
    col = jnp.where((lane >= DH) & (lane < DH + 2 * CMP_TAB), (lane - DH) & (CMP_TAB - 1), CMP_TAB)
    keys = kc_ref[...]
    p_cs, o_cs = [], []
    for i in blocks:
        rel = row - ((QB // CMP_STRIDE) * qbs[i] - 16)
        lhs = jnp.where(jnp.minimum(rel, CMP_BAND) == col, 1.0, keys)
        rhs = jnp.concatenate([qts[i], tbc_hi_ref[...], tbc_lo_ref[...], jnp.zeros((2 * HSEG - DH - 2 * CMP_TAB, GQ), BF16)],
                              axis=0)
        s = jnp.dot(lhs.astype(BF16), rhs, preferred_element_type=F32)
        m = jnp.max(s, axis=0, keepdims=True)
        e = jnp.exp2(s - m)
        p_cs.append(e * jnp.where(m > 0.5 * NEG, 1.0 / jnp.sum(e, axis=0, keepdims=True), 0.0))
    for i in blocks:
        o_cs.append(jnp.dot(vct_ref[...], p_cs[i].astype(BF16), preferred_element_type=F32))

    ssum = ssum_ref[...]
    jj = lax.broadcasted_iota(jnp.int32, (LANES, QB), 0)
    jf = jj.astype(F32)
    causal, scores, sels = [], [], []
    for i in blocks:
        imp = p_cs[i][:, 0:QB]
        for hh in range(1, NSA_HPG):
            imp = imp + p_cs[i][:, hh * QB:(hh + 1) * QB]
        p_slc = sum(jnp.dot(ssum, part, preferred_element_type=F32) for part in _split3(imp))
        blk_cur = (qbs[i] * QB + lax.broadcasted_iota(jnp.int32, (LANES, QB), 1)) // SLC_BLOCK
        causal_b = jj <= blk_cur
        forced = (jj == 0) | (causal_b & (jj > blk_cur - SLC_LOCAL))
        causal.append(causal_b)
        scores.append(jnp.where(causal_b & jnp.logical_not(forced), p_slc, NEG))
        sels.append(jnp.where(forced, 1.0, 0.0))
    for _ in range(SLC_TOPK - 1 - SLC_LOCAL):
        for i in blocks:
            mx = jnp.max(scores[i], axis=0, keepdims=True)
            first = jnp.min(jnp.where(scores[i] == mx, jf, float(LANES)), axis=0, keepdims=True)
            hit = jf == first
            sels[i] = jnp.where(hit, 1.0, sels[i])
            scores[i] = jnp.where(hit, REMOVED, scores[i])
    unsels = [jnp.where(causal[i] & (sels[i] > 0.0), 0.0, NEG).astype(BF16) for i in blocks]

    o_ws = []
    for i in blocks:
        first = jnp.maximum(qbs[i] - WINDOW // QB, 0)
        start = pl.multiple_of(first * QB, QB)
        shift = pl.multiple_of(start + WINDOW - qbs[i] * QB, QB)
        s = (jnp.dot(kw_ref[pl.ds(start, N_WIN), 0:DH], qts[i], preferred_element_type=F32)
             + tbw_ref[pl.ds(shift, N_WIN), :])
        e = jnp.exp2(s - jnp.max(s, axis=0, keepdims=True))
        inv = 1.0 / jnp.sum(e, axis=0, keepdims=True)
        e = e.astype(BF16)
        o_w = jnp.zeros((DH, GQ), F32)
        for r in range(N_WIN // QB):
            o_w = o_w + jnp.dot(vwt_ref[first + r], e[r * QB:(r + 1) * QB], preferred_element_type=F32)
        o_ws.append(o_w * inv)

    return [(o_cs[i], unsels[i], o_ws[i]) for i in blocks]


def _nsa(qt, kc, vct, ksa, vst, kw, vwt, gates, tbc_hi, tbc_lo, tbs, tbw, ssum):
    B, G, nqb, _, _ = qt.shape
    T = ksa.shape[1]
    bg = lambda a: pl.BlockSpec((None, None) + a.shape[2:], lambda b, g, i: (b, g) + (0,) * (a.ndim - 2))
    lane_g = lambda w: pl.BlockSpec((None, T, w), lambda b, g, i: (b, 0, g))
    per_g = lambda a: pl.BlockSpec((None,) + a.shape[1:], lambda b, g, i: (g,) + (0,) * (a.ndim - 1))
    per_blk = lambda a: pl.BlockSpec((None, None, NSA_PAIR) + a.shape[3:],
                                     lambda b, g, i: (b, g, i) + (0,) * (a.ndim - 3))
    n_chains = NSA_PAIR * GQ // CHAIN_W
    return pl.pallas_call(
        _nsa_kernel,
        grid=(B, G, nqb // NSA_PAIR),
        in_specs=[per_blk(qt), bg(kc), bg(vct), lane_g(2 * HSEG), bg(vst), lane_g(HSEG), bg(vwt), per_blk(gates),
                  per_g(tbc_hi), per_g(tbc_lo), per_g(tbs), per_g(tbw),
                  pl.BlockSpec(ssum.shape, lambda b, g, i: (0, 0))],
        out_specs=pl.BlockSpec((None, NSA_PAIR * QB, NSA_HPG * DH), lambda b, g, i: (b, i, g)),
        out_shape=jax.ShapeDtypeStruct((B, nqb * QB, G * NSA_HPG * DH), BF16),
        scratch_shapes=[pltpu.VMEM((2, n_chains, SEL_TK, CHAIN_W), F32),
                        pltpu.VMEM((2, n_chains, SEL_TK, CHAIN_W), BF16)],
        compiler_params=_cparams(("arbitrary", "arbitrary", "arbitrary")),
        name="nsa_attention",
    )(qt, kc, vct, ksa, vst, kw, vwt, gates, tbc_hi, tbc_lo, tbs, tbw, ssum)


def _merge_kernel(x_ref, oa_ref, ob_ref, oc_ref, ga_ref, wm_ref, bm_ref, wo_ref, wout_ref, gf_ref, wgrp_ref, bgrp_ref,
                  wrt_ref, brt_ref, xo_ref, h2_ref, comb_ref):
    x = x_ref[...]
    h = _rms(x, ga_ref[...]).astype(BF16)
    mixed = None
    for i, o_ref in enumerate((oa_ref, ob_ref, oc_ref)):
        gate = jax.nn.sigmoid(jnp.dot(h, wm_ref[:, i * D_MODEL:(i + 1) * D_MODEL], preferred_element_type=F32)
                              + bm_ref[:, i * D_MODEL:(i + 1) * D_MODEL])
        term = gate * jnp.dot(o_ref[...], wo_ref[i], preferred_element_type=F32)
        mixed = term if mixed is None else mixed + term
    xn = x + jnp.dot(mixed.astype(BF16), wout_ref[...], preferred_element_type=F32)
    xo_ref[...] = xn
    h2 = _rms(xn, gf_ref[...])
    h2_ref[...] = h2.astype(BF16)

    lane = lax.broadcasted_iota(jnp.int32, (x.shape[0], LANES), 1).astype(F32)
    gl = jnp.dot(h2, wgrp_ref[...], preferred_element_type=F32, precision=lax.Precision.HIGHEST) + bgrp_ref[...]
    ge = jnp.exp(gl - jnp.max(gl, axis=-1, keepdims=True))
    gp = ge / jnp.sum(ge, axis=-1, keepdims=True)
    grp_top = jnp.max(gp, axis=-1, keepdims=True)
    g_idx = jnp.min(jnp.where(gp == grp_top, lane, float(LANES)), axis=-1, keepdims=True)
    el = jnp.dot(h2, wrt_ref[...], preferred_element_type=F32, precision=lax.Precision.HIGHEST) + brt_ref[...]
    in_grp = jnp.floor(lane * (1.0 / EPG)) == g_idx
    el = jnp.where(in_grp, el, NEG)
    ee = jnp.exp(el - jnp.max(el, axis=-1, keepdims=True))
    ep = jnp.where(in_grp, ee / jnp.sum(ee, axis=-1, keepdims=True), -1.0)
    p1 = jnp.max(ep, axis=-1, keepdims=True)
    i1 = jnp.min(jnp.where(ep == p1, lane, float(LANES)), axis=-1, keepdims=True)
    ep2 = jnp.where(lane == i1, -1.0, ep)
    p2 = jnp.max(ep2, axis=-1, keepdims=True)
    i2 = jnp.min(jnp.where(ep2 == p2, lane, float(LANES)), axis=-1, keepdims=True)
    den = p1 + p2
    comb_ref[...] = jnp.where(lane == i1, grp_top * p1 / den, 0.0) + jnp.where(lane == i2, grp_top * p2 / den, 0.0)


def _merge(x, oa, ob, oc, ga, wm, bm, wo, wout, gf, wgrp, bgrp, wrt, brt, tm):
    N, D = x.shape
    row = lambda w: pl.BlockSpec((tm, w), lambda i: (i, 0))
    full = lambda a: pl.BlockSpec(a.shape, lambda i: (0,) * a.ndim)
    return pl.pallas_call(
        _merge_kernel,
        grid=(N // tm,),
        in_specs=[row(D), row(512), row(512), row(512), full(ga), full(wm), full(bm), full(wo), full(wout), full(gf),
                  full(wgrp), full(bgrp), full(wrt), full(brt)],
        out_specs=[row(D), row(D), row(LANES)],
        out_shape=[jax.ShapeDtypeStruct((N, D), F32), jax.ShapeDtypeStruct((N, D), BF16),
                   jax.ShapeDtypeStruct((N, LANES), F32)],
        compiler_params=_cparams(("arbitrary",)),
        name="merge_router",
    )(x, oa, ob, oc, ga, wm, bm, wo, wout, gf, wgrp, bgrp, wrt, brt)


def _moe_kernel(x_ref, h_ref, comb_ref, wg_ref, wu_ref, wd_ref, gfin_ref, o_ref, acc_sc, *, final_norm):
    e = pl.program_id(1)

    @pl.when(e == 0)
    def _():
        acc_sc[...] = jnp.zeros_like(acc_sc)

    h = h_ref[...]
    a = jnp.dot(h, wg_ref[...], preferred_element_type=F32)
    u = jnp.dot(h, wu_ref[...], preferred_element_type=F32)
    c = _lane_pick(comb_ref[...], e)
    hid = (jax.nn.silu(a) * u * c).astype(BF16)
    acc_sc[...] += jnp.dot(hid, wd_ref[...], preferred_element_type=F32)

    @pl.when(e == pl.num_programs(1) - 1)
    def _():
        y = x_ref[...] + acc_sc[...]
        o_ref[...] = _rms(y, gfin_ref[...]) if final_norm else y


def _moe(x, h2, comb, wg, wu, wd, gfin, tm, final_norm):
    N, D = x.shape
    row = lambda w: pl.BlockSpec((tm, w), lambda i, e: (i, 0))
    return pl.pallas_call(
        functools.partial(_moe_kernel, final_norm=final_norm),
        grid=(N // tm, N_EXPERTS),
        in_specs=[row(D), row(D), row(LANES),
                  pl.BlockSpec((None, D, EXPERT_DIM), lambda i, e: (e, 0, 0)),
                  pl.BlockSpec((None, D, EXPERT_DIM), lambda i, e: (e, 0, 0)),
                  pl.BlockSpec((None, EXPERT_DIM, D), lambda i, e: (e, 0, 0)),
                  pl.BlockSpec((1, D), lambda i, e: (0, 0))],
        out_specs=row(D),
        out_shape=jax.ShapeDtypeStruct((N, D), F32),
        scratch_shapes=[pltpu.VMEM((tm, D), F32)],
        compiler_params=_cparams(("arbitrary", "arbitrary")),
        name="moe_experts",
    )(x, h2, comb, wg, wu, wd, gfin)


def _pad_cols(w, n):
    return jnp.pad(w, ((0, 0), (0, n - w.shape[1])))


def _pack_w1(w_in_l):
    o = np.cumsum([0, 512, 768, 24, MLA_Q_RANK, MLA_KV_RANK, MLA_ROPE, 1536, FOX_HEADS])
    nq, nkv, ng, cq, ckv, kr, fox, ff = (w_in_l[:, o[i]:o[i + 1]] for i in range(8))
    d_in = w_in_l.shape[0]
    half = MLA_ROPE // 2
    nkv = nkv.reshape(d_in, 3, 2, NSA_G, DH)
    segs = lambda w: _pad_feat(w, HSEG).reshape(d_in, -1)
    z = lambda n: jnp.zeros((d_in, n), w_in_l.dtype)
    x1, x2 = kr[:, :half], kr[:, half:]
    parts = {
        "NQ": nq, "NCMP": nkv[:, 0].reshape(d_in, -1), "KS": segs(nkv[:, 1, 0]), "VS": nkv[:, 1, 1].reshape(d_in, -1),
        "KW": segs(nkv[:, 2, 0]), "VW": nkv[:, 2, 1].reshape(d_in, -1), "NG": _pad_cols(ng, LANES), "CQ": cq, "CKV": ckv,
        "KRA": jnp.concatenate([z(MLA_NOPE), x1, x2, z(HSEG - MLA_QK)], axis=1),
        "KRB": jnp.concatenate([z(MLA_NOPE), x2, x1, z(HSEG - MLA_QK)], axis=1),
        "FQ": fox[:, :512], "FK": segs(fox[:, 512:1024].reshape(d_in, FOX_HEADS, DH)), "FV": fox[:, 1024:],
        "FF": _pad_cols(ff, LANES),
    }
    w1 = jnp.concatenate([parts[name] for name, _ in _SEG_WIDTHS], axis=1)
    return w1.astype(BF16), w_in_l[:, o[8]:]


def _pack_wuq(w_uq_l):
    w = w_uq_l.reshape(MLA_Q_RANK, MLA_HEADS, MLA_QK)
    half = MLA_ROPE // 2
    nope, r1, r2 = w[:, :, :MLA_NOPE], w[:, :, MLA_NOPE:MLA_NOPE + half], w[:, :, MLA_NOPE + half:]
    direct = _pad_feat(jnp.concatenate([nope, r1, r2], axis=-1), HSEG).reshape(MLA_Q_RANK, -1)
    swapped = _pad_feat(jnp.concatenate([jnp.zeros_like(nope), r2, r1], axis=-1), HSEG).reshape(MLA_Q_RANK, -1)
    return jnp.concatenate([direct, swapped], axis=1).astype(BF16)


def _rotary_tables():
    half = MLA_ROPE // 2
    inv_freq = ROPE_THETA ** (-jnp.arange(half, dtype=F32) / half)
    zeros = lambda n: jnp.zeros((n,), F32)
    freq = jnp.concatenate([zeros(MLA_NOPE), inv_freq, inv_freq, zeros(HSEG - MLA_QK)])
    sign = jnp.concatenate([zeros(MLA_NOPE), -jnp.ones((half,), F32), jnp.ones((half,), F32), zeros(HSEG - MLA_QK)])
    return jnp.stack([freq, sign])


def _pad_feat(a, n):
    return jnp.pad(a, ((0, 0),) * (a.ndim - 1) + ((0, n - a.shape[-1]),))


def _ssum_matrix(ncp):
    n = np.arange(ncp)[None, :]
    j = np.arange(LANES)[:, None]
    d = n - 4 * j
    m = np.where((d == 0) | (d == 4), 1.0, np.where((d >= 1) & (d <= 3), 2.0, 0.0))
    return jnp.asarray(m, BF16)


def kernel(x, positions, rel_bias, norm_attn, w_in, b_merge, cmp_pe, cmp_k_w1, cmp_k_b1, cmp_k_w2, cmp_v_w1, cmp_v_b1,
           cmp_v_w2, mla_q_norm, mla_kv_norm, mla_w_uq, mla_w_uk, mla_w_uv, fox_b_f, w_o_nsa, w_o_mla, w_o_fox, w_out,
           norm_ffn, moe_w_grp, moe_b_grp, moe_w_router, moe_b_router, moe_w_gate, moe_w_up, moe_w_down, norm_final):
    B, T, D = x.shape
    N = B * T
    n_sub = T // CMP_STRIDE
    nqb = T // QB
    tm_in = min(512, T)
    fl_tq, fl_hb = min(512, T), 4
    tm_merge = 512
    tm_moe = 1024

    tbc, tbs, tbw = _bias_tables(rel_bias)
    tbc = tbc.reshape(2, CMP_TAB, NSA_G, GQ).transpose(0, 2, 1, 3)
    tbc_hi, tbc_lo = tbc[0], tbc[1]
    tbs = tbs.reshape(4, SEL_TK, NSA_G, GQ).transpose(2, 0, 1, 3)
    tbw = tbw.reshape(WIN_TAB, NSA_G, GQ).transpose(1, 0, 2)
    ssum = _ssum_matrix(n_sub)
    rot = _rotary_tables()
    cum_place = np.zeros((3, LANES, FOX_HEADS * HSEG), np.float32)
    for n in range(3):
        cum_place[n, np.arange(FOX_HEADS), np.arange(FOX_HEADS) * HSEG + DH + n] = -1.0
    cum_place = jnp.asarray(cum_place, BF16)
    tri =jnp.asarray(np.tril(np.ones((tm_in, tm_in), np.float32)), BF16)
    pos3 = positions.reshape(B, T, 1)

    for l in range(DEPTH):
        w1, w_merge = _pack_w1(w_in[l])
        fb = _pad_cols(fox_b_f[l].reshape(1, -1), LANES)
        wuk = _pad_feat(mla_w_uk[l].reshape(MLA_KV_RANK, MLA_HEADS, MLA_NOPE), HSEG).reshape(MLA_KV_RANK, -1)
        qt_a, ncmp, ksa, vs_t, kw, vw_t, ng, mq_t, mk, mv_t, fq_t, fk, fv_t = _inproj(
            x, pos3, norm_attn[l].reshape(1, D), w1, _pack_wuq(mla_w_uq[l]), wuk.astype(BF16),
            mla_w_uv[l].astype(BF16), mla_q_norm[l].reshape(1, -1), mla_kv_norm[l].reshape(1, -1), fb, rot, tri,
            cum_place, tm_in)

        sub = ncmp.reshape(B, T, 2, NSA_G, DH).transpose(0, 3, 2, 1, 4).reshape(B, NSA_G, 2, n_sub, CMP_STRIDE * DH)
        pe = jnp.broadcast_to(cmp_pe[l].reshape(1, 2, CMP_STRIDE * DH), (2, 2, CMP_STRIDE * DH))
        cw1 = jnp.stack([cmp_k_w1[l], cmp_v_w1[l]]).reshape(2, 2, CMP_STRIDE * DH, CMP_HIDDEN).astype(BF16)
        cb1 = jnp.stack([cmp_k_b1[l], cmp_v_b1[l]]).reshape(2, 1, CMP_HIDDEN)
        cw2 = jnp.stack([cmp_k_w2[l], cmp_v_w2[l]]).astype(BF16)
        cmp = _compress(sub, pe, cw1, cb1, cw2)
        gates_t = ng[..., :3 * NSA_HEADS].reshape(B, nqb, QB, NSA_G, NSA_HPG, 3).transpose(0, 3, 1, 5, 4, 2)
        gates_t = jnp.pad(gates_t.reshape(B, NSA_G, nqb, 3, GQ), ((0, 0), (0, 0), (0, 0), (0, 5), (0, 0)))
        o_a = _nsa(qt_a, _pad_feat(cmp[:, :, 0], 2 * HSEG), cmp[:, :, 1].transpose(0, 1, 3, 2), ksa,
                   vs_t, kw, vw_t,
                   gates_t, tbc_hi, tbc_lo, tbs, tbw, ssum)
        out_a = o_a.reshape(N, NSA_HEADS * DH)

        o_b = _flash(mq_t, mk, mv_t, tq=fl_tq, hb=fl_hb)
        out_b = o_b.reshape(N, -1)

        o_c = _flash(fq_t, fk, fv_t, tq=fl_tq, hb=fl_hb)
        out_c = o_c.reshape(N, -1)

        wo = jnp.stack([w_o_nsa[l], w_o_mla[l], w_o_fox[l]]).astype(BF16)
        pad_b = lambda b, n: jnp.concatenate([b, jnp.full((LANES - n,), NEG, F32)]).reshape(1, LANES)
        x2, h2, comb = _merge(
            x.reshape(N, D), out_a, out_b, out_c, norm_attn[l].reshape(1, D), w_merge.astype(BF16),
            b_merge[l].reshape(1, -1), wo, w_out[l].astype(BF16), norm_ffn[l].reshape(1, D),
            _pad_cols(moe_w_grp[l], LANES), pad_b(moe_b_grp[l], N_GROUPS),
            _pad_cols(moe_w_router[l], LANES), pad_b(moe_b_router[l], N_EXPERTS), tm_merge)

        x = _moe(x2, h2, comb, moe_w_gate[l].astype(BF16), moe_w_up[l].astype(BF16), moe_w_down[l].astype(BF16),
                 norm_final.reshape(1, D), tm_moe, final_norm=(l == DEPTH - 1)).reshape(B, T, D)
    return x
```
